```python
import math
import jax, jax.numpy as jnp
from jax import lax
import numpy as np

D_MODEL = 2048
BATCH = 2
SEQ = 4096
DEPTH = 1

ATT_HEAD_DIM = 128
ATT_HEADS_PER_GROUP = 4
ATT_PATTERNS = ((128, 1), (512, 4), (2048, 16))
ATT_HEADS = ATT_HEADS_PER_GROUP * len(ATT_PATTERNS)
ATT_WIDTH = ATT_HEADS * ATT_HEAD_DIM
ATT_OUT_WIDTH = ATT_HEADS_PER_GROUP * ATT_HEAD_DIM
ATT_BLOCK = 128

M_HEADS = 4
M_QK_DIM = 128
M_V_DIM = 256
M_QK_WIDTH = M_HEADS * M_QK_DIM
M_V_WIDTH = M_HEADS * M_V_DIM
M_CONV = 4
M_CHUNK = 64

N_BRANCHES = 2
IN_PROJ_SPLITS = (ATT_WIDTH, ATT_WIDTH, ATT_WIDTH, 2 * M_QK_WIDTH, M_V_WIDTH, M_V_WIDTH, 2 * M_HEADS, N_BRANCHES * D_MODEL)
IN_PROJ_WIDTH = sum(IN_PROJ_SPLITS)

N_GROUPS = 4
EXPERTS_PER_GROUP = 8
N_EXPERTS = N_GROUPS * EXPERTS_PER_GROUP
TOP_K = 2
D_FF_EXPERT = 1408
MOE_BLOCK = 128

DEEPNORM_ALPHA = (2 * DEPTH) ** 0.25
DEEPNORM_BETA = (8 * DEPTH) ** -0.25
LN_EPS = 1e-5

kernel_name = 'hybrid_dilated_attn_mlstm_hmoe_deepnorm'


def alibi_slopes(n):
    def geometric(k):
        start = 2.0 ** (-8.0 / k)
        return [start ** (i + 1) for i in range(k)]
    c = 2 ** int(math.floor(math.log2(n)))
    s = geometric(c) if c == n else geometric(c) + geometric(2 * c)[0::2][: n - c]
    return np.array(sorted(s, reverse=True), dtype=np.float32)


def layer_norm(x, g, b):
    xf = x.astype(jnp.float32)
    mu = jnp.mean(xf, axis=-1, keepdims=True)
    var = jnp.mean(jnp.square(xf - mu), axis=-1, keepdims=True)
    y = (xf - mu) * lax.rsqrt(var + LN_EPS) * g.astype(jnp.float32) + b.astype(jnp.float32)
    return y.astype(x.dtype)


def dilated_window_attention(q, k, v, slopes, window, dilation):
    B, S, H, Dh = q.shape
    r = dilation
    n_win = window // r
    L = S // r
    nb = -(-L // ATT_BLOCK)
    Lp = nb * ATT_BLOCK

    def to_sub(t):
        t = t.reshape(B, L, r, H, Dh).transpose(0, 2, 3, 1, 4)
        return jnp.pad(t, ((0, 0), (0, 0), (0, 0), (0, Lp - L), (0, 0)))

    def band(t):
        prev = jnp.pad(t, ((0, 0), (0, 0), (0, 0), (ATT_BLOCK, 0), (0, 0)))[:, :, :, :Lp]
        return jnp.concatenate([prev.reshape(B, r, H, nb, ATT_BLOCK, Dh),
                                t.reshape(B, r, H, nb, ATT_BLOCK, Dh)], axis=4)

    qb = to_sub(q).reshape(B, r, H, nb, ATT_BLOCK, Dh)
    kb = band(to_sub(k))
    vb = band(to_sub(v))
    scores = jnp.einsum('bphnqd,bphnkd->bphnqk', qb, kb,
                        preferred_element_type=jnp.float32) * (Dh ** -0.5)
    qi = jnp.arange(ATT_BLOCK)[:, None]
    ki = jnp.arange(2 * ATT_BLOCK)[None, :]
    delta = ATT_BLOCK + qi - ki
    key_u = (jnp.arange(nb)[:, None, None] - 1) * ATT_BLOCK + ki[None]
    valid = (delta >= 0) & (delta <= n_win) & (key_u >= 0)
    alibi = -jnp.asarray(slopes, jnp.float32)[:, None, None, None] * (delta * r).astype(jnp.float32)
    scores = jnp.where(valid, scores + alibi, -jnp.inf)
    m = jnp.max(scores, axis=-1, keepdims=True)
    p = jnp.exp(scores - m)
    l = jnp.sum(p, axis=-1, keepdims=True)
    o = jnp.einsum('bphnqk,bphnkd->bphnqd', p, vb.astype(jnp.float32)) / l
    lse = (m + jnp.log(l))[..., 0]
    o = o.reshape(B, r, H, Lp, Dh)[:, :, :, :L].transpose(0, 3, 1, 2, 4).reshape(B, S, H, Dh)
    lse = lse.reshape(B, r, H, Lp)[..., :L].transpose(0, 3, 1, 2).reshape(B, S, H)
    return o, lse


def causal_depthwise_conv(x, w, b):
    K, C = w.shape
    y = lax.conv_general_dilated(x, w[:, None, :].astype(x.dtype), window_strides=(1,),
                                 padding=[(K - 1, 0)], dimension_numbers=('NWC', 'WIO', 'NWC'),
                                 feature_group_count=C)
    return y + b


def mlstm_chunkwise(q, k, v, i_pre, f_pre):
    B, NH, S, dqk = q.shape
    dv = v.shape[-1]
    nc = S // M_CHUNK
    f32 = jnp.float32

    def chunked(t):
        t = t.astype(f32).reshape(B, NH, nc, M_CHUNK, *t.shape[3:])
        return jnp.moveaxis(t, 2, 0)

    qc = chunked(q) * (dqk ** -0.5)
    kc = chunked(k)
    vc = chunked(v)
    ic = chunked(i_pre)
    bc = lax.cumsum(chunked(jax.nn.log_sigmoid(f_pre.astype(f32))), axis=3)
    causal = jnp.tril(jnp.ones((M_CHUNK, M_CHUNK), dtype=bool))

    def step(carry, xs):
        C, n, m = carry
        qj, kj, vj, ij, bj = xs
        dmat = jnp.where(causal, bj[..., :, None] - bj[..., None, :] + ij[..., None, :], -jnp.inf)
        inter = bj + m[..., None]
        m_t = jnp.maximum(inter, jnp.max(dmat, axis=-1))
        w_intra = jnp.exp(dmat - m_t[..., None])
        w_inter = jnp.exp(inter - m_t)
        qk = jnp.einsum('bhld,bhsd->bhls', qj, kj) * w_intra
        num = w_inter[..., None] * jnp.einsum('bhld,bhdv->bhlv', qj, C) + jnp.einsum('bhls,bhsv->bhlv', qk, vj)
        den = w_inter * jnp.einsum('bhld,bhd->bhl', qj, n) + jnp.sum(qk, axis=-1)
        h = num / jnp.maximum(jnp.abs(den), jnp.exp(-m_t))[..., None]
        b_last = bj[..., -1]
        w_log = b_last[..., None] - bj + ij
        m_new = jnp.maximum(b_last + m, jnp.max(w_log, axis=-1))
        wk = jnp.exp(w_log - m_new[..., None])
        decay = jnp.exp(b_last + m - m_new)
        C_new = decay[..., None, None] * C + jnp.einsum('bhs,bhsd,bhsv->bhdv', wk, kj, vj)
        n_new = decay[..., None] * n + jnp.einsum('bhs,bhsd->bhd', wk, kj)
        return (C_new, n_new, m_new), h

    init = (jnp.zeros((B, NH, dqk, dv), f32), jnp.zeros((B, NH, dqk), f32), jnp.zeros((B, NH), f32))
    _, hs = lax.scan(step, init, (qc, kc, vc, ic, bc))
    return jnp.moveaxis(hs, 0, 2).reshape(B, NH, S, dv)


def token_mixer(h, w_in, conv_w, conv_b, if_bias, norm_w, w_proj_att, w_proj_mlstm, w_out):
    B, S, _ = h.shape
    proj = h @ w_in
    cuts = [int(c) for c in np.cumsum(IN_PROJ_SPLITS)[:-1]]
    aq, ak, av, mqk, mv, mo, mif, gate_pre = jnp.split(proj, cuts, axis=-1)

    aq = aq.reshape(B, S, ATT_HEADS, ATT_HEAD_DIM)
    ak = ak.reshape(B, S, ATT_HEADS, ATT_HEAD_DIM)
    av = av.reshape(B, S, ATT_HEADS, ATT_HEAD_DIM)
    slopes = alibi_slopes(ATT_HEADS)
    outs, lses = [], []
    for g, (window, dilation) in enumerate(ATT_PATTERNS):
        hs = slice(g * ATT_HEADS_PER_GROUP, (g + 1) * ATT_HEADS_PER_GROUP)
        o, lse = dilated_window_attention(aq[:, :, hs], ak[:, :, hs], av[:, :, hs], slopes[hs], window, dilation)
        outs.append(o)
        lses.append(lse)
    mix_w = jax.nn.softmax(jnp.stack(lses, axis=0), axis=0)
    att = jnp.einsum('gbsh,gbshd->bshd', mix_w, jnp.stack(outs, axis=0))
    att = att.reshape(B, S, ATT_OUT_WIDTH).astype(h.dtype)

    mqk = jax.nn.silu(causal_depthwise_conv(mqk, conv_w, conv_b))
    mq, mk = jnp.split(mqk, 2, axis=-1)

    def heads(t, d):
        return t.reshape(B, S, M_HEADS, d).transpose(0, 2, 1, 3)

    mif = (mif + if_bias).reshape(B, S, 2, M_HEADS).transpose(2, 0, 3, 1)
    hm = mlstm_chunkwise(heads(mq, M_QK_DIM), heads(mk, M_QK_DIM), heads(mv, M_V_DIM), mif[0], mif[1])
    mu = jnp.mean(hm, axis=-1, keepdims=True)
    var = jnp.mean(jnp.square(hm - mu), axis=-1, keepdims=True)
    hm = ((hm - mu) * lax.rsqrt(var + LN_EPS)).transpose(0, 2, 1, 3) * norm_w
    hm = hm.reshape(B, S, M_V_WIDTH).astype(h.dtype) * jax.nn.sigmoid(mo)

    g_att, g_mlstm = jnp.split(jax.nn.sigmoid(gate_pre), 2, axis=-1)
    merged = g_att * (att @ w_proj_att) + g_mlstm * (hm @ w_proj_mlstm)
    return merged @ w_out


def hierarchical_moe(h, w_router_group, b_router_group, w_router_expert, b_router_expert, w_gate, w_up, w_down):
    B, S, D = h.shape
    T = B * S
    xf = h.reshape(T, D)
    g_logits = (xf @ w_router_group).astype(jnp.float32) + b_router_group.astype(jnp.float32)
    grp = jnp.argmax(g_logits, axis=-1)
    g_w = jnp.take_along_axis(jax.nn.softmax(g_logits, axis=-1), grp[:, None], axis=1)[:, 0]
    e_all = ((xf @ w_router_expert).astype(jnp.float32) + b_router_expert.astype(jnp.float32)).reshape(T, N_GROUPS, EXPERTS_PER_GROUP)
    e_logits = jnp.take_along_axis(e_all, grp[:, None, None], axis=1)[:, 0]
    top_v, top_i = lax.top_k(e_logits, TOP_K)
    weight = g_w[:, None] * jax.nn.softmax(top_v, axis=-1)
    expert = grp[:, None].astype(jnp.int32) * EXPERTS_PER_GROUP + top_i.astype(jnp.int32)

    M = T * TOP_K
    e_flat = expert.reshape(M)
    tok_flat = jnp.repeat(jnp.arange(T, dtype=jnp.int32), TOP_K)
    w_flat = weight.reshape(M)
    order = jnp.argsort(e_flat)
    e_sorted = e_flat[order]
    counts = jnp.bincount(e_flat, length=N_EXPERTS)
    padded = (counts + MOE_BLOCK - 1) // MOE_BLOCK * MOE_BLOCK
    start = jnp.cumsum(counts) - counts
    pstart = jnp.cumsum(padded) - padded
    pend = pstart + padded
    dest = pstart[e_sorted] + jnp.arange(M, dtype=jnp.int32) - start[e_sorted]
    NB = M // MOE_BLOCK + N_EXPERTS
    slot_tok = jnp.zeros((NB * MOE_BLOCK,), jnp.int32).at[dest].set(tok_flat[order])
    slot_w = jnp.zeros((NB * MOE_BLOCK,), jnp.float32).at[dest].set(w_flat[order])
    block_start = jnp.arange(NB, dtype=jnp.int32) * MOE_BLOCK
    block_expert = jnp.minimum(jnp.sum(pend[None, :] <= block_start[:, None], axis=-1), N_EXPERTS - 1)

    def expert_block(args):
        toks, e = args
        xb = xf[toks]
        hb = jax.nn.silu(xb @ w_gate[e]) * (xb @ w_up[e])
        return hb @ w_down[e]

    yb = lax.map(expert_block, (slot_tok.reshape(NB, MOE_BLOCK), block_expert))
    y = jax.ops.segment_sum(yb.reshape(NB * MOE_BLOCK, D) * slot_w[:, None].astype(yb.dtype),
                            slot_tok, num_segments=T)
    return y.reshape(B, S, D)


def setup_inputs(seed: int = 0) -> dict:
    key = jax.random.key(seed)
    ks = jax.random.split(key, 24)
    f32 = jnp.float32

    def nrm(k, shape, scale):
        return jax.random.normal(k, shape, f32) * scale

    Ld = DEPTH
    col_scale = jnp.concatenate([
        jnp.ones((2 * ATT_WIDTH,), f32),
        jnp.full((ATT_WIDTH,), DEEPNORM_BETA, f32),
        jnp.ones((2 * M_QK_WIDTH,), f32),
        jnp.full((M_V_WIDTH,), DEEPNORM_BETA, f32),
        jnp.ones((M_V_WIDTH + 2 * M_HEADS + N_BRANCHES * D_MODEL,), f32)])
    m_if_bias = jnp.concatenate([
        nrm(ks[6], (Ld, M_HEADS), 0.1),
        jnp.linspace(3.0, 6.0, M_HEADS, dtype=f32)[None, :] + nrm(ks[7], (Ld, M_HEADS), 0.1)], axis=-1)
    return {
        'x': nrm(ks[0], (BATCH, SEQ, D_MODEL), 1.0),
        'ln_in_g': 1.0 + nrm(ks[1], (D_MODEL,), 0.02),
        'ln_in_b': nrm(ks[2], (D_MODEL,), 0.02),
        'w_in': nrm(ks[3], (Ld, D_MODEL, IN_PROJ_WIDTH), D_MODEL ** -0.5) * col_scale,
        'm_conv_w': nrm(ks[4], (Ld, M_CONV, 2 * M_QK_WIDTH), M_CONV ** -0.5),
        'm_conv_b': nrm(ks[5], (Ld, 2 * M_QK_WIDTH), 0.02),
        'm_if_bias': m_if_bias,
        'm_norm_w': 1.0 + nrm(ks[8], (Ld, M_HEADS, M_V_DIM), 0.02),
        'w_proj_att': nrm(ks[9], (Ld, ATT_OUT_WIDTH, D_MODEL), ATT_OUT_WIDTH ** -0.5),
        'w_proj_mlstm': nrm(ks[10], (Ld, M_V_WIDTH, D_MODEL), M_V_WIDTH ** -0.5),
        'w_out': nrm(ks[11], (Ld, D_MODEL, D_MODEL), D_MODEL ** -0.5) * DEEPNORM_BETA,
        'ln1_g': 1.0 + nrm(ks[12], (Ld, D_MODEL), 0.02),
        'ln1_b': nrm(ks[13], (Ld, D_MODEL), 0.02),
        'w_router_group': nrm(ks[14], (Ld, D_MODEL, N_GROUPS), D_MODEL ** -0.5),
        'b_router_group': nrm(ks[15], (Ld, N_GROUPS), 0.01),
        'w_router_expert': nrm(ks[16], (Ld, D_MODEL, N_EXPERTS), D_MODEL ** -0.5),
        'b_router_expert': nrm(ks[17], (Ld, N_EXPERTS), 0.01),
        'w_gate': nrm(ks[18], (Ld, N_EXPERTS, D_MODEL, D_FF_EXPERT), D_MODEL ** -0.5),
        'w_up': nrm(ks[19], (Ld, N_EXPERTS, D_MODEL, D_FF_EXPERT), D_MODEL ** -0.5),
        'w_down': nrm(ks[20], (Ld, N_EXPERTS, D_FF_EXPERT, D_MODEL), D_FF_EXPERT ** -0.5) * DEEPNORM_BETA,
        'ln2_g': 1.0 + nrm(ks[21], (Ld, D_MODEL), 0.02),
        'ln2_b': nrm(ks[22], (Ld, D_MODEL), 0.02),
    }


def reference(x, ln_in_g, ln_in_b, w_in, m_conv_w, m_conv_b, m_if_bias, m_norm_w,
              w_proj_att, w_proj_mlstm, w_out, ln1_g, ln1_b,
              w_router_group, b_router_group, w_router_expert, b_router_expert,
              w_gate, w_up, w_down, ln2_g, ln2_b):
    h = layer_norm(x, ln_in_g, ln_in_b)
    for l in range(DEPTH):
        y = token_mixer(h, w_in[l], m_conv_w[l], m_conv_b[l], m_if_bias[l], m_norm_w[l],
                        w_proj_att[l], w_proj_mlstm[l], w_out[l])
        h = layer_norm(DEEPNORM_ALPHA * h + y, ln1_g[l], ln1_b[l])
        y = hierarchical_moe(h, w_router_group[l], b_router_group[l], w_router_expert[l],
                             b_router_expert[l], w_gate[l], w_up[l], w_down[l])
        h = layer_norm(DEEPNORM_ALPHA * h + y, ln2_g[l], ln2_b[l])
    return h
```

```python
import functools
import math

import numpy as np
import jax
import jax.numpy as jnp
from jax import lax
from jax.experimental import pallas as pl
from jax.experimental.pallas import tpu as pltpu

F32 = jnp.float32
BF16 = jnp.bfloat16

D_MODEL = 2048
ATT_HEAD_DIM = 128
ATT_HEADS_PER_GROUP = 4
ATT_PATTERNS = ((128, 1), (512, 4), (2048, 16))
ATT_HEADS = ATT_HEADS_PER_GROUP * len(ATT_PATTERNS)
ATT_WIDTH = ATT_HEADS * ATT_HEAD_DIM
ATT_OUT_WIDTH = ATT_HEADS_PER_GROUP * ATT_HEAD_DIM
ATT_BLOCK = 128
ATT_SUPER = 2048

M_HEADS = 4
M_QK_DIM = 128
M_V_DIM = 256
M_QK_WIDTH = M_HEADS * M_QK_DIM
M_V_WIDTH = M_HEADS * M_V_DIM
M_CONV = 4
M_CHUNK = 128
M_MLSTM_COLS = 512

N_BRANCHES = 2
IN_PROJ_SPLITS = (ATT_WIDTH, ATT_WIDTH, ATT_WIDTH, 2 * M_QK_WIDTH, M_V_WIDTH, M_V_WIDTH,
                  2 * M_HEADS, N_BRANCHES * D_MODEL)
COL_AQ = 0
COL_AK = ATT_WIDTH
COL_AV = 2 * ATT_WIDTH
COL_MQK = 3 * ATT_WIDTH
COL_MV = COL_MQK + 2 * M_QK_WIDTH
COL_MO = COL_MV + M_V_WIDTH
COL_MIF = COL_MO + M_V_WIDTH
COL_GATE = COL_MIF + 2 * M_HEADS
PROJ_A_WIDTH = COL_MIF

N_GROUPS = 4
EXPERTS_PER_GROUP = 8
N_EXPERTS = N_GROUPS * EXPERTS_PER_GROUP
TOP_K = 2
D_FF_EXPERT = 1408
MOE_SUB = 256
MOE_SUPER = 1024
MOE_FT = 128
ROUTE_LANES = 128

DEPTH = 1
DEEPNORM_ALPHA = (2 * DEPTH) ** 0.25
LN_EPS = 1e-5
NEG = -1e30

VMEM_LIMIT = 56 * 1024 * 1024


def _alibi_slopes(n):
    def geometric(k):
        start = 2.0 ** (-8.0 / k)
        return [start ** (i + 1) for i in range(k)]
    c = 2 ** int(math.floor(math.log2(n)))
    s = geometric(c) if c == n else geometric(c) + geometric(2 * c)[0::2][: n - c]
    return np.array(sorted(s, reverse=True), dtype=np.float32)


def _params(*sem):
    return pltpu.CompilerParams(dimension_semantics=sem, vmem_limit_bytes=VMEM_LIMIT)


def _layer_norm_rows(z, g, b):
    mu = jnp.mean(z, axis=-1, keepdims=True)
    zc = z - mu
    var = jnp.mean(zc * zc, axis=-1, keepdims=True)
    return zc * lax.rsqrt(var + LN_EPS) * g + b


def _sigmoid(x):
    return 1.0 / (1.0 + jnp.exp(-x))


def _ln_in_kernel(x_ref, g_ref, b_ref, hf_ref, hb_ref):
    y = _layer_norm_rows(x_ref[...], g_ref[...], b_ref[...])
    hf_ref[...] = y
    hb_ref[...] = y.astype(BF16)


def _ln_in(x2, g, b, tm=256):
    T, D = x2.shape
    return pl.pallas_call(
        _ln_in_kernel,
        grid=(T // tm,),
        in_specs=[pl.BlockSpec((tm, D), lambda i: (i, 0)),
                  pl.BlockSpec((1, D), lambda i: (0, 0)),
                  pl.BlockSpec((1, D), lambda i: (0, 0))],
        out_specs=[pl.BlockSpec((tm, D), lambda i: (i, 0)),
                   pl.BlockSpec((tm, D), lambda i: (i, 0))],
        out_shape=[jax.ShapeDtypeStruct((T, D), F32), jax.ShapeDtypeStruct((T, D), BF16)],
        compiler_params=_params("parallel"),
        name="ln_in",
    )(x2, g.reshape(1, D), b.reshape(1, D))


def _mm_kernel(a_ref, w_ref, o_ref, wb_ref):
    @pl.when(pl.program_id(1) == 0)
    def _():
        wb_ref[...] = w_ref[...].astype(BF16)

    o_ref[...] = jnp.dot(a_ref[...], wb_ref[...], preferred_element_type=F32).astype(o_ref.dtype)


def _matmul(a, w, n_cols, tm, tn, out_dtype, name):
    T, K = a.shape
    return pl.pallas_call(
        _mm_kernel,
        grid=(n_cols // tn, T // tm),
        in_specs=[pl.BlockSpec((tm, K), lambda j, i: (i, 0)),
                  pl.BlockSpec((K, tn), lambda j, i: (0, j))],
        out_specs=pl.BlockSpec((tm, tn), lambda j, i: (i, j)),
        out_shape=jax.ShapeDtypeStruct((T, n_cols), out_dtype),
        scratch_shapes=[pltpu.VMEM((K, tn), BF16)],
        compiler_params=_params("parallel", "arbitrary"),
        name=name,
    )(a, w)


def _attn_block(r, slope_r, prev_bias, q_ref, kc_ref, vc_ref, kp_ref, vp_ref, base, base_prev,
                g, acc_ref, m_ref, l_ref):
    def rows(start):
        return pl.ds(start, ATT_BLOCK, r) if r > 1 else pl.ds(start, ATT_BLOCK)

    q = q_ref[0, rows(base), :].astype(BF16)
    kc = kc_ref[0, rows(base), :].astype(BF16)
    vc = vc_ref[0, rows(base), :].astype(BF16)
    kp = kp_ref[0, rows(base_prev), :].astype(BF16)
    vp = vp_ref[0, rows(base_prev), :].astype(BF16)
    dn = (((1,), (1,)), ((), ()))
    scale = ATT_HEAD_DIM ** -0.5
    sc = lax.dot_general(q, kc, dn, preferred_element_type=F32) * scale
    sp = lax.dot_general(q, kp, dn, preferred_element_type=F32) * scale
    qi = lax.broadcasted_iota(jnp.int32, (ATT_BLOCK, ATT_BLOCK), 0)
    ki = lax.broadcasted_iota(jnp.int32, (ATT_BLOCK, ATT_BLOCK), 1)
    dlt = (qi - ki).astype(F32)
    sc = jnp.where(ki <= qi, sc - slope_r * dlt, NEG)
    sp = jnp.where(ki >= qi, sp - slope_r * (dlt + float(ATT_BLOCK)) + prev_bias, NEG)
    m = jnp.maximum(jnp.max(sc, axis=-1, keepdims=True), jnp.max(sp, axis=-1, keepdims=True))
    pc = jnp.exp(sc - m)
    pp = jnp.exp(sp - m)
    l = jnp.sum(pc, axis=-1, keepdims=True) + jnp.sum(pp, axis=-1, keepdims=True)
    acc = (jnp.dot(pc.astype(BF16), vc, preferred_element_type=F32)
           + jnp.dot(pp.astype(BF16), vp, preferred_element_type=F32))
    acc_ref[g, rows(base), :] = acc
    m_ref[g, rows(base), :] = jnp.broadcast_to(m, (ATT_BLOCK, ATT_HEAD_DIM))
    l_ref[g, rows(base), :] = jnp.broadcast_to(l, (ATT_BLOCK, ATT_HEAD_DIM))


def _attn_kernel(slopes_ref, *refs):
    ng = len(ATT_PATTERNS)
    q_refs = refs[0:ng]
    kc_refs = refs[ng:2 * ng]
    vc_refs = refs[2 * ng:3 * ng]
    kp_refs = refs[3 * ng:4 * ng]
    vp_refs = refs[4 * ng:5 * ng]
    o_ref = refs[5 * ng]
    acc_ref, m_ref, l_ref = refs[5 * ng + 1:]
    s = pl.program_id(1)
    h = pl.program_id(2)
    prev_bias = jnp.where(s > 0, 0.0, NEG).astype(F32)

    for g, (window, r) in enumerate(ATT_PATTERNS):
        assert window // r == ATT_BLOCK
        nblk = ATT_SUPER // (ATT_BLOCK * r)
        slope_r = slopes_ref[g, h] * float(r)
        common = dict(r=r, slope_r=slope_r, g=g, acc_ref=acc_ref, m_ref=m_ref, l_ref=l_ref,
                      q_ref=q_refs[g], kc_ref=kc_refs[g], vc_ref=vc_refs[g])

        def first(p, c, common=common, g=g):
            _attn_block(prev_bias=prev_bias, kp_ref=kp_refs[g], vp_ref=vp_refs[g],
                        base=p, base_prev=p, **common)
            return c
        lax.fori_loop(0, r, first, 0)

        if nblk > 1:
            def rest(idx, c, common=common, g=g, r=r, nblk=nblk):
                p = idx // (nblk - 1)
                j = idx % (nblk - 1) + 1
                base = p + j * (ATT_BLOCK * r)
                _attn_block(prev_bias=jnp.float32(0.0), kp_ref=kc_refs[g], vp_ref=vc_refs[g],
                            base=base, base_prev=base - ATT_BLOCK * r, **common)
                return c
            lax.fori_loop(0, r * (nblk - 1), rest, 0)

    ch = 256
    def merge(i, c):
        rs = pl.ds(pl.multiple_of(i * ch, ch), ch)
        ms = [m_ref[g, rs, :] for g in range(ng)]
        mx = functools.reduce(jnp.maximum, ms)
        num = jnp.zeros((ch, ATT_HEAD_DIM), F32)
        den = jnp.zeros((ch, ATT_HEAD_DIM), F32)
        for g in range(ng):
            w = jnp.exp(ms[g] - mx)
            num = num + w * acc_ref[g, rs, :]
            den = den + w * l_ref[g, rs, :]
        o_ref[0, rs, :] = (num / den).astype(o_ref.dtype)
        return c
    lax.fori_loop(0, ATT_SUPER // ch, merge, 0)


def _attention(proj_a, B, S):
    ng = len(ATT_PATTERNS)
    nsb = S // ATT_SUPER
    cb = ATT_HEAD_DIM
    slopes = jnp.asarray(_alibi_slopes(ATT_HEADS).reshape(ng, ATT_HEADS_PER_GROUP))

    def cur_spec(col0, g):
        return pl.BlockSpec((1, ATT_SUPER, cb),
                            lambda b, s, h, g=g, col0=col0: (b, s, col0 // cb + g * ATT_HEADS_PER_GROUP + h))

    def prev_spec(col0, g):
        rows = ATT_BLOCK * ATT_PATTERNS[g][1]
        per = ATT_SUPER // rows
        return pl.BlockSpec((1, rows, cb),
                            lambda b, s, h, g=g, col0=col0, per=per: (
                                b, jnp.maximum(s * per - 1, 0), col0 // cb + g * ATT_HEADS_PER_GROUP + h))

    in_specs = [pl.BlockSpec(memory_space=pltpu.SMEM)]
    in_specs += [cur_spec(COL_AQ, g) for g in range(ng)]
    in_specs += [cur_spec(COL_AK, g) for g in range(ng)]
    in_specs += [cur_spec(COL_AV, g) for g in range(ng)]
    in_specs += [prev_spec(COL_AK, g) for g in range(ng)]
    in_specs += [prev_spec(COL_AV, g) for g in range(ng)]
    return pl.pallas_call(
        _attn_kernel,
        grid=(B, nsb, ATT_HEADS_PER_GROUP),
        in_specs=in_specs,
        out_specs=pl.BlockSpec((1, ATT_SUPER, cb), lambda b, s, h: (b, s, h)),
        out_shape=jax.ShapeDtypeStruct((B, S, ATT_OUT_WIDTH), BF16),
        scratch_shapes=[pltpu.VMEM((ng, ATT_SUPER, cb), F32)] * 3,
        compiler_params=_params("parallel", "parallel", "parallel"),
        name="dilated_attention",
    )(slopes, *([proj_a] * (5 * ng)))


def _log_sigmoid(x):
    return jnp.minimum(x, 0.0) - jnp.log(1.0 + jnp.exp(-jnp.abs(x)))


def _mlstm_kernel(mq_ref, mk_ref, mva_ref, mvb_ref, moa_ref, mob_ref, mif_ref, mift_ref, ifb_ref, ifbt_ref,
                  cw_ref, cb_ref, nw_ref, o_ref, tail_ref, c_ref, n_ref, m_ref):
    L = M_CHUNK
    c = pl.program_id(1)
    mv_refs = (mva_ref, mvb_ref)
    mo_refs = (moa_ref, mob_ref)
    hpb = M_MLSTM_COLS // M_V_DIM

    @pl.when(c == 0)
    def _():
        tail_ref[...] = jnp.zeros_like(tail_ref)
        c_ref[...] = jnp.zeros_like(c_ref)
        n_ref[...] = jnp.zeros_like(n_ref)
        m_ref[...] = jnp.zeros_like(m_ref)

    def conv_act(x_ref, part):
        cols = slice(part * M_QK_WIDTH, (part + 1) * M_QK_WIDTH)
        x = x_ref[0]
        xx = jnp.concatenate([tail_ref[:, cols], x], axis=0)
        y = cb_ref[:, cols]
        for j in range(M_CONV):
            off = 8 - (M_CONV - 1) + j
            y = y + cw_ref[j:j + 1, cols] * xx[off:off + L, :]
        tail_ref[:, cols] = x[L - 8:, :]
        return (y * _sigmoid(y)).astype(BF16)

    q_act = conv_act(mq_ref, 0)
    k_act = conv_act(mk_ref, 1)

    gi_c = mif_ref[0] + ifb_ref[...]
    gi_r = mift_ref[0] + ifbt_ref[...]
    ti = lax.broadcasted_iota(jnp.int32, (L, L), 0)
    si = lax.broadcasted_iota(jnp.int32, (L, L), 1)
    causal = si <= ti
    tri = causal.astype(F32)
    hp = lax.Precision.HIGHEST
    bcum_c = jnp.dot(tri, _log_sigmoid(gi_c), precision=hp, preferred_element_type=F32)
    bcum_r = lax.dot_general(_log_sigmoid(gi_r), tri, (((1,), (1,)), ((), ())), precision=hp,
                             preferred_element_type=F32)

    scale = M_QK_DIM ** -0.5
    for hd in range(M_HEADS):
        q = q_act[:, hd * M_QK_DIM:(hd + 1) * M_QK_DIM]
        k = k_act[:, hd * M_QK_DIM:(hd + 1) * M_QK_DIM]
        vcols = slice((hd % hpb) * M_V_DIM, (hd % hpb + 1) * M_V_DIM)
        v = mv_refs[hd // hpb][0, :, vcols].astype(BF16)
        b_c = bcum_c[:, M_HEADS + hd:M_HEADS + hd + 1]
        i_c = gi_c[:, hd:hd + 1]
        b_r = bcum_r[M_HEADS + hd:M_HEADS + hd + 1, :]
        i_r = gi_r[hd:hd + 1, :]
        m_prev = m_ref[hd]
        c_prev = c_ref[hd]
        n_prev = n_ref[hd]

        dmat = jnp.where(causal, b_c + (i_r - b_r), NEG)
        inter = b_c + m_prev
        m_t = jnp.maximum(inter, jnp.max(dmat, axis=-1, keepdims=True))
        w_intra = jnp.exp(dmat - m_t)
        w_inter = jnp.exp(inter - m_t)
        qk = lax.dot_general(q, k, (((1,), (1,)), ((), ())), preferred_element_type=F32) * scale * w_intra
        qf = q.astype(F32)
        num = (w_inter * scale) * jnp.dot(q, c_prev.astype(BF16), preferred_element_type=F32) \
            + jnp.dot(qk.astype(BF16), v, preferred_element_type=F32)
        den = (w_inter * scale) * jnp.sum(qf * n_prev, axis=-1, keepdims=True) \
            + jnp.sum(qk, axis=-1, keepdims=True)
        hh = num / jnp.maximum(jnp.abs(den), jnp.exp(-m_t))

        b_last = b_c[L - 1:L, :]
        w_log = b_last - b_c + i_c
        m_new = jnp.maximum(b_last + m_prev, jnp.max(w_log, axis=0, keepdims=True))
        wk = jnp.exp(w_log - m_new)
        decay = jnp.exp(b_last + m_prev - m_new)
        kw = (k.astype(F32) * wk)
        c_ref[hd] = decay * c_prev + lax.dot_general(kw.astype(BF16), v, (((0,), (0,)), ((), ())),
                                                     preferred_element_type=F32)
        n_ref[hd] = decay * n_prev + jnp.sum(kw, axis=0, keepdims=True)
        m_ref[hd] = m_new

        mu = jnp.mean(hh, axis=-1, keepdims=True)
        hc = hh - mu
        var = jnp.mean(hc * hc, axis=-1, keepdims=True)
        hn = hc * lax.rsqrt(var + LN_EPS) * nw_ref[:, hd * M_V_DIM:(hd + 1) * M_V_DIM]
        og = _sigmoid(mo_refs[hd // hpb][0, :, vcols])
        o_ref[0, :, hd * M_V_DIM:(hd + 1) * M_V_DIM] = (hn * og).astype(o_ref.dtype)


def _mlstm(proj_a, mif, mif_t, if_bias, conv_w, conv_b, norm_w, B, S):
    L = M_CHUNK
    W = 2 * M_QK_WIDTH
    cw = M_MLSTM_COLS
    ifb = jnp.zeros((1, ROUTE_LANES), F32).at[0, :2 * M_HEADS].set(if_bias)
    ifbt = jnp.broadcast_to(if_bias.reshape(2 * M_HEADS, 1), (2 * M_HEADS, L))

    def col_spec(col0):
        assert col0 % cw == 0
        return pl.BlockSpec((1, L, cw), lambda b, c, col0=col0: (b, c, col0 // cw))

    return pl.pallas_call(
        _mlstm_kernel,
        grid=(B, S // L),
        in_specs=[col_spec(COL_MQK), col_spec(COL_MQK + M_QK_WIDTH),
                  col_spec(COL_MV), col_spec(COL_MV + cw),
                  col_spec(COL_MO), col_spec(COL_MO + cw),
                  pl.BlockSpec((1, L, ROUTE_LANES), lambda b, c: (b, c, 0)),
                  pl.BlockSpec((1, 2 * M_HEADS, L), lambda b, c: (b, 0, c)),
                  pl.BlockSpec((1, ROUTE_LANES), lambda b, c: (0, 0)),
                  pl.BlockSpec((2 * M_HEADS, L), lambda b, c: (0, 0)),
                  pl.BlockSpec((M_CONV, W), lambda b, c: (0, 0)),
                  pl.BlockSpec((1, W), lambda b, c: (0, 0)),
                  pl.BlockSpec((1, M_V_WIDTH), lambda b, c: (0, 0))],
        out_specs=pl.BlockSpec((1, L, M_V_WIDTH), lambda b, c: (b, c, 0)),
        out_shape=jax.ShapeDtypeStruct((B, S, M_V_WIDTH), BF16),
        scratch_shapes=[pltpu.VMEM((8, W), F32),
                        pltpu.VMEM((M_HEADS, M_QK_DIM, M_V_DIM), F32),
                        pltpu.VMEM((M_HEADS, 1, M_QK_DIM), F32),
                        pltpu.VMEM((M_HEADS, 1, 1), F32)],
        compiler_params=_params("parallel", "arbitrary"),
        name="mlstm",
    )(proj_a, proj_a, proj_a, proj_a, proj_a, proj_a, mif, mif_t, ifb, ifbt, conv_w, conv_b.reshape(1, W),
      norm_w.reshape(1, M_V_WIDTH))


def _merge_kernel(att_ref, hm_ref, gate_ref, h_ref, wpa_ref, wpm_ref, wo_ref, g_ref, b_ref, wr_ref, br_ref,
                  h1_ref, lg_ref):
    pa = jnp.dot(att_ref[...], wpa_ref[...], preferred_element_type=F32)
    pm = jnp.dot(hm_ref[...], wpm_ref[...], preferred_element_type=F32)
    ga = _sigmoid(gate_ref[:, :D_MODEL])
    gm = _sigmoid(gate_ref[:, D_MODEL:])
    merged = (ga * pa + gm * pm).astype(BF16)
    y = jnp.dot(merged, wo_ref[...], preferred_element_type=F32)
    h1 = _layer_norm_rows(DEEPNORM_ALPHA * h_ref[...] + y, g_ref[...], b_ref[...])
    h1_ref[...] = h1
    lg_ref[...] = jnp.dot(h1, wr_ref[...], precision=lax.Precision.HIGHEST,
                          preferred_element_type=F32) + br_ref[...]


def _merge(att, hm, gate, h, wpa, wpm, wo, g1, b1, wr, br, tm=256):
    T, D = h.shape
    const = lambda i: (0, 0)
    one = pl.Buffered(1)
    return pl.pallas_call(
        _merge_kernel,
        grid=(T // tm,),
        in_specs=[pl.BlockSpec((tm, ATT_OUT_WIDTH), lambda i: (i, 0)),
                  pl.BlockSpec((tm, M_V_WIDTH), lambda i: (i, 0)),
                  pl.BlockSpec((tm, N_BRANCHES * D), lambda i: (i, 0)),
                  pl.BlockSpec((tm, D), lambda i: (i, 0)),
                  pl.BlockSpec((ATT_OUT_WIDTH, D), const, pipeline_mode=one),
                  pl.BlockSpec((M_V_WIDTH, D), const, pipeline_mode=one),
                  pl.BlockSpec((D, D), const, pipeline_mode=one),
                  pl.BlockSpec((1, D), const),
                  pl.BlockSpec((1, D), const),
                  pl.BlockSpec((D, ROUTE_LANES), const, pipeline_mode=one),
                  pl.BlockSpec((1, ROUTE_LANES), const)],
        out_specs=[pl.BlockSpec((tm, D), lambda i: (i, 0)),
                   pl.BlockSpec((tm, ROUTE_LANES), lambda i: (i, 0))],
        out_shape=[jax.ShapeDtypeStruct((T, D), F32), jax.ShapeDtypeStruct((T, ROUTE_LANES), F32)],
        compiler_params=_params("parallel"),
        name="merge_out_ln1",
    )(att, hm, gate, h, wpa, wpm, wo, g1.reshape(1, D), b1.reshape(1, D), wr, br)


def _route_kernel(lg_ref, e_ref, w_ref):
    lg = lg_ref[...]
    col = lax.broadcasted_iota(jnp.int32, lg.shape, 1)
    big = jnp.int32(ROUTE_LANES)

    def first_argmax(v, vmax):
        return jnp.min(jnp.where(v == vmax, col, big), axis=-1, keepdims=True)

    gl = jnp.where(col < N_GROUPS, lg, NEG)
    gmax = jnp.max(gl, axis=-1, keepdims=True)
    grp = first_argmax(gl, gmax)
    gsum = jnp.sum(jnp.where(col < N_GROUPS, jnp.exp(lg - gmax), 0.0), axis=-1, keepdims=True)
    g_w = 1.0 / gsum
    ecol = col - N_GROUPS
    egrp = lax.shift_right_arithmetic(ecol, int(math.log2(EXPERTS_PER_GROUP)))
    in_grp = (ecol >= 0) & (ecol < N_EXPERTS) & (egrp == grp)
    el = jnp.where(in_grp, lg, NEG)
    v1 = jnp.max(el, axis=-1, keepdims=True)
    i1 = first_argmax(el, v1)
    el2 = jnp.where(col == i1, NEG, el)
    v2 = jnp.max(el2, axis=-1, keepdims=True)
    i2 = first_argmax(el2, v2)
    t = jnp.exp(v2 - v1)
    p1 = 1.0 / (1.0 + t)
    p2 = t / (1.0 + t)
    e_ref[...] = jnp.where(col == 0, i1 - N_GROUPS, jnp.where(col == 1, i2 - N_GROUPS, 0))
    w_ref[...] = jnp.where(col == 0, g_w * p1, jnp.where(col == 1, g_w * p2, 0.0))


def _route(logits, tm=1024):
    T = logits.shape[0]
    spec = pl.BlockSpec((tm, ROUTE_LANES), lambda i: (i, 0))
    return pl.pallas_call(
        _route_kernel,
        grid=(T // tm,),
        in_specs=[spec],
        out_specs=[spec, spec],
        out_shape=[jax.ShapeDtypeStruct((T, ROUTE_LANES), jnp.int32),
                   jax.ShapeDtypeStruct((T, ROUTE_LANES), F32)],
        compiler_params=_params("parallel"),
        name="route",
    )(logits)


def _dispatch_plan(e_tk, T):
    M = T * TOP_K
    e_flat = e_tk.reshape(M)
    onehot = (e_flat[:, None] == jnp.arange(N_EXPERTS, dtype=jnp.int32)[None, :]).astype(jnp.int32)
    csum = jnp.cumsum(onehot, axis=0)
    counts = csum[-1]
    rank = jnp.sum((csum - onehot) * onehot, axis=1)
    padded = (counts + MOE_SUB - 1) // MOE_SUB * MOE_SUB
    pstart = jnp.cumsum(padded) - padded
    dest = jnp.sum(onehot * pstart[None, :], axis=1) + rank
    npad = M + N_EXPERTS * MOE_SUB
    tok = jnp.arange(M, dtype=jnp.int32) // TOP_K
    slot = jnp.arange(M, dtype=jnp.int32) % TOP_K
    slot_dst = jnp.zeros((npad,), jnp.int32).at[dest].set(slot * T + tok)

    nsb_max = N_EXPERTS + M // MOE_SUPER
    nsb_e = (padded + MOE_SUPER - 1) // MOE_SUPER
    sb_end = jnp.cumsum(nsb_e)
    sb_beg = sb_end - nsb_e
    total = sb_end[-1]
    sb = jnp.arange(nsb_max, dtype=jnp.int32)
    sb_c = jnp.minimum(sb, total - 1)
    ex = jnp.sum((sb_end[None, :] <= sb_c[:, None]).astype(jnp.int32), axis=1)
    local = sb_c - sb_beg[ex]
    row0 = pstart[ex] + local * MOE_SUPER
    active = sb < total
    cnt = jnp.where(active, jnp.clip(counts[ex] - local * MOE_SUPER, 0, MOE_SUPER), 0)
    nsub = jnp.where(active, jnp.clip(padded[ex] - local * MOE_SUPER, 0, MOE_SUPER) // MOE_SUB, 0)
    return (ex.astype(jnp.int32), row0.astype(jnp.int32), cnt.astype(jnp.int32), nsub.astype(jnp.int32),
            slot_dst)


def _moe_kernel(sb_ex, sb_row0, sb_cnt, sb_nsub, slot_dst,
                h1_hbm, wg_ref, wu_ref, wd_ref, y2_hbm,
                xf_buf, xb_buf, acc_buf, wgu_buf, wd_buf, gsem, ssem, *, T):
    b = pl.program_id(0)
    j = pl.program_id(1)
    nj = pl.num_programs(1)
    nsub = sb_nsub[b]
    row0 = sb_row0[b]
    cnt = sb_cnt[b]

    def row_copy_in(i, tok):
        return pltpu.make_async_copy(h1_hbm.at[pl.ds(tok, 1), :], xf_buf.at[pl.ds(i, 1), :], gsem)

    def row_copy_out(i, dst):
        return pltpu.make_async_copy(acc_buf.at[pl.ds(i, 1), :], y2_hbm.at[pl.ds(dst, 1), :], ssem)

    @pl.when((j == 0) & (nsub > 0))
    def _gather():
        n = nsub * MOE_SUB

        def issue(i, c):
            row_copy_in(i, lax.rem(slot_dst[row0 + i], T)).start()
            return c
        lax.fori_loop(0, n, issue, 0)

        def wait(i, c):
            row_copy_in(i, 0).wait()
            return c
        lax.fori_loop(0, n, wait, 0)

        def cast(k, c):
            rs = pl.ds(pl.multiple_of(k * MOE_SUB, MOE_SUB), MOE_SUB)
            xb_buf[rs, :] = xf_buf[rs, :].astype(BF16)
            acc_buf[rs, :] = jnp.zeros((MOE_SUB, D_MODEL), F32)
            return c
        lax.fori_loop(0, nsub, cast, 0)

    @pl.when(nsub > 0)
    def _compute():
        wgu_buf[:, :MOE_FT] = wg_ref[0].astype(BF16)
        wgu_buf[:, MOE_FT:] = wu_ref[0].astype(BF16)
        wd_buf[...] = wd_ref[0].astype(BF16)

        def sub(k, c):
            rs = pl.ds(pl.multiple_of(k * MOE_SUB, MOE_SUB), MOE_SUB)
            gu = jnp.dot(xb_buf[rs, :], wgu_buf[...], preferred_element_type=F32)
            gt = gu[:, :MOE_FT]
            ut = gu[:, MOE_FT:]
            hmid = (gt * _sigmoid(gt) * ut).astype(BF16)
            acc_buf[rs, :] += jnp.dot(hmid, wd_buf[...], preferred_element_type=F32)
            return c
        lax.fori_loop(0, nsub, sub, 0)

    @pl.when((j == nj - 1) & (nsub > 0))
    def _scatter():
        def issue(i, c):
            row_copy_out(i, slot_dst[row0 + i]).start()
            return c
        lax.fori_loop(0, cnt, issue, 0)

        def wait(i, c):
            row_copy_out(i, 0).wait()
            return c
        lax.fori_loop(0, cnt, wait, 0)


def _moe_ffn(h1, plan, w_gate, w_up, w_down):
    T, D = h1.shape
    sb_ex, sb_row0, sb_cnt, sb_nsub, slot_dst = plan
    nsb_max = sb_ex.shape[0]
    nj = D_FF_EXPERT // MOE_FT

    def jj(b, j, nsub):
        return jnp.where(nsub[b] > 0, j, nj - 1)

    grid_spec = pltpu.PrefetchScalarGridSpec(
        num_scalar_prefetch=5,
        grid=(nsb_max, nj),
        in_specs=[pl.BlockSpec(memory_space=pl.ANY),
                  pl.BlockSpec((1, D, MOE_FT), lambda b, j, ex, r0, ct, ns, sd: (ex[b], 0, jj(b, j, ns))),
                  pl.BlockSpec((1, D, MOE_FT), lambda b, j, ex, r0, ct, ns, sd: (ex[b], 0, jj(b, j, ns))),
                  pl.BlockSpec((1, MOE_FT, D), lambda b, j, ex, r0, ct, ns, sd: (ex[b], jj(b, j, ns), 0))],
        out_specs=pl.BlockSpec(memory_space=pl.ANY),
        scratch_shapes=[pltpu.VMEM((MOE_SUPER, D), F32),
                        pltpu.VMEM((MOE_SUPER, D), BF16),
                        pltpu.VMEM((MOE_SUPER, D), F32),
                        pltpu.VMEM((D, 2 * MOE_FT), BF16),
                        pltpu.VMEM((MOE_FT, D), BF16),
                        pltpu.SemaphoreType.DMA(()),
                        pltpu.SemaphoreType.DMA(())],
    )
    return pl.pallas_call(
        functools.partial(_moe_kernel, T=T),
        grid_spec=grid_spec,
        out_shape=jax.ShapeDtypeStruct((TOP_K * T, D), F32),
        compiler_params=_params("arbitrary", "arbitrary"),
        name="moe_experts",
    )(sb_ex, sb_row0, sb_cnt, sb_nsub, slot_dst, h1, w_gate, w_up, w_down)


def _ln_out_kernel(h1_ref, y0_ref, y1_ref, rw_ref, g_ref, b_ref, o_ref):
    rw = rw_ref[...]
    z = DEEPNORM_ALPHA * h1_ref[...] + rw[:, 0:1] * y0_ref[...] + rw[:, 1:2] * y1_ref[...]
    o_ref[...] = _layer_norm_rows(z, g_ref[...], b_ref[...])


def _ln_out(h1, y2, rw, g, b, tm=256):
    T, D = h1.shape
    nb = T // tm
    return pl.pallas_call(
        _ln_out_kernel,
        grid=(nb,),
        in_specs=[pl.BlockSpec((tm, D), lambda i: (i, 0)),
                  pl.BlockSpec((tm, D), lambda i: (i, 0)),
                  pl.BlockSpec((tm, D), lambda i: (i + nb, 0)),
                  pl.BlockSpec((tm, ROUTE_LANES), lambda i: (i, 0)),
                  pl.BlockSpec((1, D), lambda i: (0, 0)),
                  pl.BlockSpec((1, D), lambda i: (0, 0))],
        out_specs=pl.BlockSpec((tm, D), lambda i: (i, 0)),
        out_shape=jax.ShapeDtypeStruct((T, D), F32),
        compiler_params=_params("parallel"),
        name="combine_ln2",
    )(h1, y2, y2, rw, g.reshape(1, D), b.reshape(1, D))


def kernel(x, ln_in_g, ln_in_b, w_in, m_conv_w, m_conv_b, m_if_bias, m_norm_w, w_proj_att, w_proj_mlstm, w_out,
           ln1_g, ln1_b, w_router_group, b_router_group, w_router_expert, b_router_expert, w_gate, w_up, w_down,
           ln2_g, ln2_b):
    B, S, D = x.shape
    T = B * S
    assert D == D_MODEL and S % ATT_SUPER == 0 and w_in.shape[0] == DEPTH == 1

    h, hb = _ln_in(x.reshape(T, D), ln_in_g, ln_in_b)
    for l in range(DEPTH):
        wl = w_in[l]
        proj_a = _matmul(hb, wl, PROJ_A_WIDTH, 1024, 768, F32, "in_proj_a")
        gate = _matmul(hb, wl[:, COL_GATE:], N_BRANCHES * D, 1024, 512, F32, "in_proj_gate")
        w_if = jnp.zeros((D, ROUTE_LANES), F32).at[:, :2 * M_HEADS].set(wl[:, COL_MIF:COL_GATE])
        mif = _matmul(hb, w_if, ROUTE_LANES, 1024, ROUTE_LANES, F32, "in_proj_if")

        proj_a3 = proj_a.reshape(B, S, PROJ_A_WIDTH)
        att = _attention(proj_a3, B, S)
        mif3 = mif.reshape(B, S, ROUTE_LANES)
        mif_t = jnp.swapaxes(mif3[:, :, :2 * M_HEADS], 1, 2)
        hm = _mlstm(proj_a3, mif3, mif_t, m_if_bias[l], m_conv_w[l], m_conv_b[l], m_norm_w[l], B, S)

        w_r = jnp.zeros((D, ROUTE_LANES), F32)
        w_r = w_r.at[:, :N_GROUPS].set(w_router_group[l]).at[:, N_GROUPS:N_GROUPS + N_EXPERTS].set(w_router_expert[l])
        b_r = jnp.zeros((1, ROUTE_LANES), F32)
        b_r = b_r.at[0, :N_GROUPS].set(b_router_group[l]).at[0, N_GROUPS:N_GROUPS + N_EXPERTS].set(b_router_expert[l])
        h1, logits = _merge(att.reshape(T, ATT_OUT_WIDTH), hm.reshape(T, M_V_WIDTH), gate, h,
                            w_proj_att[l].astype(BF16), w_proj_mlstm[l].astype(BF16), w_out[l].astype(BF16),
                            ln1_g[l], ln1_b[l], w_r, b_r)

        e_out, rw = _route(logits)
        plan = _dispatch_plan(e_out[:, :TOP_K], T)
        y2 = _moe_ffn(h1, plan, w_gate[l], w_up[l], w_down[l])
        h = _ln_out(h1, y2, rw, ln2_g[l], ln2_b[l])
    return h.reshape(B, S, D)
```

```python
import functools
import math

import numpy as np
import jax
import jax.numpy as jnp
from jax import lax
from jax.experimental import pallas as pl
from jax.experimental.pallas import tpu as pltpu

F32 = jnp.float32
BF16 = jnp.bfloat16

D_MODEL = 2048
ATT_HEAD_DIM = 128
ATT_HEADS_PER_GROUP = 4
ATT_PATTERNS = ((128, 1), (512, 4), (2048, 16))
ATT_HEADS = ATT_HEADS_PER_GROUP * len(ATT_PATTERNS)
ATT_WIDTH = ATT_HEADS * ATT_HEAD_DIM
ATT_OUT_WIDTH = ATT_HEADS_PER_GROUP * ATT_HEAD_DIM
ATT_BLOCK = 128
ATT_SUPER = 2048

M_HEADS = 4
M_QK_DIM = 128
M_V_DIM = 256
M_QK_WIDTH = M_HEADS * M_QK_DIM
M_V_WIDTH = M_HEADS * M_V_DIM
M_CONV = 4
M_CHUNK = 128
M_MLSTM_COLS = 512

N_BRANCHES = 2
IN_PROJ_SPLITS = (ATT_WIDTH, ATT_WIDTH, ATT_WIDTH, 2 * M_QK_WIDTH, M_V_WIDTH, M_V_WIDTH,
                  2 * M_HEADS, N_BRANCHES * D_MODEL)
COL_AQ = 0
COL_AK = ATT_WIDTH
COL_AV = 2 * ATT_WIDTH
COL_MQK = 3 * ATT_WIDTH
COL_MV = COL_MQK + 2 * M_QK_WIDTH
COL_MO = COL_MV + M_V_WIDTH
COL_MIF = COL_MO + M_V_WIDTH
COL_GATE = COL_MIF + 2 * M_HEADS
PROJ_A_WIDTH = COL_MIF

N_GROUPS = 4
EXPERTS_PER_GROUP = 8
N_EXPERTS = N_GROUPS * EXPERTS_PER_GROUP
TOP_K = 2
D_FF_EXPERT = 1408
MOE_SUB = 256
MOE_SUPER = 1024
MOE_FT = 256
MOE_NFT = D_FF_EXPERT // MOE_FT
MOE_FT_TAIL = D_FF_EXPERT - MOE_NFT * MOE_FT
MOE_NJ = MOE_NFT + 1
MOE_DMA_UNROLL = 8
assert MOE_FT_TAIL > 0 and D_FF_EXPERT % MOE_FT_TAIL == 0 and MOE_FT_TAIL % 128 == 0
ROUTE_LANES = 128

DEPTH = 1
DEEPNORM_ALPHA = (2 * DEPTH) ** 0.25
LN_EPS = 1e-5
NEG = -1e30

VMEM_LIMIT = 56 * 1024 * 1024


def _alibi_slopes(n):
    def geometric(k):
        start = 2.0 ** (-8.0 / k)
        return [start ** (i + 1) for i in range(k)]
    c = 2 ** int(math.floor(math.log2(n)))
    s = geometric(c) if c == n else geometric(c) + geometric(2 * c)[0::2][: n - c]
    return np.array(sorted(s, reverse=True), dtype=np.float32)


def _params(*sem):
    return pltpu.CompilerParams(dimension_semantics=sem, vmem_limit_bytes=VMEM_LIMIT)


def _layer_norm_rows(z, g, b):
    mu = jnp.mean(z, axis=-1, keepdims=True)
    zc = z - mu
    var = jnp.mean(zc * zc, axis=-1, keepdims=True)
    return zc * lax.rsqrt(var + LN_EPS) * g + b


def _sigmoid(x):
    return 1.0 / (1.0 + jnp.exp(-x))


def _ln_in_kernel(x_ref, g_ref, b_ref, hf_ref, hb_ref):
    y = _layer_norm_rows(x_ref[...], g_ref[...], b_ref[...])
    hf_ref[...] = y
    hb_ref[...] = y.astype(BF16)


def _ln_in(x2, g, b, tm=256):
    T, D = x2.shape
    return pl.pallas_call(
        _ln_in_kernel,
        grid=(T // tm,),
        in_specs=[pl.BlockSpec((tm, D), lambda i: (i, 0)),
                  pl.BlockSpec((1, D), lambda i: (0, 0)),
                  pl.BlockSpec((1, D), lambda i: (0, 0))],
        out_specs=[pl.BlockSpec((tm, D), lambda i: (i, 0)),
                   pl.BlockSpec((tm, D), lambda i: (i, 0))],
        out_shape=[jax.ShapeDtypeStruct((T, D), F32), jax.ShapeDtypeStruct((T, D), BF16)],
        compiler_params=_params("parallel"),
        name="ln_in",
    )(x2, g.reshape(1, D), b.reshape(1, D))


def _mm_nt_kernel(a_ref, w_ref, o_ref, wb_ref):
    @pl.when(pl.program_id(1) == 0)
    def _():
        wb_ref[...] = w_ref[...].astype(BF16)

    o_ref[...] = lax.dot_general(a_ref[...], wb_ref[...], (((1,), (1,)), ((), ())),
                                 preferred_element_type=F32).astype(o_ref.dtype)


def _matmul_nt(a, wt, row0, n_cols, tm, tn, out_dtype, name):
    T, K = a.shape
    if row0 % tn == 0:
        w_spec = pl.BlockSpec((tn, K), lambda j, i: (j + row0 // tn, 0))
    else:
        assert row0 % 8 == 0 and tn % 8 == 0
        w_spec = pl.BlockSpec((pl.Element(tn), pl.Element(K)),
                              lambda j, i: ((row0 // 8 + j * (tn // 8)) * 8, 0))
    return pl.pallas_call(
        _mm_nt_kernel,
        grid=(n_cols // tn, T // tm),
        in_specs=[pl.BlockSpec((tm, K), lambda j, i: (i, 0)), w_spec],
        out_specs=pl.BlockSpec((tm, tn), lambda j, i: (i, j)),
        out_shape=jax.ShapeDtypeStruct((T, n_cols), out_dtype),
        scratch_shapes=[pltpu.VMEM((tn, K), BF16)],
        compiler_params=_params("parallel", "arbitrary"),
        name=name,
    )(a, wt)


ATT_UNROLL = 4


def _batched_loop(n, body):
    u = max(d for d in range(1, ATT_UNROLL + 1) if n % d == 0)
    if n == u:
        body(list(range(n)))
        return

    def step(i, c):
        body([i * u + k for k in range(u)])
        return c
    lax.fori_loop(0, n // u, step, 0)


def _attn_blocks(r, slope_r, prev_bias, q_ref, kc_ref, vc_ref, kp_ref, vp_ref, bases, g, acc_ref, m_ref, l_ref):
    def rows(start):
        return pl.ds(start, ATT_BLOCK, r) if r > 1 else pl.ds(start, ATT_BLOCK)

    dn = (((1,), (1,)), ((), ()))
    scale = ATT_HEAD_DIM ** -0.5
    qi = lax.broadcasted_iota(jnp.int32, (ATT_BLOCK, ATT_BLOCK), 0)
    ki = lax.broadcasted_iota(jnp.int32, (ATT_BLOCK, ATT_BLOCK), 1)
    dlt = (qi - ki).astype(F32)
    alibi_c = -slope_r * dlt
    alibi_p = -slope_r * (dlt + float(ATT_BLOCK)) + prev_bias

    scores = []
    for base, base_prev in bases:
        q = q_ref[0, rows(base), :].astype(BF16)
        kc = kc_ref[0, rows(base), :].astype(BF16)
        kp = kp_ref[0, rows(base_prev), :].astype(BF16)
        sc = lax.dot_general(q, kc, dn, preferred_element_type=F32) * scale + alibi_c
        sp = lax.dot_general(q, kp, dn, preferred_element_type=F32) * scale + alibi_p
        scores.append((jnp.where(ki <= qi, sc, NEG), jnp.where(ki >= qi, sp, NEG)))
    probs = []
    for sc, sp in scores:
        m = jnp.max(jnp.maximum(sc, sp), axis=-1, keepdims=True)
        pc = jnp.exp(sc - m)
        pp = jnp.exp(sp - m)
        l = jnp.sum(pc + pp, axis=-1, keepdims=True)
        probs.append((m, l, pc.astype(BF16), pp.astype(BF16)))
    outs = []
    for (base, base_prev), (m, l, pc, pp) in zip(bases, probs):
        vc = vc_ref[0, rows(base), :].astype(BF16)
        vp = vp_ref[0, rows(base_prev), :].astype(BF16)
        outs.append(jnp.dot(pc, vc, preferred_element_type=F32) + jnp.dot(pp, vp, preferred_element_type=F32))
    for (base, _), (m, l, _, _), acc in zip(bases, probs, outs):
        acc_ref[g, rows(base), :] = acc
        m_ref[g, rows(base), :] = jnp.broadcast_to(m, (ATT_BLOCK, ATT_HEAD_DIM))
        l_ref[g, rows(base), :] = jnp.broadcast_to(l, (ATT_BLOCK, ATT_HEAD_DIM))


def _attn_kernel(slopes_ref, *refs):
    ng = len(ATT_PATTERNS)
    q_refs = refs[0:ng]
    kc_refs = refs[ng:2 * ng]
    vc_refs = refs[2 * ng:3 * ng]
    kp_refs = refs[3 * ng:4 * ng]
    vp_refs = refs[4 * ng:5 * ng]
    o_ref = refs[5 * ng]
    acc_ref, m_ref, l_ref = refs[5 * ng + 1:]
    s = pl.program_id(1)
    h = pl.program_id(2)
    prev_bias = jnp.where(s > 0, 0.0, NEG).astype(F32)

    for g, (window, r) in enumerate(ATT_PATTERNS):
        assert window // r == ATT_BLOCK
        nblk = ATT_SUPER // (ATT_BLOCK * r)
        slope_r = slopes_ref[g, h] * float(r)
        common = dict(r=r, slope_r=slope_r, g=g, acc_ref=acc_ref, m_ref=m_ref, l_ref=l_ref,
                      q_ref=q_refs[g], kc_ref=kc_refs[g], vc_ref=vc_refs[g])

        def first(ps, common=common, g=g):
            _attn_blocks(prev_bias=prev_bias, kp_ref=kp_refs[g], vp_ref=vp_refs[g],
                         bases=[(p, p) for p in ps], **common)
        _batched_loop(r, first)

        if nblk > 1:
            def rest(idxs, common=common, g=g, r=r, nblk=nblk):
                bases = []
                for idx in idxs:
                    p = idx // (nblk - 1)
                    j = idx % (nblk - 1) + 1
                    base = p + j * (ATT_BLOCK * r)
                    bases.append((base, base - ATT_BLOCK * r))
                _attn_blocks(prev_bias=jnp.float32(0.0), kp_ref=kc_refs[g], vp_ref=vc_refs[g],
                             bases=bases, **common)
            _batched_loop(r * (nblk - 1), rest)

    ch = 256
    def merge(i, c):
        rs = pl.ds(pl.multiple_of(i * ch, ch), ch)
        ms = [m_ref[g, rs, :] for g in range(ng)]
        mx = functools.reduce(jnp.maximum, ms)
        num = jnp.zeros((ch, ATT_HEAD_DIM), F32)
        den = jnp.zeros((ch, ATT_HEAD_DIM), F32)
        for g in range(ng):
            w = jnp.exp(ms[g] - mx)
            num = num + w * acc_ref[g, rs, :]
            den = den + w * l_ref[g, rs, :]
        o_ref[0, rs, :] = (num / den).astype(o_ref.dtype)
        return c
    lax.fori_loop(0, ATT_SUPER // ch, merge, 0)


def _attention(proj_a, B, S):
    ng = len(ATT_PATTERNS)
    nsb = S // ATT_SUPER
    cb = ATT_HEAD_DIM
    slopes = jnp.asarray(_alibi_slopes(ATT_HEADS).reshape(ng, ATT_HEADS_PER_GROUP))

    def cur_spec(col0, g):
        return pl.BlockSpec((1, ATT_SUPER, cb),
                            lambda b, s, h, g=g, col0=col0: (b, s, col0 // cb + g * ATT_HEADS_PER_GROUP + h))

    def prev_spec(col0, g):
        rows = ATT_BLOCK * ATT_PATTERNS[g][1]
        per = ATT_SUPER // rows
        return pl.BlockSpec((1, rows, cb),
                            lambda b, s, h, g=g, col0=col0, per=per: (
                                b, jnp.maximum(s * per - 1, 0), col0 // cb + g * ATT_HEADS_PER_GROUP + h))

    in_specs = [pl.BlockSpec(memory_space=pltpu.SMEM)]
    in_specs += [cur_spec(COL_AQ, g) for g in range(ng)]
    in_specs += [cur_spec(COL_AK, g) for g in range(ng)]
    in_specs += [cur_spec(COL_AV, g) for g in range(ng)]
    in_specs += [prev_spec(COL_AK, g) for g in range(ng)]
    in_specs += [prev_spec(COL_AV, g) for g in range(ng)]
    return pl.pallas_call(
        _attn_kernel,
        grid=(B, nsb, ATT_HEADS_PER_GROUP),
        in_specs=in_specs,
        out_specs=pl.BlockSpec((1, ATT_SUPER, cb), lambda b, s, h: (b, s, h)),
        out_shape=jax.ShapeDtypeStruct((B, S, ATT_OUT_WIDTH), BF16),
        scratch_shapes=[pltpu.VMEM((ng, ATT_SUPER, cb), F32)] * 3,
        compiler_params=_params("parallel", "parallel", "parallel"),
        name="dilated_attention",
    )(slopes, *([proj_a] * (5 * ng)))


def _log_sigmoid(x):
    return jnp.minimum(x, 0.0) - jnp.log(1.0 + jnp.exp(-jnp.abs(x)))


def _mlstm_kernel(mq_ref, mk_ref, mva_ref, mvb_ref, moa_ref, mob_ref, mif_ref, mift_ref, ifb_ref, ifbt_ref,
                  cw_ref, cb_ref, nw_ref, o_ref, tail_ref, c_ref, n_ref, m_ref):
    L = M_CHUNK
    c = pl.program_id(1)
    mv_refs = (mva_ref, mvb_ref)
    mo_refs = (moa_ref, mob_ref)
    hpb = M_MLSTM_COLS // M_V_DIM

    @pl.when(c == 0)
    def _():
        tail_ref[...] = jnp.zeros_like(tail_ref)
        c_ref[...] = jnp.zeros_like(c_ref)
        n_ref[...] = jnp.zeros_like(n_ref)
        m_ref[...] = jnp.zeros_like(m_ref)

    def conv_act(x_ref, part):
        cols = slice(part * M_QK_WIDTH, (part + 1) * M_QK_WIDTH)
        x = x_ref[0]
        xx = jnp.concatenate([tail_ref[:, cols], x], axis=0)
        y = cb_ref[:, cols]
        for j in range(M_CONV):
            off = 8 - (M_CONV - 1) + j
            y = y + cw_ref[j:j + 1, cols] * xx[off:off + L, :]
        tail_ref[:, cols] = x[L - 8:, :]
        return (y * _sigmoid(y)).astype(BF16)

    q_act = conv_act(mq_ref, 0)
    k_act = conv_act(mk_ref, 1)

    gi_c = mif_ref[0] + ifb_ref[...]
    gi_r = mift_ref[0] + ifbt_ref[...]
    ti = lax.broadcasted_iota(jnp.int32, (L, L), 0)
    si = lax.broadcasted_iota(jnp.int32, (L, L), 1)
    causal = si <= ti
    tri = causal.astype(F32)
    hp = lax.Precision.HIGHEST
    bcum_c = jnp.dot(tri, _log_sigmoid(gi_c), precision=hp, preferred_element_type=F32)
    bcum_r = lax.dot_general(_log_sigmoid(gi_r), tri, (((1,), (1,)), ((), ())), precision=hp,
                             preferred_element_type=F32)

    scale = M_QK_DIM ** -0.5
    for hd in range(M_HEADS):
        q = q_act[:, hd * M_QK_DIM:(hd + 1) * M_QK_DIM]
        k = k_act[:, hd * M_QK_DIM:(hd + 1) * M_QK_DIM]
        vcols = slice((hd % hpb) * M_V_DIM, (hd % hpb + 1) * M_V_DIM)
        v = mv_refs[hd // hpb][0, :, vcols].astype(BF16)
        b_c = bcum_c[:, M_HEADS + hd:M_HEADS + hd + 1]
        i_c = gi_c[:, hd:hd + 1]
        b_r = bcum_r[M_HEADS + hd:M_HEADS + hd + 1, :]
        i_r = gi_r[hd:hd + 1, :]
        m_prev = m_ref[hd]
        c_prev = c_ref[hd]
        n_prev = n_ref[hd]

        dmat = jnp.where(causal, b_c + (i_r - b_r), NEG)
        inter = b_c + m_prev
        m_t = jnp.maximum(inter, jnp.max(dmat, axis=-1, keepdims=True))
        w_intra = jnp.exp(dmat - m_t)
        w_inter = jnp.exp(inter - m_t)
        qk = lax.dot_general(q, k, (((1,), (1,)), ((), ())), preferred_element_type=F32) * scale * w_intra
        qf = q.astype(F32)
        num = (w_inter * scale) * jnp.dot(q, c_prev.astype(BF16), preferred_element_type=F32) \
            + jnp.dot(qk.astype(BF16), v, preferred_element_type=F32)
        den = (w_inter * scale) * jnp.sum(qf * n_prev, axis=-1, keepdims=True) \
            + jnp.sum(qk, axis=-1, keepdims=True)
        hh = num / jnp.maximum(jnp.abs(den), jnp.exp(-m_t))

        b_last = b_c[L - 1:L, :]
        w_log = b_last - b_c + i_c
        m_new = jnp.maximum(b_last + m_prev, jnp.max(w_log, axis=0, keepdims=True))
        wk = jnp.exp(w_log - m_new)
        decay = jnp.exp(b_last + m_prev - m_new)
        kw = (k.astype(F32) * wk)
        c_ref[hd] = decay * c_prev + lax.dot_general(kw.astype(BF16), v, (((0,), (0,)), ((), ())),
                                                     preferred_element_type=F32)
        n_ref[hd] = decay * n_prev + jnp.sum(kw, axis=0, keepdims=True)
        m_ref[hd] = m_new

        mu = jnp.mean(hh, axis=-1, keepdims=True)
        hc = hh - mu
        var = jnp.mean(hc * hc, axis=-1, keepdims=True)
        hn = hc * lax.rsqrt(var + LN_EPS) * nw_ref[:, hd * M_V_DIM:(hd + 1) * M_V_DIM]
        og = _sigmoid(mo_refs[hd // hpb][0, :, vcols])
        o_ref[0, :, hd * M_V_DIM:(hd + 1) * M_V_DIM] = (hn * og).astype(o_ref.dtype)


def _mlstm(proj_a, mif, mif_t, if_bias, conv_w, conv_b, norm_w, B, S):
    L = M_CHUNK
    W = 2 * M_QK_WIDTH
    cw = M_MLSTM_COLS
    ifb = jnp.zeros((1, ROUTE_LANES), F32).at[0, :2 * M_HEADS].set(if_bias)
    ifbt = jnp.broadcast_to(if_bias.reshape(2 * M_HEADS, 1), (2 * M_HEADS, L))

    def col_spec(col0):
        assert col0 % cw == 0
        return pl.BlockSpec((1, L, cw), lambda b, c, col0=col0: (b, c, col0 // cw))

    return pl.pallas_call(
        _mlstm_kernel,
        grid=(B, S // L),
        in_specs=[col_spec(COL_MQK), col_spec(COL_MQK + M_QK_WIDTH),
                  col_spec(COL_MV), col_spec(COL_MV + cw),
                  col_spec(COL_MO), col_spec(COL_MO + cw),
                  pl.BlockSpec((1, L, ROUTE_LANES), lambda b, c: (b, c, 0)),
                  pl.BlockSpec((1, 2 * M_HEADS, L), lambda b, c: (b, 0, c)),
                  pl.BlockSpec((1, ROUTE_LANES), lambda b, c: (0, 0)),
                  pl.BlockSpec((2 * M_HEADS, L), lambda b, c: (0, 0)),
                  pl.BlockSpec((M_CONV, W), lambda b, c: (0, 0)),
                  pl.BlockSpec((1, W), lambda b, c: (0, 0)),
                  pl.BlockSpec((1, M_V_WIDTH), lambda b, c: (0, 0))],
        out_specs=pl.BlockSpec((1, L, M_V_WIDTH), lambda b, c: (b, c, 0)),
        out_shape=jax.ShapeDtypeStruct((B, S, M_V_WIDTH), BF16),
        scratch_shapes=[pltpu.VMEM((8, W), F32),
                        pltpu.VMEM((M_HEADS, M_QK_DIM, M_V_DIM), F32),
                        pltpu.VMEM((M_HEADS, 1, M_QK_DIM), F32),
                        pltpu.VMEM((M_HEADS, 1, 1), F32)],
        compiler_params=_params("parallel", "arbitrary"),
        name="mlstm",
    )(proj_a, proj_a, proj_a, proj_a, proj_a, proj_a, mif, mif_t, ifb, ifbt, conv_w, conv_b.reshape(1, W),
      norm_w.reshape(1, M_V_WIDTH))


def _merge_kernel(att_ref, hm_ref, gate_ref, h_ref, wpa_ref, wpm_ref, wo_ref, g_ref, b_ref, wrh_ref, wrl_ref,
                  br_ref, h1_ref, lg_ref):
    pa = jnp.dot(att_ref[...], wpa_ref[...], preferred_element_type=F32)
    pm = jnp.dot(hm_ref[...], wpm_ref[...], preferred_element_type=F32)
    ga = _sigmoid(gate_ref[:, :D_MODEL])
    gm = _sigmoid(gate_ref[:, D_MODEL:])
    merged = (ga * pa + gm * pm).astype(BF16)
    y = jnp.dot(merged, wo_ref[...], preferred_element_type=F32)
    h1 = _layer_norm_rows(DEEPNORM_ALPHA * h_ref[...] + y, g_ref[...], b_ref[...])
    h1_ref[...] = h1
    h1h = h1.astype(BF16)
    h1l = (h1 - h1h.astype(F32)).astype(BF16)
    lg_ref[...] = (jnp.dot(h1h, wrh_ref[...], preferred_element_type=F32)
                   + jnp.dot(h1l, wrh_ref[...], preferred_element_type=F32)
                   + jnp.dot(h1h, wrl_ref[...], preferred_element_type=F32)) + br_ref[...]


def _merge(att, hm, gate, h, wpa, wpm, wo, g1, b1, wr, br, tm=256):
    T, D = h.shape
    const = lambda i: (0, 0)
    one = pl.Buffered(1)
    wrh = wr.astype(BF16)
    wrl = (wr - wrh.astype(F32)).astype(BF16)
    return pl.pallas_call(
        _merge_kernel,
        grid=(T // tm,),
        in_specs=[pl.BlockSpec((tm, ATT_OUT_WIDTH), lambda i: (i, 0)),
                  pl.BlockSpec((tm, M_V_WIDTH), lambda i: (i, 0)),
                  pl.BlockSpec((tm, N_BRANCHES * D), lambda i: (i, 0)),
                  pl.BlockSpec((tm, D), lambda i: (i, 0)),
                  pl.BlockSpec((ATT_OUT_WIDTH, D), const, pipeline_mode=one),
                  pl.BlockSpec((M_V_WIDTH, D), const, pipeline_mode=one),
                  pl.BlockSpec((D, D), const, pipeline_mode=one),
                  pl.BlockSpec((1, D), const),
                  pl.BlockSpec((1, D), const),
                  pl.BlockSpec((D, ROUTE_LANES), const, pipeline_mode=one),
                  pl.BlockSpec((D, ROUTE_LANES), const, pipeline_mode=one),
                  pl.BlockSpec((1, ROUTE_LANES), const)],
        out_specs=[pl.BlockSpec((tm, D), lambda i: (i, 0)),
                   pl.BlockSpec((tm, ROUTE_LANES), lambda i: (i, 0))],
        out_shape=[jax.ShapeDtypeStruct((T, D), F32), jax.ShapeDtypeStruct((T, ROUTE_LANES), F32)],
        compiler_params=_params("parallel"),
        name="merge_out_ln1",
    )(att, hm, gate, h, wpa, wpm, wo, g1.reshape(1, D), b1.reshape(1, D), wrh, wrl, br)


def _route_kernel(lg_ref, e_ref, w_ref):
    lg = lg_ref[...]
    col = lax.broadcasted_iota(jnp.int32, lg.shape, 1)
    big = jnp.int32(ROUTE_LANES)

    def first_argmax(v, vmax):
        return jnp.min(jnp.where(v == vmax, col, big), axis=-1, keepdims=True)

    gl = jnp.where(col < N_GROUPS, lg, NEG)
    gmax = jnp.max(gl, axis=-1, keepdims=True)
    grp = first_argmax(gl, gmax)
    gsum = jnp.sum(jnp.where(col < N_GROUPS, jnp.exp(lg - gmax), 0.0), axis=-1, keepdims=True)
    g_w = 1.0 / gsum
    ecol = col - N_GROUPS
    egrp = lax.shift_right_arithmetic(ecol, int(math.log2(EXPERTS_PER_GROUP)))
    in_grp = (ecol >= 0) & (ecol < N_EXPERTS) & (egrp == grp)
    el = jnp.where(in_grp, lg, NEG)
    v1 = jnp.max(el, axis=-1, keepdims=True)
    i1 = first_argmax(el, v1)
    el2 = jnp.where(col == i1, NEG, el)
    v2 = jnp.max(el2, axis=-1, keepdims=True)
    i2 = first_argmax(el2, v2)
    t = jnp.exp(v2 - v1)
    p1 = 1.0 / (1.0 + t)
    p2 = t / (1.0 + t)
    e_ref[...] = jnp.where(col == 0, i1 - N_GROUPS, jnp.where(col == 1, i2 - N_GROUPS, 0))
    w_ref[...] = jnp.where(col == 0, g_w * p1, jnp.where(col == 1, g_w * p2, 0.0))


def _route(logits, tm=1024):
    T = logits.shape[0]
    spec = pl.BlockSpec((tm, ROUTE_LANES), lambda i: (i, 0))
    return pl.pallas_call(
        _route_kernel,
        grid=(T // tm,),
        in_specs=[spec],
        out_specs=[spec, spec],
        out_shape=[jax.ShapeDtypeStruct((T, ROUTE_LANES), jnp.int32),
                   jax.ShapeDtypeStruct((T, ROUTE_LANES), F32)],
        compiler_params=_params("parallel"),
        name="route",
    )(logits)


def _dispatch_plan(e_tk, T):
    M = T * TOP_K
    e_flat = e_tk.reshape(M)
    onehot = (e_flat[:, None] == jnp.arange(N_EXPERTS, dtype=jnp.int32)[None, :]).astype(jnp.int32)
    csum = jnp.cumsum(onehot, axis=0)
    counts = csum[-1]
    rank = jnp.sum((csum - onehot) * onehot, axis=1)
    padded = (counts + MOE_SUB - 1) // MOE_SUB * MOE_SUB
    pstart = jnp.cumsum(padded) - padded
    dest = jnp.sum(onehot * pstart[None, :], axis=1) + rank
    npad = M + N_EXPERTS * MOE_SUB
    tok = jnp.arange(M, dtype=jnp.int32) // TOP_K
    slot = jnp.arange(M, dtype=jnp.int32) % TOP_K
    slot_dst = jnp.zeros((npad,), jnp.int32).at[dest].set(slot * T + tok)

    nsb_max = N_EXPERTS + M // MOE_SUPER
    nsb_e = (padded + MOE_SUPER - 1) // MOE_SUPER
    sb_end = jnp.cumsum(nsb_e)
    sb_beg = sb_end - nsb_e
    total = sb_end[-1]
    sb = jnp.arange(nsb_max, dtype=jnp.int32)
    sb_c = jnp.minimum(sb, total - 1)
    ex = jnp.sum((sb_end[None, :] <= sb_c[:, None]).astype(jnp.int32), axis=1)
    local = sb_c - sb_beg[ex]
    row0 = pstart[ex] + local * MOE_SUPER
    active = sb < total
    cnt = jnp.where(active, jnp.clip(counts[ex] - local * MOE_SUPER, 0, MOE_SUPER), 0)
    nsub = jnp.where(active, jnp.clip(padded[ex] - local * MOE_SUPER, 0, MOE_SUPER) // MOE_SUB, 0)
    return (ex.astype(jnp.int32), row0.astype(jnp.int32), cnt.astype(jnp.int32), nsub.astype(jnp.int32),
            slot_dst)


def _moe_kernel(sb_ex, sb_row0, sb_cnt, sb_nsub, slot_dst,
                h1_hbm, wga_ref, wua_ref, wda_ref, wgb_ref, wub_ref, wdb_ref, y2_hbm,
                xb_buf, acc_buf, gsem, ssem, *, T):
    b = pl.program_id(0)
    j = pl.program_id(1)
    nsub = sb_nsub[b]
    row0 = sb_row0[b]
    cnt = sb_cnt[b]
    U = MOE_DMA_UNROLL

    def row_copy_in(i, tok):
        return pltpu.make_async_copy(h1_hbm.at[pl.ds(tok, 1), :], acc_buf.at[pl.ds(i, 1), :], gsem)

    def row_copy_out(i, dst):
        return pltpu.make_async_copy(acc_buf.at[pl.ds(i, 1), :], y2_hbm.at[pl.ds(dst, 1), :], ssem)

    @pl.when((j == 0) & (nsub > 0))
    def _gather():
        def issue(q, c):
            for k in range(U):
                i = q * U + k
                row_copy_in(i, lax.rem(slot_dst[row0 + i], T)).start()
            return c
        lax.fori_loop(0, nsub * (MOE_SUB // U), issue, 0)

        def wait(k, c):
            pltpu.make_async_copy(h1_hbm.at[pl.ds(0, MOE_SUB), :], acc_buf.at[pl.ds(0, MOE_SUB), :], gsem).wait()
            return c
        lax.fori_loop(0, nsub, wait, 0)

        def cast(k, c):
            rs = pl.ds(pl.multiple_of(k * MOE_SUB, MOE_SUB), MOE_SUB)
            xb_buf[rs, :] = acc_buf[rs, :].astype(BF16)
            acc_buf[rs, :] = jnp.zeros((MOE_SUB, D_MODEL), F32)
            return c
        lax.fori_loop(0, nsub, cast, 0)

    def ffn_tile(wg_ref, wu_ref, wd_ref):
        def chunk(r0, rows):
            rs = pl.ds(r0, rows)
            x = xb_buf[rs, :]
            gt = jnp.dot(x, wg_ref[0].astype(BF16), preferred_element_type=F32)
            ut = jnp.dot(x, wu_ref[0].astype(BF16), preferred_element_type=F32)
            hmid = (gt * _sigmoid(gt) * ut).astype(BF16)
            acc_buf[rs, :] += jnp.dot(hmid, wd_ref[0].astype(BF16), preferred_element_type=F32)

        def pair(k, c):
            chunk(pl.multiple_of(k * (2 * MOE_SUB), 2 * MOE_SUB), 2 * MOE_SUB)
            return c
        lax.fori_loop(0, nsub // 2, pair, 0)

        @pl.when(nsub % 2 == 1)
        def _():
            chunk(pl.multiple_of((nsub - 1) * MOE_SUB, MOE_SUB), MOE_SUB)

    @pl.when((nsub > 0) & (j < MOE_NFT))
    def _main_tiles():
        ffn_tile(wga_ref, wua_ref, wda_ref)

    @pl.when((nsub > 0) & (j == MOE_NFT))
    def _tail_tile():
        ffn_tile(wgb_ref, wub_ref, wdb_ref)

    @pl.when((j == MOE_NJ - 1) & (nsub > 0))
    def _scatter():
        nq = cnt // U

        def issue(q, c):
            for k in range(U):
                i = q * U + k
                row_copy_out(i, slot_dst[row0 + i]).start()
            return c
        lax.fori_loop(0, nq, issue, 0)

        def issue_one(i, c):
            row_copy_out(i, slot_dst[row0 + i]).start()
            return c
        lax.fori_loop(nq * U, cnt, issue_one, 0)

        def wait(q, c):
            pltpu.make_async_copy(acc_buf.at[pl.ds(0, U), :], y2_hbm.at[pl.ds(0, U), :], ssem).wait()
            return c
        lax.fori_loop(0, nq, wait, 0)

        def wait_one(i, c):
            row_copy_out(0, 0).wait()
            return c
        lax.fori_loop(nq * U, cnt, wait_one, 0)


def _moe_ffn(h1, plan, w_gate, w_up, w_down):
    T, D = h1.shape
    sb_ex, sb_row0, sb_cnt, sb_nsub, slot_dst = plan
    nsb_max = sb_ex.shape[0]
    last = MOE_NFT - 1
    tail = D_FF_EXPERT // MOE_FT_TAIL - 1

    def ja(b, j, nsub):
        return jnp.where(nsub[b] > 0, jnp.minimum(j, last), last)

    grid_spec = pltpu.PrefetchScalarGridSpec(
        num_scalar_prefetch=5,
        grid=(nsb_max, MOE_NJ),
        in_specs=[pl.BlockSpec(memory_space=pl.ANY),
                  pl.BlockSpec((1, D, MOE_FT), lambda b, j, ex, r0, ct, ns, sd: (ex[b], 0, ja(b, j, ns))),
                  pl.BlockSpec((1, D, MOE_FT), lambda b, j, ex, r0, ct, ns, sd: (ex[b], 0, ja(b, j, ns))),
                  pl.BlockSpec((1, MOE_FT, D), lambda b, j, ex, r0, ct, ns, sd: (ex[b], ja(b, j, ns), 0)),
                  pl.BlockSpec((1, D, MOE_FT_TAIL), lambda b, j, ex, r0, ct, ns, sd: (ex[b], 0, tail)),
                  pl.BlockSpec((1, D, MOE_FT_TAIL), lambda b, j, ex, r0, ct, ns, sd: (ex[b], 0, tail)),
                  pl.BlockSpec((1, MOE_FT_TAIL, D), lambda b, j, ex, r0, ct, ns, sd: (ex[b], tail, 0))],
        out_specs=pl.BlockSpec(memory_space=pl.ANY),
        scratch_shapes=[pltpu.VMEM((MOE_SUPER, D), BF16),
                        pltpu.VMEM((MOE_SUPER, D), F32),
                        pltpu.SemaphoreType.DMA(()),
                        pltpu.SemaphoreType.DMA(())],
    )
    return pl.pallas_call(
        functools.partial(_moe_kernel, T=T),
        grid_spec=grid_spec,
        out_shape=jax.ShapeDtypeStruct((TOP_K * T, D), F32),
        compiler_params=_params("arbitrary", "arbitrary"),
        name="moe_experts",
    )(sb_ex, sb_row0, sb_cnt, sb_nsub, slot_dst, h1, w_gate, w_up, w_down, w_gate, w_up, w_down)


def _ln_out_kernel(h1_ref, y0_ref, y1_ref, rw_ref, g_ref, b_ref, o_ref):
    rw = rw_ref[...]
    z = DEEPNORM_ALPHA * h1_ref[...] + rw[:, 0:1] * y0_ref[...] + rw[:, 1:2] * y1_ref[...]
    o_ref[...] = _layer_norm_rows(z, g_ref[...], b_ref[...])


def _ln_out(h1, y2, rw, g, b, tm=256):
    T, D = h1.shape
    nb = T // tm
    return pl.pallas_call(
        _ln_out_kernel,
        grid=(nb,),
        in_specs=[pl.BlockSpec((tm, D), lambda i: (i, 0)),
                  pl.BlockSpec((tm, D), lambda i: (i, 0)),
                  pl.BlockSpec((tm, D), lambda i: (i + nb, 0)),
                  pl.BlockSpec((tm, ROUTE_LANES), lambda i: (i, 0)),
                  pl.BlockSpec((1, D), lambda i: (0, 0)),
                  pl.BlockSpec((1, D), lambda i: (0, 0))],
        out_specs=pl.BlockSpec((tm, D), lambda i: (i, 0)),
        out_shape=jax.ShapeDtypeStruct((T, D), F32),
        compiler_params=_params("parallel"),
        name="combine_ln2",
    )(h1, y2, y2, rw, g.reshape(1, D), b.reshape(1, D))


def kernel(x, ln_in_g, ln_in_b, w_in, m_conv_w, m_conv_b, m_if_bias, m_norm_w, w_proj_att, w_proj_mlstm, w_out,
           ln1_g, ln1_b, w_router_group, b_router_group, w_router_expert, b_router_expert, w_gate, w_up, w_down,
           ln2_g, ln2_b):
    B, S, D = x.shape
    T = B * S
    assert D == D_MODEL and S % ATT_SUPER == 0 and w_in.shape[0] == DEPTH == 1

    h, hb = _ln_in(x.reshape(T, D), ln_in_g, ln_in_b)
    for l in range(DEPTH):
        wt = jnp.swapaxes(w_in[l], 0, 1)
        proj_a = _matmul_nt(hb, wt, 0, PROJ_A_WIDTH, 1024, 768, F32, "in_proj_a")
        gate = _matmul_nt(hb, wt, COL_GATE, N_BRANCHES * D, 1024, 512, F32, "in_proj_gate")
        mif = _matmul_nt(hb, wt, COL_MIF, ROUTE_LANES, 1024, ROUTE_LANES, F32, "in_proj_if")

        proj_a3 = proj_a.reshape(B, S, PROJ_A_WIDTH)
        att = _attention(proj_a3, B, S)
        mif3 = mif.reshape(B, S, ROUTE_LANES)
        mif_t = jnp.swapaxes(mif3[:, :, :2 * M_HEADS], 1, 2)
        hm = _mlstm(proj_a3, mif3, mif_t, m_if_bias[l], m_conv_w[l], m_conv_b[l], m_norm_w[l], B, S)

        w_r = jnp.zeros((D, ROUTE_LANES), F32)
        w_r = w_r.at[:, :N_GROUPS].set(w_router_group[l]).at[:, N_GROUPS:N_GROUPS + N_EXPERTS].set(w_router_expert[l])
        b_r = jnp.zeros((1, ROUTE_LANES), F32)
        b_r = b_r.at[0, :N_GROUPS].set(b_router_group[l]).at[0, N_GROUPS:N_GROUPS + N_EXPERTS].set(b_router_expert[l])
        h1, logits = _merge(att.reshape(T, ATT_OUT_WIDTH), hm.reshape(T, M_V_WIDTH), gate, h,
                            w_proj_att[l].astype(BF16), w_proj_mlstm[l].astype(BF16), w_out[l].astype(BF16),
                            ln1_g[l], ln1_b[l], w_r, b_r)

        e_out, rw = _route(logits)
        plan = _dispatch_plan(e_out[:, :TOP_K], T)
        y2 = _moe_ffn(h1, plan, w_gate[l], w_up[l], w_down[l])
        h = _ln_out(h1, y2, rw, ln2_g[l], ln2_b[l])
    return h.reshape(B, S, D)
```

```python
import functools
import math

import numpy as np
import jax
import jax.numpy as jnp
from jax import lax
from jax.experimental import pallas as pl
from jax.experimental.pallas import tpu as pltpu

F32 = jnp.float32
BF16 = jnp.bfloat16

D_MODEL = 2048
ATT_HEAD_DIM = 128
ATT_HEADS_PER_GROUP = 4
ATT_PATTERNS = ((128, 1), (512, 4), (2048, 16))
ATT_HEADS = ATT_HEADS_PER_GROUP * len(ATT_PATTERNS)
ATT_WIDTH = ATT_HEADS * ATT_HEAD_DIM
ATT_OUT_WIDTH = ATT_HEADS_PER_GROUP * ATT_HEAD_DIM
ATT_BLOCK = 128
ATT_SUPER = 2048

M_HEADS = 4
M_QK_DIM = 128
M_V_DIM = 256
M_QK_WIDTH = M_HEADS * M_QK_DIM
M_V_WIDTH = M_HEADS * M_V_DIM
M_CONV = 4
M_CHUNK = 128
M_MLSTM_COLS = 512

N_BRANCHES = 2
IN_PROJ_SPLITS = (ATT_WIDTH, ATT_WIDTH, ATT_WIDTH, 2 * M_QK_WIDTH, M_V_WIDTH, M_V_WIDTH,
                  2 * M_HEADS, N_BRANCHES * D_MODEL)
COL_AQ = 0
COL_AK = ATT_WIDTH
COL_AV = 2 * ATT_WIDTH
COL_MQK = 3 * ATT_WIDTH
COL_MV = COL_MQK + 2 * M_QK_WIDTH
COL_MO = COL_MV + M_V_WIDTH
COL_MIF = COL_MO + M_V_WIDTH
COL_GATE = COL_MIF + 2 * M_HEADS
PROJ_A_WIDTH = COL_MIF

N_GROUPS = 4
EXPERTS_PER_GROUP = 8
N_EXPERTS = N_GROUPS * EXPERTS_PER_GROUP
TOP_K = 2
D_FF_EXPERT = 1408
MOE_SUB = 256
MOE_SUPER = 1024
MOE_FT = 256
MOE_NFT = D_FF_EXPERT // MOE_FT
MOE_FT_TAIL = D_FF_EXPERT - MOE_NFT * MOE_FT
MOE_NJ = MOE_NFT + 1
MOE_DMA_UNROLL = 8
assert MOE_FT_TAIL > 0 and D_FF_EXPERT % MOE_FT_TAIL == 0 and MOE_FT_TAIL % 128 == 0
ROUTE_LANES = 128

DEPTH = 1
DEEPNORM_ALPHA = (2 * DEPTH) ** 0.25
LN_EPS = 1e-5
NEG = -1e30

VMEM_LIMIT = 56 * 1024 * 1024


def _alibi_slopes(n):
    def geometric(k):
        start = 2.0 ** (-8.0 / k)
        return [start ** (i + 1) for i in range(k)]
    c = 2 ** int(math.floor(math.log2(n)))
    s = geometric(c) if c == n else geometric(c) + geometric(2 * c)[0::2][: n - c]
    return np.array(sorted(s, reverse=True), dtype=np.float32)


def _params(*sem):
    return pltpu.CompilerParams(dimension_semantics=sem, vmem_limit_bytes=VMEM_LIMIT)


def _layer_norm_rows(z, g, b):
    mu = jnp.mean(z, axis=-1, keepdims=True)
    zc = z - mu
    var = jnp.mean(zc * zc, axis=-1, keepdims=True)
    return zc * lax.rsqrt(var + LN_EPS) * g + b


def _sigmoid(x):
    return 1.0 / (1.0 + jnp.exp(-x))


def _ln_in_kernel(x_ref, g_ref, b_ref, hb_ref):
    hb_ref[...] = _layer_norm_rows(x_ref[...], g_ref[...], b_ref[...]).astype(BF16)


def _ln_in(x2, g, b, tm=512):
    T, D = x2.shape
    return pl.pallas_call(
        _ln_in_kernel,
        grid=(T // tm,),
        in_specs=[pl.BlockSpec((tm, D), lambda i: (i, 0)),
                  pl.BlockSpec((1, D), lambda i: (0, 0)),
                  pl.BlockSpec((1, D), lambda i: (0, 0))],
        out_specs=pl.BlockSpec((tm, D), lambda i: (i, 0)),
        out_shape=jax.ShapeDtypeStruct((T, D), BF16),
        compiler_params=_params("parallel"),
        name="ln_in",
    )(x2, g.reshape(1, D), b.reshape(1, D))


def _mm_nt_kernel(a_ref, w_ref, o_ref, wb_ref):
    @pl.when(pl.program_id(1) == 0)
    def _():
        wb_ref[...] = w_ref[...].astype(BF16)

    o_ref[...] = lax.dot_general(a_ref[...], wb_ref[...], (((1,), (1,)), ((), ())),
                                 preferred_element_type=F32).astype(o_ref.dtype)


def _matmul_nt(a, wt, row0, n_cols, tm, tn, out_dtype, name):
    T, K = a.shape
    if row0 % tn == 0:
        w_spec = pl.BlockSpec((tn, K), lambda j, i: (j + row0 // tn, 0))
    else:
        assert row0 % 8 == 0 and tn % 8 == 0
        w_spec = pl.BlockSpec((pl.Element(tn), pl.Element(K)),
                              lambda j, i: ((row0 // 8 + j * (tn // 8)) * 8, 0))
    return pl.pallas_call(
        _mm_nt_kernel,
        grid=(n_cols // tn, T // tm),
        in_specs=[pl.BlockSpec((tm, K), lambda j, i: (i, 0)), w_spec],
        out_specs=pl.BlockSpec((tm, tn), lambda j, i: (i, j)),
        out_shape=jax.ShapeDtypeStruct((T, n_cols), out_dtype),
        scratch_shapes=[pltpu.VMEM((tn, K), BF16)],
        compiler_params=_params("parallel", "arbitrary"),
        name=name,
    )(a, wt)


ATT_UNROLL = 8


def _batched_loop(n, body):
    u = max(d for d in range(1, ATT_UNROLL + 1) if n % d == 0)
    if n == u:
        body(list(range(n)))
        return

    def step(i, c):
        body([i * u + k for k in range(u)])
        return c
    lax.fori_loop(0, n // u, step, 0)


def _attn_blocks(r, slope_r, prev_bias, q_ref, kc_ref, vc_ref, kp_ref, vp_ref, bases, g, acc_ref, m_ref, l_ref):
    def rows(start):
        return pl.ds(start, ATT_BLOCK, r) if r > 1 else pl.ds(start, ATT_BLOCK)

    dn = (((1,), (1,)), ((), ()))
    scale = ATT_HEAD_DIM ** -0.5
    qi = lax.broadcasted_iota(jnp.int32, (ATT_BLOCK, ATT_BLOCK), 0)
    ki = lax.broadcasted_iota(jnp.int32, (ATT_BLOCK, ATT_BLOCK), 1)
    dlt = (qi - ki).astype(F32)
    alibi_c = -slope_r * dlt
    alibi_p = -slope_r * (dlt + float(ATT_BLOCK)) + prev_bias

    scores = []
    for base, base_prev in bases:
        q = q_ref[0, rows(base), :].astype(BF16)
        kc = kc_ref[0, rows(base), :].astype(BF16)
        kp = kp_ref[0, rows(base_prev), :].astype(BF16)
        sc = lax.dot_general(q, kc, dn, preferred_element_type=F32) * scale + alibi_c
        sp = lax.dot_general(q, kp, dn, preferred_element_type=F32) * scale + alibi_p
        scores.append((jnp.where(ki <= qi, sc, NEG), jnp.where(ki >= qi, sp, NEG)))
    probs = []
    for sc, sp in scores:
        m = jnp.max(jnp.maximum(sc, sp), axis=-1, keepdims=True)
        pc = jnp.exp(sc - m)
        pp = jnp.exp(sp - m)
        l = jnp.sum(pc + pp, axis=-1, keepdims=True)
        probs.append((m, l, pc.astype(BF16), pp.astype(BF16)))
    outs = []
    for (base, base_prev), (m, l, pc, pp) in zip(bases, probs):
        vc = vc_ref[0, rows(base), :].astype(BF16)
        vp = vp_ref[0, rows(base_prev), :].astype(BF16)
        outs.append(jnp.dot(pc, vc, preferred_element_type=F32) + jnp.dot(pp, vp, preferred_element_type=F32))
    for (base, _), (m, l, _, _), acc in zip(bases, probs, outs):
        acc_ref[g, rows(base), :] = acc
        m_ref[g, rows(base), :] = jnp.broadcast_to(m, (ATT_BLOCK, ATT_HEAD_DIM))
        l_ref[g, rows(base), :] = jnp.broadcast_to(l, (ATT_BLOCK, ATT_HEAD_DIM))


def _attn_kernel(slopes_ref, *refs):
    ng = len(ATT_PATTERNS)
    q_refs = refs[0:ng]
    kc_refs = refs[ng:2 * ng]
    vc_refs = refs[2 * ng:3 * ng]
    kp_refs = refs[3 * ng:4 * ng]
    vp_refs = refs[4 * ng:5 * ng]
    o_ref = refs[5 * ng]
    acc_ref, m_ref, l_ref = refs[5 * ng + 1:]
    s = pl.program_id(1)
    h = pl.program_id(2)
    prev_bias = jnp.where(s > 0, 0.0, NEG).astype(F32)

    for g, (window, r) in enumerate(ATT_PATTERNS):
        assert window // r == ATT_BLOCK
        nblk = ATT_SUPER // (ATT_BLOCK * r)
        slope_r = slopes_ref[g, h] * float(r)
        common = dict(r=r, slope_r=slope_r, g=g, acc_ref=acc_ref, m_ref=m_ref, l_ref=l_ref,
                      q_ref=q_refs[g], kc_ref=kc_refs[g], vc_ref=vc_refs[g])

        def first(ps, common=common, g=g):
            _attn_blocks(prev_bias=prev_bias, kp_ref=kp_refs[g], vp_ref=vp_refs[g],
                         bases=[(p, p) for p in ps], **common)
        _batched_loop(r, first)

        if nblk > 1:
            def rest(idxs, common=common, g=g, r=r, nblk=nblk):
                bases = []
                for idx in idxs:
                    p = idx // (nblk - 1)
                    j = idx % (nblk - 1) + 1
                    base = p + j * (ATT_BLOCK * r)
                    bases.append((base, base - ATT_BLOCK * r))
                _attn_blocks(prev_bias=jnp.float32(0.0), kp_ref=kc_refs[g], vp_ref=vc_refs[g],
                             bases=bases, **common)
            _batched_loop(r * (nblk - 1), rest)

    ch = 256
    def merge(i, c):
        rs = pl.ds(pl.multiple_of(i * ch, ch), ch)
        ms = [m_ref[g, rs, :] for g in range(ng)]
        mx = functools.reduce(jnp.maximum, ms)
        num = jnp.zeros((ch, ATT_HEAD_DIM), F32)
        den = jnp.zeros((ch, ATT_HEAD_DIM), F32)
        for g in range(ng):
            w = jnp.exp(ms[g] - mx)
            num = num + w * acc_ref[g, rs, :]
            den = den + w * l_ref[g, rs, :]
        o_ref[0, rs, :] = (num / den).astype(o_ref.dtype)
        return c
    lax.fori_loop(0, ATT_SUPER // ch, merge, 0)


def _attention(proj_a, B, S):
    ng = len(ATT_PATTERNS)
    nsb = S // ATT_SUPER
    cb = ATT_HEAD_DIM
    slopes = jnp.asarray(_alibi_slopes(ATT_HEADS).reshape(ng, ATT_HEADS_PER_GROUP))

    def cur_spec(col0, g):
        return pl.BlockSpec((1, ATT_SUPER, cb),
                            lambda b, s, h, g=g, col0=col0: (b, s, col0 // cb + g * ATT_HEADS_PER_GROUP + h))

    def prev_spec(col0, g):
        rows = ATT_BLOCK * ATT_PATTERNS[g][1]
        per = ATT_SUPER // rows
        return pl.BlockSpec((1, rows, cb),
                            lambda b, s, h, g=g, col0=col0, per=per: (
                                b, jnp.maximum(s * per - 1, 0), col0 // cb + g * ATT_HEADS_PER_GROUP + h))

    in_specs = [pl.BlockSpec(memory_space=pltpu.SMEM)]
    in_specs += [cur_spec(COL_AQ, g) for g in range(ng)]
    in_specs += [cur_spec(COL_AK, g) for g in range(ng)]
    in_specs += [cur_spec(COL_AV, g) for g in range(ng)]
    in_specs += [prev_spec(COL_AK, g) for g in range(ng)]
    in_specs += [prev_spec(COL_AV, g) for g in range(ng)]
    return pl.pallas_call(
        _attn_kernel,
        grid=(B, nsb, ATT_HEADS_PER_GROUP),
        in_specs=in_specs,
        out_specs=pl.BlockSpec((1, ATT_SUPER, cb), lambda b, s, h: (b, s, h)),
        out_shape=jax.ShapeDtypeStruct((B, S, ATT_OUT_WIDTH), BF16),
        scratch_shapes=[pltpu.VMEM((ng, ATT_SUPER, cb), F32)] * 3,
        compiler_params=_params("parallel", "parallel", "parallel"),
        name="dilated_attention",
    )(slopes, *([proj_a] * (5 * ng)))


def _log_sigmoid(x):
    return jnp.minimum(x, 0.0) - jnp.log(1.0 + jnp.exp(-jnp.abs(x)))


def _mlstm_kernel(mq_ref, mk_ref, mva_ref, mvb_ref, moa_ref, mob_ref, mif_ref, mift_ref, ifb_ref, ifbt_ref,
                  cw_ref, cb_ref, nw_ref, o_ref, tail_ref, c_ref, n_ref, m_ref):
    L = M_CHUNK
    c = pl.program_id(1)
    mv_refs = (mva_ref, mvb_ref)
    mo_refs = (moa_ref, mob_ref)
    hpb = M_MLSTM_COLS // M_V_DIM

    @pl.when(c == 0)
    def _():
        tail_ref[...] = jnp.zeros_like(tail_ref)
        c_ref[...] = jnp.zeros_like(c_ref)
        n_ref[...] = jnp.zeros_like(n_ref)
        m_ref[...] = jnp.zeros_like(m_ref)

    def conv_act(x_ref, part):
        cols = slice(part * M_QK_WIDTH, (part + 1) * M_QK_WIDTH)
        x = x_ref[0]
        xx = jnp.concatenate([tail_ref[:, cols], x], axis=0)
        y = cb_ref[:, cols]
        for j in range(M_CONV):
            off = 8 - (M_CONV - 1) + j
            y = y + cw_ref[j:j + 1, cols] * xx[off:off + L, :]
        tail_ref[:, cols] = x[L - 8:, :]
        return (y * _sigmoid(y)).astype(BF16)

    q_act = conv_act(mq_ref, 0)
    k_act = conv_act(mk_ref, 1)

    gi_c = mif_ref[0] + ifb_ref[...]
    gi_r = mift_ref[0] + ifbt_ref[...]
    ti = lax.broadcasted_iota(jnp.int32, (L, L), 0)
    si = lax.broadcasted_iota(jnp.int32, (L, L), 1)
    causal = si <= ti
    tri = causal.astype(F32)
    hp = lax.Precision.HIGHEST
    bcum_c = jnp.dot(tri, _log_sigmoid(gi_c), precision=hp, preferred_element_type=F32)
    bcum_r = lax.dot_general(_log_sigmoid(gi_r), tri, (((1,), (1,)), ((), ())), precision=hp,
                             preferred_element_type=F32)

    scale = M_QK_DIM ** -0.5
    for hd in range(M_HEADS):
        q = q_act[:, hd * M_QK_DIM:(hd + 1) * M_QK_DIM]
        k = k_act[:, hd * M_QK_DIM:(hd + 1) * M_QK_DIM]
        vcols = slice((hd % hpb) * M_V_DIM, (hd % hpb + 1) * M_V_DIM)
        v = mv_refs[hd // hpb][0, :, vcols].astype(BF16)
        b_c = bcum_c[:, M_HEADS + hd:M_HEADS + hd + 1]
        i_c = gi_c[:, hd:hd + 1]
        b_r = bcum_r[M_HEADS + hd:M_HEADS + hd + 1, :]
        i_r = gi_r[hd:hd + 1, :]
        m_prev = m_ref[hd]
        c_prev = c_ref[hd]
        n_prev = n_ref[hd]

        dmat = jnp.where(causal, b_c + (i_r - b_r), NEG)
        inter = b_c + m_prev
        m_t = jnp.maximum(inter, jnp.max(dmat, axis=-1, keepdims=True))
        w_intra = jnp.exp(dmat - m_t)
        w_inter = jnp.exp(inter - m_t)
        qk = lax.dot_general(q, k, (((1,), (1,)), ((), ())), preferred_element_type=F32) * scale * w_intra
        qf = q.astype(F32)
        num = (w_inter * scale) * jnp.dot(q, c_prev.astype(BF16), preferred_element_type=F32) \
            + jnp.dot(qk.astype(BF16), v, preferred_element_type=F32)
        den = (w_inter * scale) * jnp.sum(qf * n_prev, axis=-1, keepdims=True) \
            + jnp.sum(qk, axis=-1, keepdims=True)
        hh = num / jnp.maximum(jnp.abs(den), jnp.exp(-m_t))

        b_last = b_c[L - 1:L, :]
        w_log = b_last - b_c + i_c
        m_new = jnp.maximum(b_last + m_prev, jnp.max(w_log, axis=0, keepdims=True))
        wk = jnp.exp(w_log - m_new)
        decay = jnp.exp(b_last + m_prev - m_new)
        kw = (k.astype(F32) * wk)
        c_ref[hd] = decay * c_prev + lax.dot_general(kw.astype(BF16), v, (((0,), (0,)), ((), ())),
                                                     preferred_element_type=F32)
        n_ref[hd] = decay * n_prev + jnp.sum(kw, axis=0, keepdims=True)
        m_ref[hd] = m_new

        mu = jnp.mean(hh, axis=-1, keepdims=True)
        hc = hh - mu
        var = jnp.mean(hc * hc, axis=-1, keepdims=True)
        hn = hc * lax.rsqrt(var + LN_EPS) * nw_ref[:, hd * M_V_DIM:(hd + 1) * M_V_DIM]
        og = _sigmoid(mo_refs[hd // hpb][0, :, vcols])
        o_ref[0, :, hd * M_V_DIM:(hd + 1) * M_V_DIM] = (hn * og).astype(o_ref.dtype)


def _mlstm(proj_a, mif, mif_t, if_bias, conv_w, conv_b, norm_w, B, S):
    L = M_CHUNK
    W = 2 * M_QK_WIDTH
    cw = M_MLSTM_COLS
    ifb = jnp.zeros((1, ROUTE_LANES), F32).at[0, :2 * M_HEADS].set(if_bias)
    ifbt = jnp.broadcast_to(if_bias.reshape(2 * M_HEADS, 1), (2 * M_HEADS, L))

    def col_spec(col0):
        assert col0 % cw == 0
        return pl.BlockSpec((1, L, cw), lambda b, c, col0=col0: (b, c, col0 // cw))

    return pl.pallas_call(
        _mlstm_kernel,
        grid=(B, S // L),
        in_specs=[col_spec(COL_MQK), col_spec(COL_MQK + M_QK_WIDTH),
                  col_spec(COL_MV), col_spec(COL_MV + cw),
                  col_spec(COL_MO), col_spec(COL_MO + cw),
                  pl.BlockSpec((1, L, ROUTE_LANES), lambda b, c: (b, c, 0)),
                  pl.BlockSpec((1, 2 * M_HEADS, L), lambda b, c: (b, 0, c)),
                  pl.BlockSpec((1, ROUTE_LANES), lambda b, c: (0, 0)),
                  pl.BlockSpec((2 * M_HEADS, L), lambda b, c: (0, 0)),
                  pl.BlockSpec((M_CONV, W), lambda b, c: (0, 0)),
                  pl.BlockSpec((1, W), lambda b, c: (0, 0)),
                  pl.BlockSpec((1, M_V_WIDTH), lambda b, c: (0, 0))],
        out_specs=pl.BlockSpec((1, L, M_V_WIDTH), lambda b, c: (b, c, 0)),
        out_shape=jax.ShapeDtypeStruct((B, S, M_V_WIDTH), BF16),
        scratch_shapes=[pltpu.VMEM((8, W), F32),
                        pltpu.VMEM((M_HEADS, M_QK_DIM, M_V_DIM), F32),
                        pltpu.VMEM((M_HEADS, 1, M_QK_DIM), F32),
                        pltpu.VMEM((M_HEADS, 1, 1), F32)],
        compiler_params=_params("parallel", "arbitrary"),
        name="mlstm",
    )(proj_a, proj_a, proj_a, proj_a, proj_a, proj_a, mif, mif_t, ifb, ifbt, conv_w, conv_b.reshape(1, W),
      norm_w.reshape(1, M_V_WIDTH))


def _merge_kernel(att_ref, hm_ref, gate_ref, x_ref, g0_ref, b0_ref, wpa_ref, wpm_ref, wo_ref, g_ref, b_ref,
                  wrh_ref, wrl_ref, br_ref, h1_ref, lg_ref):
    pa = jnp.dot(att_ref[...], wpa_ref[...], preferred_element_type=F32)
    pm = jnp.dot(hm_ref[...], wpm_ref[...], preferred_element_type=F32)
    ga = _sigmoid(gate_ref[:, :D_MODEL])
    gm = _sigmoid(gate_ref[:, D_MODEL:])
    merged = (ga * pa + gm * pm).astype(BF16)
    y = jnp.dot(merged, wo_ref[...], preferred_element_type=F32)
    h = _layer_norm_rows(x_ref[...], g0_ref[...], b0_ref[...])
    h1 = _layer_norm_rows(DEEPNORM_ALPHA * h + y, g_ref[...], b_ref[...])
    h1_ref[...] = h1
    h1h = h1.astype(BF16)
    h1l = (h1 - h1h.astype(F32)).astype(BF16)
    lg_ref[...] = (jnp.dot(h1h, wrh_ref[...], preferred_element_type=F32)
                   + jnp.dot(h1l, wrh_ref[...], preferred_element_type=F32)
                   + jnp.dot(h1h, wrl_ref[...], preferred_element_type=F32)) + br_ref[...]


def _merge(att, hm, gate, x2, g0, b0, wpa, wpm, wo, g1, b1, wr, br, tm=256):
    T, D = x2.shape
    const = lambda i: (0, 0)
    one = pl.Buffered(1)
    wrh = wr.astype(BF16)
    wrl = (wr - wrh.astype(F32)).astype(BF16)
    return pl.pallas_call(
        _merge_kernel,
        grid=(T // tm,),
        in_specs=[pl.BlockSpec((tm, ATT_OUT_WIDTH), lambda i: (i, 0)),
                  pl.BlockSpec((tm, M_V_WIDTH), lambda i: (i, 0)),
                  pl.BlockSpec((tm, N_BRANCHES * D), lambda i: (i, 0)),
                  pl.BlockSpec((tm, D), lambda i: (i, 0)),
                  pl.BlockSpec((1, D), const),
                  pl.BlockSpec((1, D), const),
                  pl.BlockSpec((ATT_OUT_WIDTH, D), const, pipeline_mode=one),
                  pl.BlockSpec((M_V_WIDTH, D), const, pipeline_mode=one),
                  pl.BlockSpec((D, D), const, pipeline_mode=one),
                  pl.BlockSpec((1, D), const),
                  pl.BlockSpec((1, D), const),
                  pl.BlockSpec((D, ROUTE_LANES), const, pipeline_mode=one),
                  pl.BlockSpec((D, ROUTE_LANES), const, pipeline_mode=one),
                  pl.BlockSpec((1, ROUTE_LANES), const)],
        out_specs=[pl.BlockSpec((tm, D), lambda i: (i, 0)),
                   pl.BlockSpec((tm, ROUTE_LANES), lambda i: (i, 0))],
        out_shape=[jax.ShapeDtypeStruct((T, D), F32), jax.ShapeDtypeStruct((T, ROUTE_LANES), F32)],
        compiler_params=_params("parallel"),
        name="merge_out_ln1",
    )(att, hm, gate, x2, g0.reshape(1, D), b0.reshape(1, D), wpa, wpm, wo, g1.reshape(1, D), b1.reshape(1, D),
      wrh, wrl, br)


def _route_kernel(lg_ref, e_ref, w_ref):
    lg = lg_ref[...]
    col = lax.broadcasted_iota(jnp.int32, lg.shape, 1)
    big = jnp.int32(ROUTE_LANES)

    def first_argmax(v, vmax):
        return jnp.min(jnp.where(v == vmax, col, big), axis=-1, keepdims=True)

    gl = jnp.where(col < N_GROUPS, lg, NEG)
    gmax = jnp.max(gl, axis=-1, keepdims=True)
    grp = first_argmax(gl, gmax)
    gsum = jnp.sum(jnp.where(col < N_GROUPS, jnp.exp(lg - gmax), 0.0), axis=-1, keepdims=True)
    g_w = 1.0 / gsum
    ecol = col - N_GROUPS
    egrp = lax.shift_right_arithmetic(ecol, int(math.log2(EXPERTS_PER_GROUP)))
    in_grp = (ecol >= 0) & (ecol < N_EXPERTS) & (egrp == grp)
    el = jnp.where(in_grp, lg, NEG)
    v1 = jnp.max(el, axis=-1, keepdims=True)
    i1 = first_argmax(el, v1)
    el2 = jnp.where(col == i1, NEG, el)
    v2 = jnp.max(el2, axis=-1, keepdims=True)
    i2 = first_argmax(el2, v2)
    t = jnp.exp(v2 - v1)
    p1 = 1.0 / (1.0 + t)
    p2 = t / (1.0 + t)
    e_ref[...] = jnp.where(col == 0, i1 - N_GROUPS, jnp.where(col == 1, i2 - N_GROUPS, 0))
    w_ref[...] = jnp.where(col == 0, g_w * p1, jnp.where(col == 1, g_w * p2, 0.0))


def _route(logits, tm=1024):
    T = logits.shape[0]
    spec = pl.BlockSpec((tm, ROUTE_LANES), lambda i: (i, 0))
    return pl.pallas_call(
        _route_kernel,
        grid=(T // tm,),
        in_specs=[spec],
        out_specs=[spec, spec],
        out_shape=[jax.ShapeDtypeStruct((T, ROUTE_LANES), jnp.int32),
                   jax.ShapeDtypeStruct((T, ROUTE_LANES), F32)],
        compiler_params=_params("parallel"),
        name="route",
    )(logits)


def _dispatch_plan(e_tk, T):
    M = T * TOP_K
    e_flat = e_tk.reshape(M)
    onehot = (e_flat[:, None] == jnp.arange(N_EXPERTS, dtype=jnp.int32)[None, :]).astype(jnp.int32)
    csum = jnp.cumsum(onehot, axis=0)
    counts = csum[-1]
    rank = jnp.sum((csum - onehot) * onehot, axis=1)
    padded = (counts + MOE_SUB - 1) // MOE_SUB * MOE_SUB
    pstart = jnp.cumsum(padded) - padded
    dest = jnp.sum(onehot * pstart[None, :], axis=1) + rank
    npad = M + N_EXPERTS * MOE_SUB
    tok = jnp.arange(M, dtype=jnp.int32) // TOP_K
    slot = jnp.arange(M, dtype=jnp.int32) % TOP_K
    slot_dst = jnp.zeros((npad,), jnp.int32).at[dest].set(slot * T + tok)

    nsb_max = N_EXPERTS + M // MOE_SUPER
    nsb_e = (padded + MOE_SUPER - 1) // MOE_SUPER
    sb_end = jnp.cumsum(nsb_e)
    sb_beg = sb_end - nsb_e
    total = sb_end[-1]
    sb = jnp.arange(nsb_max, dtype=jnp.int32)
    sb_c = jnp.minimum(sb, total - 1)
    ex = jnp.sum((sb_end[None, :] <= sb_c[:, None]).astype(jnp.int32), axis=1)
    local = sb_c - sb_beg[ex]
    row0 = pstart[ex] + local * MOE_SUPER
    active = sb < total
    cnt = jnp.where(active, jnp.clip(counts[ex] - local * MOE_SUPER, 0, MOE_SUPER), 0)
    nsub = jnp.where(active, jnp.clip(padded[ex] - local * MOE_SUPER, 0, MOE_SUPER) // MOE_SUB, 0)
    return (ex.astype(jnp.int32), row0.astype(jnp.int32), cnt.astype(jnp.int32), nsub.astype(jnp.int32),
            slot_dst % T, slot_dst)


def _moe_kernel(sb_ex, sb_row0, sb_cnt, sb_nsub, slot_tok, slot_dst,
                h1_hbm, wga_ref, wua_ref, wda_ref, wgb_ref, wub_ref, wdb_ref, y2_hbm,
                stage_buf, xb_buf, acc_buf, gsem, ssem):
    b = pl.program_id(0)
    j = pl.program_id(1)
    nb = pl.num_programs(0)
    nsub = sb_nsub[b]
    cnt = sb_cnt[b]
    slot = lax.rem(b, 2)
    U = MOE_DMA_UNROLL

    def gather_issue(bb):
        r0 = sb_row0[bb]

        def issue(q, c):
            i0 = pl.multiple_of(q * U, U)
            for k in range(U):
                tok = slot_tok[r0 + i0 + k]
                pltpu.make_async_copy(h1_hbm.at[pl.ds(tok, 1), :], stage_buf.at[pl.ds(i0 + k, 1), :],
                                      gsem).start()
            return c
        lax.fori_loop(0, sb_nsub[bb] * (MOE_SUB // U), issue, 0)

    def gather_wait(n_sub):
        def wait(k, c):
            pltpu.make_async_copy(h1_hbm.at[pl.ds(0, MOE_SUB), :], stage_buf.at[pl.ds(0, MOE_SUB), :],
                                  gsem).wait()
            return c
        lax.fori_loop(0, n_sub, wait, 0)

    def scatter_issue():
        r0 = sb_row0[b]
        nq = cnt // U

        def row_out(i):
            return pltpu.make_async_copy(acc_buf.at[slot, pl.ds(i, 1), :],
                                         y2_hbm.at[pl.ds(slot_dst[r0 + i], 1), :], ssem)

        def issue(q, c):
            i0 = pl.multiple_of(q * U, U)
            for k in range(U):
                row_out(i0 + k).start()
            return c
        lax.fori_loop(0, nq, issue, 0)

        def issue_one(i, c):
            row_out(i).start()
            return c
        lax.fori_loop(nq * U, cnt, issue_one, 0)

    def scatter_wait(n_rows):
        nq = n_rows // U

        def wait(q, c):
            pltpu.make_async_copy(acc_buf.at[0, pl.ds(0, U), :], y2_hbm.at[pl.ds(0, U), :], ssem).wait()
            return c
        lax.fori_loop(0, nq, wait, 0)

        def wait_one(i, c):
            pltpu.make_async_copy(acc_buf.at[0, pl.ds(0, 1), :], y2_hbm.at[pl.ds(0, 1), :], ssem).wait()
            return c
        lax.fori_loop(nq * U, n_rows, wait_one, 0)

    @pl.when(j == 0)
    def _first_step():
        @pl.when((b == 0) & (nsub > 0))
        def _():
            gather_issue(0)

        @pl.when(nsub > 0)
        def _():
            gather_wait(nsub)

            def cast(k, c):
                rs = pl.ds(pl.multiple_of(k * MOE_SUB, MOE_SUB), MOE_SUB)
                xb_buf[rs, :] = stage_buf[rs, :].astype(BF16)
                acc_buf[slot, rs, :] = jnp.zeros((MOE_SUB, D_MODEL), F32)
                return c
            lax.fori_loop(0, nsub, cast, 0)

        nxt = jnp.minimum(b + 1, nb - 1)

        @pl.when((b + 1 < nb) & (sb_nsub[nxt] > 0))
        def _():
            gather_issue(nxt)

    def ffn_tile(wg_ref, wu_ref, wd_ref):
        def chunk(r0, rows):
            rs = pl.ds(r0, rows)
            x = xb_buf[rs, :]
            gt = jnp.dot(x, wg_ref[0].astype(BF16), preferred_element_type=F32)
            ut = jnp.dot(x, wu_ref[0].astype(BF16), preferred_element_type=F32)
            hmid = (gt * _sigmoid(gt) * ut).astype(BF16)
            acc_buf[slot, rs, :] += jnp.dot(hmid, wd_ref[0].astype(BF16), preferred_element_type=F32)

        def pair(k, c):
            chunk(pl.multiple_of(k * (2 * MOE_SUB), 2 * MOE_SUB), 2 * MOE_SUB)
            return c
        lax.fori_loop(0, nsub // 2, pair, 0)

        @pl.when(nsub % 2 == 1)
        def _():
            chunk(pl.multiple_of((nsub - 1) * MOE_SUB, MOE_SUB), MOE_SUB)

    @pl.when((nsub > 0) & (j < MOE_NFT))
    def _main_tiles():
        ffn_tile(wga_ref, wua_ref, wda_ref)

    @pl.when((nsub > 0) & (j == MOE_NFT))
    def _tail_tile():
        ffn_tile(wgb_ref, wub_ref, wdb_ref)

    @pl.when(j == MOE_NJ - 1)
    def _last_step():
        prev = jnp.maximum(b - 1, 0)

        @pl.when((b > 0) & (sb_nsub[prev] > 0))
        def _():
            scatter_wait(sb_cnt[prev])

        @pl.when(nsub > 0)
        def _():
            scatter_issue()

        @pl.when((b == nb - 1) & (nsub > 0))
        def _():
            scatter_wait(cnt)


def _moe_ffn(h1, plan, w_gate, w_up, w_down):
    T, D = h1.shape
    sb_ex, sb_row0, sb_cnt, sb_nsub, slot_tok, slot_dst = plan
    nsb_max = sb_ex.shape[0]
    last = MOE_NFT - 1
    tail = D_FF_EXPERT // MOE_FT_TAIL - 1

    def ja(b, j, nsub):
        return jnp.where(nsub[b] > 0, jnp.minimum(j, last), last)

    def main_cols(b, j, ex, r0, ct, ns, st, sd):
        return (ex[b], 0, ja(b, j, ns))

    def main_rows(b, j, ex, r0, ct, ns, st, sd):
        return (ex[b], ja(b, j, ns), 0)

    def tail_cols(b, j, ex, r0, ct, ns, st, sd):
        return (ex[b], 0, tail)

    def tail_rows(b, j, ex, r0, ct, ns, st, sd):
        return (ex[b], tail, 0)

    grid_spec = pltpu.PrefetchScalarGridSpec(
        num_scalar_prefetch=6,
        grid=(nsb_max, MOE_NJ),
        in_specs=[pl.BlockSpec(memory_space=pl.ANY),
                  pl.BlockSpec((1, D, MOE_FT), main_cols),
                  pl.BlockSpec((1, D, MOE_FT), main_cols),
                  pl.BlockSpec((1, MOE_FT, D), main_rows),
                  pl.BlockSpec((1, D, MOE_FT_TAIL), tail_cols),
                  pl.BlockSpec((1, D, MOE_FT_TAIL), tail_cols),
                  pl.BlockSpec((1, MOE_FT_TAIL, D), tail_rows)],
        out_specs=pl.BlockSpec(memory_space=pl.ANY),
        scratch_shapes=[pltpu.VMEM((MOE_SUPER, D), F32),
                        pltpu.VMEM((MOE_SUPER, D), BF16),
                        pltpu.VMEM((2, MOE_SUPER, D), F32),
                        pltpu.SemaphoreType.DMA(()),
                        pltpu.SemaphoreType.DMA(())],
    )
    return pl.pallas_call(
        _moe_kernel,
        grid_spec=grid_spec,
        out_shape=jax.ShapeDtypeStruct((TOP_K * T, D), F32),
        compiler_params=_params("arbitrary", "arbitrary"),
        name="moe_experts",
    )(sb_ex, sb_row0, sb_cnt, sb_nsub, slot_tok, slot_dst, h1, w_gate, w_up, w_down, w_gate, w_up, w_down)


def _ln_out_kernel(h1_ref, y0_ref, y1_ref, rw_ref, g_ref, b_ref, o_ref):
    rw = rw_ref[...]
    z = DEEPNORM_ALPHA * h1_ref[...] + rw[:, 0:1] * y0_ref[...] + rw[:, 1:2] * y1_ref[...]
    o_ref[...] = _layer_norm_rows(z, g_ref[...], b_ref[...])


def _ln_out(h1, y2, rw, g, b, tm=256):
    T, D = h1.shape
    nb = T // tm
    return pl.pallas_call(
        _ln_out_kernel,
        grid=(nb,),
        in_specs=[pl.BlockSpec((tm, D), lambda i: (i, 0)),
                  pl.BlockSpec((tm, D), lambda i: (i, 0)),
                  pl.BlockSpec((tm, D), lambda i: (i + nb, 0)),
                  pl.BlockSpec((tm, ROUTE_LANES), lambda i: (i, 0)),
                  pl.BlockSpec((1, D), lambda i: (0, 0)),
                  pl.BlockSpec((1, D), lambda i: (0, 0))],
        out_specs=pl.BlockSpec((tm, D), lambda i: (i, 0)),
        out_shape=jax.ShapeDtypeStruct((T, D), F32),
        compiler_params=_params("parallel"),
        name="combine_ln2",
    )(h1, y2, y2, rw, g.reshape(1, D), b.reshape(1, D))


def kernel(x, ln_in_g, ln_in_b, w_in, m_conv_w, m_conv_b, m_if_bias, m_norm_w, w_proj_att, w_proj_mlstm, w_out,
           ln1_g, ln1_b, w_router_group, b_router_group, w_router_expert, b_router_expert, w_gate, w_up, w_down,
           ln2_g, ln2_b):
    B, S, D = x.shape
    T = B * S
    assert D == D_MODEL and S % ATT_SUPER == 0 and w_in.shape[0] == DEPTH == 1

    x2 = x.reshape(T, D)
    hb = _ln_in(x2, ln_in_g, ln_in_b)
    for l in range(DEPTH):
        wt = jnp.swapaxes(w_in[l], 0, 1)
        proj_a = _matmul_nt(hb, wt, 0, PROJ_A_WIDTH, 1024, 768, F32, "in_proj_a")
        gate = _matmul_nt(hb, wt, COL_GATE, N_BRANCHES * D, 1024, 512, F32, "in_proj_gate")
        mif = _matmul_nt(hb, wt, COL_MIF, ROUTE_LANES, 1024, ROUTE_LANES, F32, "in_proj_if")

        proj_a3 = proj_a.reshape(B, S, PROJ_A_WIDTH)
        att = _attention(proj_a3, B, S)
        mif3 = mif.reshape(B, S, ROUTE_LANES)
        mif_t = jnp.swapaxes(mif3[:, :, :2 * M_HEADS], 1, 2)
        hm = _mlstm(proj_a3, mif3, mif_t, m_if_bias[l], m_conv_w[l], m_conv_b[l], m_norm_w[l], B, S)

        w_r = jnp.zeros((D, ROUTE_LANES), F32)
        w_r = w_r.at[:, :N_GROUPS].set(w_router_group[l]).at[:, N_GROUPS:N_GROUPS + N_EXPERTS].set(w_router_expert[l])
        b_r = jnp.zeros((1, ROUTE_LANES), F32)
        b_r = b_r.at[0, :N_GROUPS].set(b_router_group[l]).at[0, N_GROUPS:N_GROUPS + N_EXPERTS].set(b_router_expert[l])
        h1, logits = _merge(att.reshape(T, ATT_OUT_WIDTH), hm.reshape(T, M_V_WIDTH), gate, x2, ln_in_g, ln_in_b,
                            w_proj_att[l].astype(BF16), w_proj_mlstm[l].astype(BF16), w_out[l].astype(BF16),
                            ln1_g[l], ln1_b[l], w_r, b_r)

        e_out, rw = _route(logits)
        plan = _dispatch_plan(e_out[:, :TOP_K], T)
        y2 = _moe_ffn(h1, plan, w_gate[l], w_up[l], w_down[l])
        h = _ln_out(h1, y2, rw, ln2_g[l], ln2_b[l])
    return h.reshape(B, S, D)
```

```python
import functools
import math

import numpy as np
import jax
import jax.numpy as jnp
from jax import lax
from jax.experimental import pallas as pl
from jax.experimental.pallas import tpu as pltpu

F32 = jnp.float32
BF16 = jnp.bfloat16

D_MODEL = 2048
ATT_HEAD_DIM = 128
ATT_HEADS_PER_GROUP = 4
ATT_PATTERNS = ((128, 1), (512, 4), (2048, 16))
ATT_HEADS = ATT_HEADS_PER_GROUP * len(ATT_PATTERNS)
ATT_WIDTH = ATT_HEADS * ATT_HEAD_DIM
ATT_OUT_WIDTH = ATT_HEADS_PER_GROUP * ATT_HEAD_DIM
ATT_BLOCK = 128
ATT_SUPER = 2048

M_HEADS = 4
M_QK_DIM = 128
M_V_DIM = 256
M_QK_WIDTH = M_HEADS * M_QK_DIM
M_V_WIDTH = M_HEADS * M_V_DIM
M_CONV = 4
M_CHUNK = 128
M_MLSTM_COLS = 512

N_BRANCHES = 2
IN_PROJ_SPLITS = (ATT_WIDTH, ATT_WIDTH, ATT_WIDTH, 2 * M_QK_WIDTH, M_V_WIDTH, M_V_WIDTH,
                  2 * M_HEADS, N_BRANCHES * D_MODEL)
COL_AQ = 0
COL_AK = ATT_WIDTH
COL_AV = 2 * ATT_WIDTH
COL_MQK = 3 * ATT_WIDTH
COL_MV = COL_MQK + 2 * M_QK_WIDTH
COL_MO = COL_MV + M_V_WIDTH
COL_MIF = COL_MO + M_V_WIDTH
COL_GATE = COL_MIF + 2 * M_HEADS
PROJ_A_WIDTH = COL_MIF

N_GROUPS = 4
EXPERTS_PER_GROUP = 8
N_EXPERTS = N_GROUPS * EXPERTS_PER_GROUP
TOP_K = 2
D_FF_EXPERT = 1408
MOE_SUB = 256
MOE_SUPER = 1024
MOE_FT = 256
MOE_NFT = D_FF_EXPERT // MOE_FT
MOE_FT_TAIL = D_FF_EXPERT - MOE_NFT * MOE_FT
MOE_NJ = MOE_NFT + 1
MOE_DMA_UNROLL = 32
assert MOE_DMA_UNROLL % 8 == 0 and MOE_SUB % MOE_DMA_UNROLL == 0
assert MOE_FT_TAIL > 0 and D_FF_EXPERT % MOE_FT_TAIL == 0 and MOE_FT_TAIL % 128 == 0
ROUTE_LANES = 128

DEPTH = 1
DEEPNORM_ALPHA = (2 * DEPTH) ** 0.25
LN_EPS = 1e-5
NEG = -1e30

VMEM_LIMIT = 56 * 1024 * 1024


def _alibi_slopes(n):
    def geometric(k):
        start = 2.0 ** (-8.0 / k)
        return [start ** (i + 1) for i in range(k)]
    c = 2 ** int(math.floor(math.log2(n)))
    s = geometric(c) if c == n else geometric(c) + geometric(2 * c)[0::2][: n - c]
    return np.array(sorted(s, reverse=True), dtype=np.float32)


def _params(*sem):
    return pltpu.CompilerParams(dimension_semantics=sem, vmem_limit_bytes=VMEM_LIMIT)


def _layer_norm_rows(z, g, b):
    mu = jnp.mean(z, axis=-1, keepdims=True)
    zc = z - mu
    var = jnp.mean(zc * zc, axis=-1, keepdims=True)
    return zc * lax.rsqrt(var + LN_EPS) * g + b


def _sigmoid(x):
    return 1.0 / (1.0 + jnp.exp(-x))


def _ln_in_kernel(x_ref, g_ref, b_ref, hb_ref):
    hb_ref[...] = _layer_norm_rows(x_ref[...], g_ref[...], b_ref[...]).astype(BF16)


def _ln_in(x2, g, b, tm=512):
    T, D = x2.shape
    return pl.pallas_call(
        _ln_in_kernel,
        grid=(T // tm,),
        in_specs=[pl.BlockSpec((tm, D), lambda i: (i, 0)),
                  pl.BlockSpec((1, D), lambda i: (0, 0)),
                  pl.BlockSpec((1, D), lambda i: (0, 0))],
        out_specs=pl.BlockSpec((tm, D), lambda i: (i, 0)),
        out_shape=jax.ShapeDtypeStruct((T, D), BF16),
        compiler_params=_params("parallel"),
        name="ln_in",
    )(x2, g.reshape(1, D), b.reshape(1, D))


def _mm_nt_kernel(a_ref, w_ref, o_ref, wb_ref):
    @pl.when(pl.program_id(1) == 0)
    def _():
        wb_ref[...] = w_ref[...].astype(BF16)

    o_ref[...] = lax.dot_general(a_ref[...], wb_ref[...], (((1,), (1,)), ((), ())),
                                 preferred_element_type=F32).astype(o_ref.dtype)


def _matmul_nt(a, wt, row0, n_cols, tm, tn, out_dtype, name):
    T, K = a.shape
    if row0 % tn == 0:
        w_spec = pl.BlockSpec((tn, K), lambda j, i: (j + row0 // tn, 0))
    else:
        assert row0 % 8 == 0 and tn % 8 == 0
        w_spec = pl.BlockSpec((pl.Element(tn), pl.Element(K)),
                              lambda j, i: ((row0 // 8 + j * (tn // 8)) * 8, 0))
    return pl.pallas_call(
        _mm_nt_kernel,
        grid=(n_cols // tn, T // tm),
        in_specs=[pl.BlockSpec((tm, K), lambda j, i: (i, 0)), w_spec],
        out_specs=pl.BlockSpec((tm, tn), lambda j, i: (i, j)),
        out_shape=jax.ShapeDtypeStruct((T, n_cols), out_dtype),
        scratch_shapes=[pltpu.VMEM((tn, K), BF16)],
        compiler_params=_params("parallel", "arbitrary"),
        name=name,
    )(a, wt)


ATT_UNROLL = 8


def _batched_loop(n, body):
    u = max(d for d in range(1, ATT_UNROLL + 1) if n % d == 0)
    if n == u:
        body(list(range(n)))
        return

    def step(i, c):
        body([i * u + k for k in range(u)])
        return c
    lax.fori_loop(0, n // u, step, 0)


def _attn_blocks(r, slope_r, prev_bias, q_ref, kc_ref, vc_ref, kp_ref, vp_ref, bases, g, acc_ref, m_ref, l_ref):
    def rows(start):
        return pl.ds(start, ATT_BLOCK, r) if r > 1 else pl.ds(start, ATT_BLOCK)

    dn = (((1,), (1,)), ((), ()))
    scale = ATT_HEAD_DIM ** -0.5
    qi = lax.broadcasted_iota(jnp.int32, (ATT_BLOCK, ATT_BLOCK), 0)
    ki = lax.broadcasted_iota(jnp.int32, (ATT_BLOCK, ATT_BLOCK), 1)
    dlt = (qi - ki).astype(F32)
    alibi_c = -slope_r * dlt
    alibi_p = -slope_r * (dlt + float(ATT_BLOCK)) + prev_bias

    scores = []
    for base, base_prev in bases:
        q = q_ref[0, rows(base), :].astype(BF16)
        kc = kc_ref[0, rows(base), :].astype(BF16)
        kp = kp_ref[0, rows(base_prev), :].astype(BF16)
        sc = lax.dot_general(q, kc, dn, preferred_element_type=F32) * scale + alibi_c
        sp = lax.dot_general(q, kp, dn, preferred_element_type=F32) * scale + alibi_p
        scores.append((jnp.where(ki <= qi, sc, NEG), jnp.where(ki >= qi, sp, NEG)))
    probs = []
    for sc, sp in scores:
        m = jnp.max(jnp.maximum(sc, sp), axis=-1, keepdims=True)
        pc = jnp.exp(sc - m)
        pp = jnp.exp(sp - m)
        l = jnp.sum(pc + pp, axis=-1, keepdims=True)
        probs.append((m, l, pc.astype(BF16), pp.astype(BF16)))
    outs = []
    for (base, base_prev), (m, l, pc, pp) in zip(bases, probs):
        vc = vc_ref[0, rows(base), :].astype(BF16)
        vp = vp_ref[0, rows(base_prev), :].astype(BF16)
        outs.append(jnp.dot(pc, vc, preferred_element_type=F32) + jnp.dot(pp, vp, preferred_element_type=F32))
    for (base, _), (m, l, _, _), acc in zip(bases, probs, outs):
        acc_ref[g, rows(base), :] = acc
        m_ref[g, rows(base), :] = jnp.broadcast_to(m, (ATT_BLOCK, ATT_HEAD_DIM))
        l_ref[g, rows(base), :] = jnp.broadcast_to(l, (ATT_BLOCK, ATT_HEAD_DIM))


def _attn_kernel(slopes_ref, *refs):
    ng = len(ATT_PATTERNS)
    q_refs = refs[0:ng]
    kc_refs = refs[ng:2 * ng]
    vc_refs = refs[2 * ng:3 * ng]
    kp_refs = refs[3 * ng:4 * ng]
    vp_refs = refs[4 * ng:5 * ng]
    o_ref = refs[5 * ng]
    acc_ref, m_ref, l_ref = refs[5 * ng + 1:]
    s = pl.program_id(1)
    h = pl.program_id(2)
    prev_bias = jnp.where(s > 0, 0.0, NEG).astype(F32)

    for g, (window, r) in enumerate(ATT_PATTERNS):
        assert window // r == ATT_BLOCK
        nblk = ATT_SUPER // (ATT_BLOCK * r)
        slope_r = slopes_ref[g, h] * float(r)
        common = dict(r=r, slope_r=slope_r, g=g, acc_ref=acc_ref, m_ref=m_ref, l_ref=l_ref,
                      q_ref=q_refs[g], kc_ref=kc_refs[g], vc_ref=vc_refs[g])

        def first(ps, common=common, g=g):
            _attn_blocks(prev_bias=prev_bias, kp_ref=kp_refs[g], vp_ref=vp_refs[g],
                         bases=[(p, p) for p in ps], **common)
        _batched_loop(r, first)

        if nblk > 1:
            def rest(idxs, common=common, g=g, r=r, nblk=nblk):
                bases = []
                for idx in idxs:
                    p = idx // (nblk - 1)
                    j = idx % (nblk - 1) + 1
                    base = p + j * (ATT_BLOCK * r)
                    bases.append((base, base - ATT_BLOCK * r))
                _attn_blocks(prev_bias=jnp.float32(0.0), kp_ref=kc_refs[g], vp_ref=vc_refs[g],
                             bases=bases, **common)
            _batched_loop(r * (nblk - 1), rest)

    ch = 256
    def merge(i, c):
        rs = pl.ds(pl.multiple_of(i * ch, ch), ch)
        ms = [m_ref[g, rs, :] for g in range(ng)]
        mx = functools.reduce(jnp.maximum, ms)
        num = jnp.zeros((ch, ATT_HEAD_DIM), F32)
        den = jnp.zeros((ch, ATT_HEAD_DIM), F32)
        for g in range(ng):
            w = jnp.exp(ms[g] - mx)
            num = num + w * acc_ref[g, rs, :]
            den = den + w * l_ref[g, rs, :]
        o_ref[0, rs, :] = (num / den).astype(o_ref.dtype)
        return c
    lax.fori_loop(0, ATT_SUPER // ch, merge, 0)


def _attention(proj_a, B, S):
    ng = len(ATT_PATTERNS)
    nsb = S // ATT_SUPER
    cb = ATT_HEAD_DIM
    slopes = jnp.asarray(_alibi_slopes(ATT_HEADS).reshape(ng, ATT_HEADS_PER_GROUP))

    def cur_spec(col0, g):
        return pl.BlockSpec((1, ATT_SUPER, cb),
                            lambda b, s, h, g=g, col0=col0: (b, s, col0 // cb + g * ATT_HEADS_PER_GROUP + h))

    def prev_spec(col0, g):
        rows = ATT_BLOCK * ATT_PATTERNS[g][1]
        per = ATT_SUPER // rows
        return pl.BlockSpec((1, rows, cb),
                            lambda b, s, h, g=g, col0=col0, per=per: (
                                b, jnp.maximum(s * per - 1, 0), col0 // cb + g * ATT_HEADS_PER_GROUP + h))

    in_specs = [pl.BlockSpec(memory_space=pltpu.SMEM)]
    in_specs += [cur_spec(COL_AQ, g) for g in range(ng)]
    in_specs += [cur_spec(COL_AK, g) for g in range(ng)]
    in_specs += [cur_spec(COL_AV, g) for g in range(ng)]
    in_specs += [prev_spec(COL_AK, g) for g in range(ng)]
    in_specs += [prev_spec(COL_AV, g) for g in range(ng)]
    return pl.pallas_call(
        _attn_kernel,
        grid=(B, nsb, ATT_HEADS_PER_GROUP),
        in_specs=in_specs,
        out_specs=pl.BlockSpec((1, ATT_SUPER, cb), lambda b, s, h: (b, s, h)),
        out_shape=jax.ShapeDtypeStruct((B, S, ATT_OUT_WIDTH), BF16),
        scratch_shapes=[pltpu.VMEM((ng, ATT_SUPER, cb), F32)] * 3,
        compiler_params=_params("parallel", "parallel", "parallel"),
        name="dilated_attention",
    )(slopes, *([proj_a] * (5 * ng)))


def _log_sigmoid(x):
    return jnp.minimum(x, 0.0) - jnp.log(1.0 + jnp.exp(-jnp.abs(x)))


def _mlstm_kernel(mq_ref, mk_ref, mva_ref, mvb_ref, moa_ref, mob_ref, mif_ref, mift_ref, ifb_ref, ifbt_ref,
                  cw_ref, cb_ref, nw_ref, o_ref, tail_ref, c_ref, n_ref, m_ref):
    L = M_CHUNK
    c = pl.program_id(1)
    mv_refs = (mva_ref, mvb_ref)
    mo_refs = (moa_ref, mob_ref)
    hpb = M_MLSTM_COLS // M_V_DIM

    @pl.when(c == 0)
    def _():
        tail_ref[...] = jnp.zeros_like(tail_ref)
        c_ref[...] = jnp.zeros_like(c_ref)
        n_ref[...] = jnp.zeros_like(n_ref)
        m_ref[...] = jnp.zeros_like(m_ref)

    def conv_act(x_ref, part):
        cols = slice(part * M_QK_WIDTH, (part + 1) * M_QK_WIDTH)
        x = x_ref[0]
        xx = jnp.concatenate([tail_ref[:, cols], x], axis=0)
        y = cb_ref[:, cols]
        for j in range(M_CONV):
            off = 8 - (M_CONV - 1) + j
            y = y + cw_ref[j:j + 1, cols] * xx[off:off + L, :]
        tail_ref[:, cols] = x[L - 8:, :]
        return (y * _sigmoid(y)).astype(BF16)

    q_act = conv_act(mq_ref, 0)
    k_act = conv_act(mk_ref, 1)

    gi_c = mif_ref[0] + ifb_ref[...]
    gi_r = mift_ref[0] + ifbt_ref[...]
    ti = lax.broadcasted_iota(jnp.int32, (L, L), 0)
    si = lax.broadcasted_iota(jnp.int32, (L, L), 1)
    causal = si <= ti
    tri = causal.astype(F32)
    hp = lax.Precision.HIGHEST
    bcum_c = jnp.dot(tri, _log_sigmoid(gi_c), precision=hp, preferred_element_type=F32)
    bcum_r = lax.dot_general(_log_sigmoid(gi_r), tri, (((1,), (1,)), ((), ())), precision=hp,
                             preferred_element_type=F32)

    scale = M_QK_DIM ** -0.5
    for hd in range(M_HEADS):
        q = q_act[:, hd * M_QK_DIM:(hd + 1) * M_QK_DIM]
        k = k_act[:, hd * M_QK_DIM:(hd + 1) * M_QK_DIM]
        vcols = slice((hd % hpb) * M_V_DIM, (hd % hpb + 1) * M_V_DIM)
        v = mv_refs[hd // hpb][0, :, vcols].astype(BF16)
        b_c = bcum_c[:, M_HEADS + hd:M_HEADS + hd + 1]
        i_c = gi_c[:, hd:hd + 1]
        b_r = bcum_r[M_HEADS + hd:M_HEADS + hd + 1, :]
        i_r = gi_r[hd:hd + 1, :]
        m_prev = m_ref[hd]
        c_prev = c_ref[hd]
        n_prev = n_ref[hd]

        dmat = jnp.where(causal, b_c + (i_r - b_r), NEG)
        inter = b_c + m_prev
        m_t = jnp.maximum(inter, jnp.max(dmat, axis=-1, keepdims=True))
        w_intra = jnp.exp(dmat - m_t)
        w_inter = jnp.exp(inter - m_t)
        qk = lax.dot_general(q, k, (((1,), (1,)), ((), ())), preferred_element_type=F32) * scale * w_intra
        qf = q.astype(F32)
        num = (w_inter * scale) * jnp.dot(q, c_prev.astype(BF16), preferred_element_type=F32) \
            + jnp.dot(qk.astype(BF16), v, preferred_element_type=F32)
        den = (w_inter * scale) * jnp.sum(qf * n_prev, axis=-1, keepdims=True) \
            + jnp.sum(qk, axis=-1, keepdims=True)
        hh = num / jnp.maximum(jnp.abs(den), jnp.exp(-m_t))

        b_last = b_c[L - 1:L, :]
        w_log = b_last - b_c + i_c
        m_new = jnp.maximum(b_last + m_prev, jnp.max(w_log, axis=0, keepdims=True))
        wk = jnp.exp(w_log - m_new)
        decay = jnp.exp(b_last + m_prev - m_new)
        kw = (k.astype(F32) * wk)
        c_ref[hd] = decay * c_prev + lax.dot_general(kw.astype(BF16), v, (((0,), (0,)), ((), ())),
                                                     preferred_element_type=F32)
        n_ref[hd] = decay * n_prev + jnp.sum(kw, axis=0, keepdims=True)
        m_ref[hd] = m_new

        mu = jnp.mean(hh, axis=-1, keepdims=True)
        hc = hh - mu
        var = jnp.mean(hc * hc, axis=-1, keepdims=True)
        hn = hc * lax.rsqrt(var + LN_EPS) * nw_ref[:, hd * M_V_DIM:(hd + 1) * M_V_DIM]
        og = _sigmoid(mo_refs[hd // hpb][0, :, vcols])
        o_ref[0, :, hd * M_V_DIM:(hd + 1) * M_V_DIM] = (hn * og).astype(o_ref.dtype)


def _mlstm(proj_a, mif, mif_t, if_bias, conv_w, conv_b, norm_w, B, S):
    L = M_CHUNK
    W = 2 * M_QK_WIDTH
    cw = M_MLSTM_COLS
    ifb = jnp.zeros((1, ROUTE_LANES), F32).at[0, :2 * M_HEADS].set(if_bias)
    ifbt = jnp.broadcast_to(if_bias.reshape(2 * M_HEADS, 1), (2 * M_HEADS, L))

    def col_spec(col0):
        assert col0 % cw == 0
        return pl.BlockSpec((1, L, cw), lambda b, c, col0=col0: (b, c, col0 // cw))

    return pl.pallas_call(
        _mlstm_kernel,
        grid=(B, S // L),
        in_specs=[col_spec(COL_MQK), col_spec(COL_MQK + M_QK_WIDTH),
                  col_spec(COL_MV), col_spec(COL_MV + cw),
                  col_spec(COL_MO), col_spec(COL_MO + cw),
                  pl.BlockSpec((1, L, ROUTE_LANES), lambda b, c: (b, c, 0)),
                  pl.BlockSpec((1, 2 * M_HEADS, L), lambda b, c: (b, 0, c)),
                  pl.BlockSpec((1, ROUTE_LANES), lambda b, c: (0, 0)),
                  pl.BlockSpec((2 * M_HEADS, L), lambda b, c: (0, 0)),
                  pl.BlockSpec((M_CONV, W), lambda b, c: (0, 0)),
                  pl.BlockSpec((1, W), lambda b, c: (0, 0)),
                  pl.BlockSpec((1, M_V_WIDTH), lambda b, c: (0, 0))],
        out_specs=pl.BlockSpec((1, L, M_V_WIDTH), lambda b, c: (b, c, 0)),
        out_shape=jax.ShapeDtypeStruct((B, S, M_V_WIDTH), BF16),
        scratch_shapes=[pltpu.VMEM((8, W), F32),
                        pltpu.VMEM((M_HEADS, M_QK_DIM, M_V_DIM), F32),
                        pltpu.VMEM((M_HEADS, 1, M_QK_DIM), F32),
                        pltpu.VMEM((M_HEADS, 1, 1), F32)],
        compiler_params=_params("parallel", "arbitrary"),
        name="mlstm",
    )(proj_a, proj_a, proj_a, proj_a, proj_a, proj_a, mif, mif_t, ifb, ifbt, conv_w, conv_b.reshape(1, W),
      norm_w.reshape(1, M_V_WIDTH))


def _merge_kernel(att_ref, hm_ref, gate_ref, x_ref, g0_ref, b0_ref, wpa_ref, wpm_ref, wo_ref, g_ref, b_ref,
                  wrh_ref, wrl_ref, br_ref, h1_ref, lg_ref):
    pa = jnp.dot(att_ref[...], wpa_ref[...], preferred_element_type=F32)
    pm = jnp.dot(hm_ref[...], wpm_ref[...], preferred_element_type=F32)
    ga = _sigmoid(gate_ref[:, :D_MODEL])
    gm = _sigmoid(gate_ref[:, D_MODEL:])
    merged = (ga * pa + gm * pm).astype(BF16)
    y = jnp.dot(merged, wo_ref[...], preferred_element_type=F32)
    h = _layer_norm_rows(x_ref[...], g0_ref[...], b0_ref[...])
    h1 = _layer_norm_rows(DEEPNORM_ALPHA * h + y, g_ref[...], b_ref[...])
    h1_ref[...] = h1
    h1h = h1.astype(BF16)
    h1l = (h1 - h1h.astype(F32)).astype(BF16)
    lg_ref[...] = (jnp.dot(h1h, wrh_ref[...], preferred_element_type=F32)
                   + jnp.dot(h1l, wrh_ref[...], preferred_element_type=F32)
                   + jnp.dot(h1h, wrl_ref[...], preferred_element_type=F32)) + br_ref[...]


def _merge(att, hm, gate, x2, g0, b0, wpa, wpm, wo, g1, b1, wr, br, tm=256):
    T, D = x2.shape
    const = lambda i: (0, 0)
    one = pl.Buffered(1)
    wrh = wr.astype(BF16)
    wrl = (wr - wrh.astype(F32)).astype(BF16)
    return pl.pallas_call(
        _merge_kernel,
        grid=(T // tm,),
        in_specs=[pl.BlockSpec((tm, ATT_OUT_WIDTH), lambda i: (i, 0)),
                  pl.BlockSpec((tm, M_V_WIDTH), lambda i: (i, 0)),
                  pl.BlockSpec((tm, N_BRANCHES * D), lambda i: (i, 0)),
                  pl.BlockSpec((tm, D), lambda i: (i, 0)),
                  pl.BlockSpec((1, D), const),
                  pl.BlockSpec((1, D), const),
                  pl.BlockSpec((ATT_OUT_WIDTH, D), const, pipeline_mode=one),
                  pl.BlockSpec((M_V_WIDTH, D), const, pipeline_mode=one),
                  pl.BlockSpec((D, D), const, pipeline_mode=one),
                  pl.BlockSpec((1, D), const),
                  pl.BlockSpec((1, D), const),
                  pl.BlockSpec((D, ROUTE_LANES), const, pipeline_mode=one),
                  pl.BlockSpec((D, ROUTE_LANES), const, pipeline_mode=one),
                  pl.BlockSpec((1, ROUTE_LANES), const)],
        out_specs=[pl.BlockSpec((tm, D), lambda i: (i, 0)),
                   pl.BlockSpec((tm, ROUTE_LANES), lambda i: (i, 0))],
        out_shape=[jax.ShapeDtypeStruct((T, D), F32), jax.ShapeDtypeStruct((T, ROUTE_LANES), F32)],
        compiler_params=_params("parallel"),
        name="merge_out_ln1",
    )(att, hm, gate, x2, g0.reshape(1, D), b0.reshape(1, D), wpa, wpm, wo, g1.reshape(1, D), b1.reshape(1, D),
      wrh, wrl, br)


def _route_kernel(lg_ref, e_ref, w_ref):
    lg = lg_ref[...]
    col = lax.broadcasted_iota(jnp.int32, lg.shape, 1)
    big = jnp.int32(ROUTE_LANES)

    def first_argmax(v, vmax):
        return jnp.min(jnp.where(v == vmax, col, big), axis=-1, keepdims=True)

    gl = jnp.where(col < N_GROUPS, lg, NEG)
    gmax = jnp.max(gl, axis=-1, keepdims=True)
    grp = first_argmax(gl, gmax)
    gsum = jnp.sum(jnp.where(col < N_GROUPS, jnp.exp(lg - gmax), 0.0), axis=-1, keepdims=True)
    g_w = 1.0 / gsum
    ecol = col - N_GROUPS
    egrp = lax.shift_right_arithmetic(ecol, int(math.log2(EXPERTS_PER_GROUP)))
    in_grp = (ecol >= 0) & (ecol < N_EXPERTS) & (egrp == grp)
    el = jnp.where(in_grp, lg, NEG)
    v1 = jnp.max(el, axis=-1, keepdims=True)
    i1 = first_argmax(el, v1)
    el2 = jnp.where(col == i1, NEG, el)
    v2 = jnp.max(el2, axis=-1, keepdims=True)
    i2 = first_argmax(el2, v2)
    t = jnp.exp(v2 - v1)
    p1 = 1.0 / (1.0 + t)
    p2 = t / (1.0 + t)
    e_ref[...] = jnp.where(col == 0, i1 - N_GROUPS, jnp.where(col == 1, i2 - N_GROUPS, 0))
    w_ref[...] = jnp.where(col == 0, g_w * p1, jnp.where(col == 1, g_w * p2, 0.0))


def _route(logits, tm=1024):
    T = logits.shape[0]
    spec = pl.BlockSpec((tm, ROUTE_LANES), lambda i: (i, 0))
    return pl.pallas_call(
        _route_kernel,
        grid=(T // tm,),
        in_specs=[spec],
        out_specs=[spec, spec],
        out_shape=[jax.ShapeDtypeStruct((T, ROUTE_LANES), jnp.int32),
                   jax.ShapeDtypeStruct((T, ROUTE_LANES), F32)],
        compiler_params=_params("parallel"),
        name="route",
    )(logits)


def _dispatch_plan(e_tk, T):
    M = T * TOP_K
    e_flat = e_tk.reshape(M)
    onehot = (e_flat[:, None] == jnp.arange(N_EXPERTS, dtype=jnp.int32)[None, :]).astype(jnp.int32)
    csum = jnp.cumsum(onehot, axis=0)
    counts = csum[-1]
    rank = jnp.sum((csum - onehot) * onehot, axis=1)
    padded = (counts + MOE_SUB - 1) // MOE_SUB * MOE_SUB
    pstart = jnp.cumsum(padded) - padded
    dest = jnp.sum(onehot * pstart[None, :], axis=1) + rank

    nsb_max = N_EXPERTS + M // MOE_SUPER
    nsb_e = (padded + MOE_SUPER - 1) // MOE_SUPER
    sb_end = jnp.cumsum(nsb_e)
    sb_beg = sb_end - nsb_e
    total = sb_end[-1]
    sb = jnp.arange(nsb_max, dtype=jnp.int32)
    sb_c = jnp.minimum(sb, total - 1)
    ex = jnp.sum((sb_end[None, :] <= sb_c[:, None]).astype(jnp.int32), axis=1)
    local = sb_c - sb_beg[ex]
    row0 = pstart[ex] + local * MOE_SUPER
    active = sb < total
    cnt = jnp.where(active, jnp.clip(counts[ex] - local * MOE_SUPER, 0, MOE_SUPER), 0)
    nsub = jnp.where(active, jnp.clip(padded[ex] - local * MOE_SUPER, 0, MOE_SUPER) // MOE_SUB, 0)
    return (ex.astype(jnp.int32), row0.astype(jnp.int32), cnt.astype(jnp.int32), nsub.astype(jnp.int32),
            dest.astype(jnp.int32))


def _moe_kernel(sb_ex, sb_row0, sb_cnt, sb_nsub, dest,
                h1_hbm, wga_ref, wua_ref, wda_ref, wgb_ref, wub_ref, wdb_ref, y2_hbm,
                stage_buf, xb_buf, acc_buf, slot_tok, slot_dst, gsem, ssem, *, T):
    b = pl.program_id(0)
    j = pl.program_id(1)
    nb = pl.num_programs(0)
    nsub = sb_nsub[b]
    cnt = sb_cnt[b]
    slot = lax.rem(b, 2)
    U = MOE_DMA_UNROLL

    def gather_batches(bb):
        return (sb_cnt[bb] + (U - 1)) // U

    def gather_issue(bb):
        r0 = sb_row0[bb]

        def pad_row(i, c):
            slot_tok[r0 + i] = 0
            return c
        lax.fori_loop(sb_cnt[bb], gather_batches(bb) * U, pad_row, 0)

        def issue(q, c):
            i0 = pl.multiple_of(q * U, U)
            for k in range(U):
                tok = slot_tok[r0 + i0 + k]
                pltpu.make_async_copy(h1_hbm.at[lax.shift_right_logical(tok, 3), pl.ds(tok & 7, 1), :],
                                      stage_buf.at[q * (U // 8) + k // 8, pl.ds(k % 8, 1), :], gsem).start()
            return c
        lax.fori_loop(0, gather_batches(bb), issue, 0)

    def gather_wait(bb):
        def wait(k, c):
            pltpu.make_async_copy(h1_hbm.at[pl.ds(0, U // 8)], stage_buf.at[pl.ds(0, U // 8)], gsem).wait()
            return c
        lax.fori_loop(0, gather_batches(bb), wait, 0)

    def build_tables():
        n_asg = dest.shape[0]
        step = 16

        def fill(q, c):
            for k in range(step):
                d = dest[q * step + k]
                tok = q * (step // TOP_K) + k // TOP_K
                slot_tok[d] = tok
                slot_dst[d] = (k % TOP_K) * T + tok
            return c
        lax.fori_loop(0, n_asg // step, fill, 0)

    def scatter_issue():
        r0 = sb_row0[b]
        nq = cnt // U

        def row_out(i):
            return pltpu.make_async_copy(acc_buf.at[slot, pl.ds(i, 1), :],
                                         y2_hbm.at[pl.ds(slot_dst[r0 + i], 1), :], ssem)

        def issue(q, c):
            i0 = pl.multiple_of(q * U, U)
            for k in range(U):
                row_out(i0 + k).start()
            return c
        lax.fori_loop(0, nq, issue, 0)

        def issue_one(i, c):
            row_out(i).start()
            return c
        lax.fori_loop(nq * U, cnt, issue_one, 0)

    def scatter_wait(n_rows):
        nq = n_rows // U

        def wait(q, c):
            pltpu.make_async_copy(acc_buf.at[0, pl.ds(0, U), :], y2_hbm.at[pl.ds(0, U), :], ssem).wait()
            return c
        lax.fori_loop(0, nq, wait, 0)

        def wait_one(i, c):
            pltpu.make_async_copy(acc_buf.at[0, pl.ds(0, 1), :], y2_hbm.at[pl.ds(0, 1), :], ssem).wait()
            return c
        lax.fori_loop(nq * U, n_rows, wait_one, 0)

    @pl.when(j == 0)
    def _first_step():
        @pl.when(b == 0)
        def _():
            build_tables()

            def clear(q, c):
                stage_buf[q] = jnp.zeros((8, D_MODEL), F32)
                return c
            lax.fori_loop(0, stage_buf.shape[0], clear, 0)
            gather_issue(0)

        @pl.when(nsub > 0)
        def _():
            gather_wait(b)

            def cast(k, c):
                rs = pl.ds(pl.multiple_of(k * MOE_SUB, MOE_SUB), MOE_SUB)
                tiles = pl.ds(pl.multiple_of(k * (MOE_SUB // 8), MOE_SUB // 8), MOE_SUB // 8)
                xb_buf[rs, :] = stage_buf[tiles].reshape(MOE_SUB, D_MODEL).astype(BF16)
                acc_buf[slot, rs, :] = jnp.zeros((MOE_SUB, D_MODEL), F32)
                return c
            lax.fori_loop(0, nsub, cast, 0)

        nxt = jnp.minimum(b + 1, nb - 1)

        @pl.when((b + 1 < nb) & (sb_nsub[nxt] > 0))
        def _():
            gather_issue(nxt)

    def ffn_tile(wg_ref, wu_ref, wd_ref):
        def chunk(r0, rows):
            rs = pl.ds(r0, rows)
            x = xb_buf[rs, :]
            gt = jnp.dot(x, wg_ref[0].astype(BF16), preferred_element_type=F32)
            ut = jnp.dot(x, wu_ref[0].astype(BF16), preferred_element_type=F32)
            hmid = (gt * _sigmoid(gt) * ut).astype(BF16)
            acc_buf[slot, rs, :] += jnp.dot(hmid, wd_ref[0].astype(BF16), preferred_element_type=F32)

        def pair(k, c):
            chunk(pl.multiple_of(k * (2 * MOE_SUB), 2 * MOE_SUB), 2 * MOE_SUB)
            return c
        lax.fori_loop(0, nsub // 2, pair, 0)

        @pl.when(nsub % 2 == 1)
        def _():
            chunk(pl.multiple_of((nsub - 1) * MOE_SUB, MOE_SUB), MOE_SUB)

    @pl.when((nsub > 0) & (j < MOE_NFT))
    def _main_tiles():
        ffn_tile(wga_ref, wua_ref, wda_ref)

    @pl.when((nsub > 0) & (j == MOE_NFT))
    def _tail_tile():
        ffn_tile(wgb_ref, wub_ref, wdb_ref)

    @pl.when(j == MOE_NJ - 1)
    def _last_step():
        prev = jnp.maximum(b - 1, 0)

        @pl.when((b > 0) & (sb_nsub[prev] > 0))
        def _():
            scatter_wait(sb_cnt[prev])

        @pl.when(nsub > 0)
        def _():
            scatter_issue()

        @pl.when((b == nb - 1) & (nsub > 0))
        def _():
            scatter_wait(cnt)


def _moe_ffn(h1, plan, w_gate, w_up, w_down):
    T, D = h1.shape
    sb_ex, sb_row0, sb_cnt, sb_nsub, dest = plan
    nsb_max = sb_ex.shape[0]
    n_rows = dest.shape[0] + N_EXPERTS * MOE_SUB
    last = MOE_NFT - 1
    tail = D_FF_EXPERT // MOE_FT_TAIL - 1

    def ja(b, j, nsub):
        return jnp.where(nsub[b] > 0, jnp.minimum(j, last), last)

    def main_cols(b, j, ex, r0, ct, ns, ds):
        return (ex[b], 0, ja(b, j, ns))

    def main_rows(b, j, ex, r0, ct, ns, ds):
        return (ex[b], ja(b, j, ns), 0)

    def tail_cols(b, j, ex, r0, ct, ns, ds):
        return (ex[b], 0, tail)

    def tail_rows(b, j, ex, r0, ct, ns, ds):
        return (ex[b], tail, 0)

    grid_spec = pltpu.PrefetchScalarGridSpec(
        num_scalar_prefetch=5,
        grid=(nsb_max, MOE_NJ),
        in_specs=[pl.BlockSpec(memory_space=pl.ANY),
                  pl.BlockSpec((1, D, MOE_FT), main_cols),
                  pl.BlockSpec((1, D, MOE_FT), main_cols),
                  pl.BlockSpec((1, MOE_FT, D), main_rows),
                  pl.BlockSpec((1, D, MOE_FT_TAIL), tail_cols),
                  pl.BlockSpec((1, D, MOE_FT_TAIL), tail_cols),
                  pl.BlockSpec((1, MOE_FT_TAIL, D), tail_rows)],
        out_specs=pl.BlockSpec(memory_space=pl.ANY),
        scratch_shapes=[pltpu.VMEM((MOE_SUPER // 8, 8, D), F32),
                        pltpu.VMEM((MOE_SUPER, D), BF16),
                        pltpu.VMEM((2, MOE_SUPER, D), F32),
                        pltpu.SMEM((n_rows,), jnp.int32),
                        pltpu.SMEM((n_rows,), jnp.int32),
                        pltpu.SemaphoreType.DMA(()),
                        pltpu.SemaphoreType.DMA(())],
    )
    return pl.pallas_call(
        functools.partial(_moe_kernel, T=T),
        grid_spec=grid_spec,
        out_shape=jax.ShapeDtypeStruct((TOP_K * T, D), F32),
        compiler_params=_params("arbitrary", "arbitrary"),
        name="moe_experts",
    )(sb_ex, sb_row0, sb_cnt, sb_nsub, dest, h1.reshape(T // 8, 8, D),
      w_gate, w_up, w_down, w_gate, w_up, w_down)


def _ln_out_kernel(h1_ref, y0_ref, y1_ref, rw_ref, g_ref, b_ref, o_ref):
    rw = rw_ref[...]
    z = DEEPNORM_ALPHA * h1_ref[...] + rw[:, 0:1] * y0_ref[...] + rw[:, 1:2] * y1_ref[...]
    o_ref[...] = _layer_norm_rows(z, g_ref[...], b_ref[...])


def _ln_out(h1, y2, rw, g, b, tm=256):
    T, D = h1.shape
    nb = T // tm
    return pl.pallas_call(
        _ln_out_kernel,
        grid=(nb,),
        in_specs=[pl.BlockSpec((tm, D), lambda i: (i, 0)),
                  pl.BlockSpec((tm, D), lambda i: (i, 0)),
                  pl.BlockSpec((tm, D), lambda i: (i + nb, 0)),
                  pl.BlockSpec((tm, ROUTE_LANES), lambda i: (i, 0)),
                  pl.BlockSpec((1, D), lambda i: (0, 0)),
                  pl.BlockSpec((1, D), lambda i: (0, 0))],
        out_specs=pl.BlockSpec((tm, D), lambda i: (i, 0)),
        out_shape=jax.ShapeDtypeStruct((T, D), F32),
        compiler_params=_params("parallel"),
        name="combine_ln2",
    )(h1, y2, y2, rw, g.reshape(1, D), b.reshape(1, D))


def kernel(x, ln_in_g, ln_in_b, w_in, m_conv_w, m_conv_b, m_if_bias, m_norm_w, w_proj_att, w_proj_mlstm, w_out,
           ln1_g, ln1_b, w_router_group, b_router_group, w_router_expert, b_router_expert, w_gate, w_up, w_down,
           ln2_g, ln2_b):
    B, S, D = x.shape
    T = B * S
    assert D == D_MODEL and S % ATT_SUPER == 0 and w_in.shape[0] == DEPTH == 1

    x2 = x.reshape(T, D)
    hb = _ln_in(x2, ln_in_g, ln_in_b)
    for l in range(DEPTH):
        wt = jnp.swapaxes(w_in[l], 0, 1)
        proj_a = _matmul_nt(hb, wt, 0, PROJ_A_WIDTH, 1024, 768, F32, "in_proj_a")
        gate = _matmul_nt(hb, wt, COL_GATE, N_BRANCHES * D, 1024, 512, F32, "in_proj_gate")
        mif = _matmul_nt(hb, wt, COL_MIF, ROUTE_LANES, 1024, ROUTE_LANES, F32, "in_proj_if")

        proj_a3 = proj_a.reshape(B, S, PROJ_A_WIDTH)
        att = _attention(proj_a3, B, S)
        mif3 = mif.reshape(B, S, ROUTE_LANES)
        mif_t = jnp.swapaxes(mif3[:, :, :2 * M_HEADS], 1, 2)
        hm = _mlstm(proj_a3, mif3, mif_t, m_if_bias[l], m_conv_w[l], m_conv_b[l], m_norm_w[l], B, S)

        lane_pad = ROUTE_LANES - N_GROUPS - N_EXPERTS
        w_r = jnp.pad(jnp.concatenate([w_router_group[l], w_router_expert[l]], axis=1), ((0, 0), (0, lane_pad)))
        b_r = jnp.pad(jnp.concatenate([b_router_group[l], b_router_expert[l]]), (0, lane_pad)).reshape(1, ROUTE_LANES)
        h1, logits = _merge(att.reshape(T, ATT_OUT_WIDTH), hm.reshape(T, M_V_WIDTH), gate, x2, ln_in_g, ln_in_b,
                            w_proj_att[l].astype(BF16), w_proj_mlstm[l].astype(BF16), w_out[l].astype(BF16),
                            ln1_g[l], ln1_b[l], w_r, b_r)

        e_out, rw = _route(logits)
        plan = _dispatch_plan(e_out[:, :TOP_K], T)
        y2 = _moe_ffn(h1, plan, w_gate[l], w_up[l], w_down[l])
        h = _ln_out(h1, y2, rw, ln2_g[l], ln2_b[l])
    return h.reshape(B, S, D)
```

```python
import functools
import math

import numpy as np
import jax
import jax.numpy as jnp
from jax import lax
from jax.experimental import pallas as pl
from jax.experimental.pallas import tpu as pltpu

F32 = jnp.float32
BF16 = jnp.bfloat16

D_MODEL = 2048
ATT_HEAD_DIM = 128
ATT_HEADS_PER_GROUP = 4
ATT_PATTERNS = ((128, 1), (512, 4), (2048, 16))
ATT_HEADS = ATT_HEADS_PER_GROUP * len(ATT_PATTERNS)
ATT_WIDTH = ATT_HEADS * ATT_HEAD_DIM
ATT_OUT_WIDTH = ATT_HEADS_PER_GROUP * ATT_HEAD_DIM
ATT_BLOCK = 128
ATT_SUPER = 2048

M_HEADS = 4
M_QK_DIM = 128
M_V_DIM = 256
M_QK_WIDTH = M_HEADS * M_QK_DIM
M_V_WIDTH = M_HEADS * M_V_DIM
M_CONV = 4
M_CHUNK = 128
M_MLSTM_COLS = 512

N_BRANCHES = 2
IN_PROJ_SPLITS = (ATT_WIDTH, ATT_WIDTH, ATT_WIDTH, 2 * M_QK_WIDTH, M_V_WIDTH, M_V_WIDTH,
                  2 * M_HEADS, N_BRANCHES * D_MODEL)
COL_AQ = 0
COL_AK = ATT_WIDTH
COL_AV = 2 * ATT_WIDTH
COL_MQK = 3 * ATT_WIDTH
COL_MV = COL_MQK + 2 * M_QK_WIDTH
COL_MO = COL_MV + M_V_WIDTH
COL_MIF = COL_MO + M_V_WIDTH
COL_GATE = COL_MIF + 2 * M_HEADS
PROJ_A_WIDTH = COL_MIF

N_GROUPS = 4
EXPERTS_PER_GROUP = 8
N_EXPERTS = N_GROUPS * EXPERTS_PER_GROUP
TOP_K = 2
D_FF_EXPERT = 1408
MOE_SUB = 128
MOE_SUPER = 1024
MOE_CHUNK = 512
MOE_WHOLE_MAX = 768
MOE_FT = 256
MOE_NFT = D_FF_EXPERT // MOE_FT
MOE_FT_TAIL = D_FF_EXPERT - MOE_NFT * MOE_FT
MOE_NJ = MOE_NFT
MOE_DMA_UNROLL = 32
assert MOE_DMA_UNROLL % 8 == 0 and MOE_SUB % MOE_DMA_UNROLL == 0
assert MOE_FT_TAIL > 0 and D_FF_EXPERT % MOE_FT_TAIL == 0 and MOE_FT_TAIL % 128 == 0
ROUTE_LANES = 128

DEPTH = 1
DEEPNORM_ALPHA = (2 * DEPTH) ** 0.25
LN_EPS = 1e-5
NEG = -1e30

VMEM_LIMIT = 56 * 1024 * 1024


def _alibi_slopes(n):
    def geometric(k):
        start = 2.0 ** (-8.0 / k)
        return [start ** (i + 1) for i in range(k)]
    c = 2 ** int(math.floor(math.log2(n)))
    s = geometric(c) if c == n else geometric(c) + geometric(2 * c)[0::2][: n - c]
    return np.array(sorted(s, reverse=True), dtype=np.float32)


def _params(*sem):
    return pltpu.CompilerParams(dimension_semantics=sem, vmem_limit_bytes=VMEM_LIMIT)


def _layer_norm_rows(z, g, b):
    mu = jnp.mean(z, axis=-1, keepdims=True)
    zc = z - mu
    var = jnp.mean(zc * zc, axis=-1, keepdims=True)
    return zc * lax.rsqrt(var + LN_EPS) * g + b


def _sigmoid(x):
    return 1.0 / (1.0 + jnp.exp(-x))


def _ln_in_kernel(x_ref, g_ref, b_ref, hb_ref):
    hb_ref[...] = _layer_norm_rows(x_ref[...], g_ref[...], b_ref[...]).astype(BF16)


def _ln_in(x2, g, b, tm=512):
    T, D = x2.shape
    return pl.pallas_call(
        _ln_in_kernel,
        grid=(T // tm,),
        in_specs=[pl.BlockSpec((tm, D), lambda i: (i, 0)),
                  pl.BlockSpec((1, D), lambda i: (0, 0)),
                  pl.BlockSpec((1, D), lambda i: (0, 0))],
        out_specs=pl.BlockSpec((tm, D), lambda i: (i, 0)),
        out_shape=jax.ShapeDtypeStruct((T, D), BF16),
        compiler_params=_params("parallel"),
        name="ln_in",
    )(x2, g.reshape(1, D), b.reshape(1, D))


def _mm_nt_kernel(a_ref, w_ref, o_ref, wb_ref):
    @pl.when(pl.program_id(1) == 0)
    def _():
        wb_ref[...] = w_ref[...].astype(BF16)

    o_ref[...] = lax.dot_general(a_ref[...], wb_ref[...], (((1,), (1,)), ((), ())),
                                 preferred_element_type=F32).astype(o_ref.dtype)


def _matmul_nt(a, wt, row0, n_cols, tm, tn, out_dtype, name):
    T, K = a.shape
    if row0 % tn == 0:
        w_spec = pl.BlockSpec((tn, K), lambda j, i: (j + row0 // tn, 0))
    else:
        assert row0 % 8 == 0 and tn % 8 == 0
        w_spec = pl.BlockSpec((pl.Element(tn), pl.Element(K)),
                              lambda j, i: ((row0 // 8 + j * (tn // 8)) * 8, 0))
    return pl.pallas_call(
        _mm_nt_kernel,
        grid=(n_cols // tn, T // tm),
        in_specs=[pl.BlockSpec((tm, K), lambda j, i: (i, 0)), w_spec],
        out_specs=pl.BlockSpec((tm, tn), lambda j, i: (i, j)),
        out_shape=jax.ShapeDtypeStruct((T, n_cols), out_dtype),
        scratch_shapes=[pltpu.VMEM((tn, K), BF16)],
        compiler_params=_params("parallel", "arbitrary"),
        name=name,
    )(a, wt)


ATT_UNROLL = 8


def _batched_loop(n, body):
    u = max(d for d in range(1, ATT_UNROLL + 1) if n % d == 0)
    if n == u:
        body(list(range(n)))
        return

    def step(i, c):
        body([i * u + k for k in range(u)])
        return c
    lax.fori_loop(0, n // u, step, 0)


def _attn_blocks(r, slope_r, prev_bias, q_ref, kc_ref, vc_ref, kp_ref, vp_ref, bases, g, acc_ref, m_ref, l_ref):
    def rows(start):
        return pl.ds(start, ATT_BLOCK, r) if r > 1 else pl.ds(start, ATT_BLOCK)

    dn = (((1,), (1,)), ((), ()))
    scale = ATT_HEAD_DIM ** -0.5
    qi = lax.broadcasted_iota(jnp.int32, (ATT_BLOCK, ATT_BLOCK), 0)
    ki = lax.broadcasted_iota(jnp.int32, (ATT_BLOCK, ATT_BLOCK), 1)
    dlt = (qi - ki).astype(F32)
    alibi_c = -slope_r * dlt
    alibi_p = -slope_r * (dlt + float(ATT_BLOCK)) + prev_bias

    scores = []
    for base, base_prev in bases:
        q = q_ref[0, rows(base), :].astype(BF16)
        kc = kc_ref[0, rows(base), :].astype(BF16)
        kp = kp_ref[0, rows(base_prev), :].astype(BF16)
        sc = lax.dot_general(q, kc, dn, preferred_element_type=F32) * scale + alibi_c
        sp = lax.dot_general(q, kp, dn, preferred_element_type=F32) * scale + alibi_p
        scores.append((jnp.where(ki <= qi, sc, NEG), jnp.where(ki >= qi, sp, NEG)))
    probs = []
    for sc, sp in scores:
        m = jnp.max(jnp.maximum(sc, sp), axis=-1, keepdims=True)
        pc = jnp.exp(sc - m)
        pp = jnp.exp(sp - m)
        l = jnp.sum(pc + pp, axis=-1, keepdims=True)
        probs.append((m, l, pc.astype(BF16), pp.astype(BF16)))
    outs = []
    for (base, base_prev), (m, l, pc, pp) in zip(bases, probs):
        vc = vc_ref[0, rows(base), :].astype(BF16)
        vp = vp_ref[0, rows(base_prev), :].astype(BF16)
        outs.append(jnp.dot(pc, vc, preferred_element_type=F32) + jnp.dot(pp, vp, preferred_element_type=F32))
    for (base, _), (m, l, _, _), acc in zip(bases, probs, outs):
        acc_ref[g, rows(base), :] = acc
        m_ref[g, rows(base), :] = jnp.broadcast_to(m, (ATT_BLOCK, ATT_HEAD_DIM))
        l_ref[g, rows(base), :] = jnp.broadcast_to(l, (ATT_BLOCK, ATT_HEAD_DIM))


def _attn_kernel(slopes_ref, *refs):
    ng = len(ATT_PATTERNS)
    q_refs = refs[0:ng]
    kc_refs = refs[ng:2 * ng]
    vc_refs = refs[2 * ng:3 * ng]
    kp_refs = refs[3 * ng:4 * ng]
    vp_refs = refs[4 * ng:5 * ng]
    o_ref = refs[5 * ng]
    acc_ref, m_ref, l_ref = refs[5 * ng + 1:]
    s = pl.program_id(1)
    h = pl.program_id(2)
    prev_bias = jnp.where(s > 0, 0.0, NEG).astype(F32)

    for g, (window, r) in enumerate(ATT_PATTERNS):
        assert window // r == ATT_BLOCK
        nblk = ATT_SUPER // (ATT_BLOCK * r)
        slope_r = slopes_ref[g, h] * float(r)
        common = dict(r=r, slope_r=slope_r, g=g, acc_ref=acc_ref, m_ref=m_ref, l_ref=l_ref,
                      q_ref=q_refs[g], kc_ref=kc_refs[g], vc_ref=vc_refs[g])

        def first(ps, common=common, g=g):
            _attn_blocks(prev_bias=prev_bias, kp_ref=kp_refs[g], vp_ref=vp_refs[g],
                         bases=[(p, p) for p in ps], **common)
        _batched_loop(r, first)

        if nblk > 1:
            def rest(idxs, common=common, g=g, r=r, nblk=nblk):
                bases = []
                for idx in idxs:
                    p = idx // (nblk - 1)
                    j = idx % (nblk - 1) + 1
                    base = p + j * (ATT_BLOCK * r)
                    bases.append((base, base - ATT_BLOCK * r))
                _attn_blocks(prev_bias=jnp.float32(0.0), kp_ref=kc_refs[g], vp_ref=vc_refs[g],
                             bases=bases, **common)
            _batched_loop(r * (nblk - 1), rest)

    ch = 256
    def merge(i, c):
        rs = pl.ds(pl.multiple_of(i * ch, ch), ch)
        ms = [m_ref[g, rs, :] for g in range(ng)]
        mx = functools.reduce(jnp.maximum, ms)
        num = jnp.zeros((ch, ATT_HEAD_DIM), F32)
        den = jnp.zeros((ch, ATT_HEAD_DIM), F32)
        for g in range(ng):
            w = jnp.exp(ms[g] - mx)
            num = num + w * acc_ref[g, rs, :]
            den = den + w * l_ref[g, rs, :]
        o_ref[0, rs, :] = (num / den).astype(o_ref.dtype)
        return c
    lax.fori_loop(0, ATT_SUPER // ch, merge, 0)


def _attention(proj_a, B, S):
    ng = len(ATT_PATTERNS)
    nsb = S // ATT_SUPER
    cb = ATT_HEAD_DIM
    slopes = jnp.asarray(_alibi_slopes(ATT_HEADS).reshape(ng, ATT_HEADS_PER_GROUP))

    def cur_spec(col0, g):
        return pl.BlockSpec((1, ATT_SUPER, cb),
                            lambda b, s, h, g=g, col0=col0: (b, s, col0 // cb + g * ATT_HEADS_PER_GROUP + h))

    def prev_spec(col0, g):
        rows = ATT_BLOCK * ATT_PATTERNS[g][1]
        per = ATT_SUPER // rows
        return pl.BlockSpec((1, rows, cb),
                            lambda b, s, h, g=g, col0=col0, per=per: (
                                b, jnp.maximum(s * per - 1, 0), col0 // cb + g * ATT_HEADS_PER_GROUP + h))

    in_specs = [pl.BlockSpec(memory_space=pltpu.SMEM)]
    in_specs += [cur_spec(COL_AQ, g) for g in range(ng)]
    in_specs += [cur_spec(COL_AK, g) for g in range(ng)]
    in_specs += [cur_spec(COL_AV, g) for g in range(ng)]
    in_specs += [prev_spec(COL_AK, g) for g in range(ng)]
    in_specs += [prev_spec(COL_AV, g) for g in range(ng)]
    return pl.pallas_call(
        _attn_kernel,
        grid=(B, nsb, ATT_HEADS_PER_GROUP),
        in_specs=in_specs,
        out_specs=pl.BlockSpec((1, ATT_SUPER, cb), lambda b, s, h: (b, s, h)),
        out_shape=jax.ShapeDtypeStruct((B, S, ATT_OUT_WIDTH), BF16),
        scratch_shapes=[pltpu.VMEM((ng, ATT_SUPER, cb), F32)] * 3,
        compiler_params=_params("parallel", "parallel", "parallel"),
        name="dilated_attention",
    )(slopes, *([proj_a] * (5 * ng)))


def _log_sigmoid(x):
    return jnp.minimum(x, 0.0) - jnp.log(1.0 + jnp.exp(-jnp.abs(x)))


def _mlstm_kernel(mq_ref, mk_ref, mva_ref, mvb_ref, moa_ref, mob_ref, mif_ref, mift_ref, ifb_ref, ifbt_ref,
                  cw_ref, cb_ref, nw_ref, o_ref, tail_ref, c_ref, n_ref, m_ref):
    L = M_CHUNK
    NB = mq_ref.shape[0]
    c = pl.program_id(0)
    mv_refs = (mva_ref, mvb_ref)
    mo_refs = (moa_ref, mob_ref)
    hpb = M_MLSTM_COLS // M_V_DIM
    scale = M_QK_DIM ** -0.5
    hp = lax.Precision.HIGHEST

    @pl.when(c == 0)
    def _():
        tail_ref[...] = jnp.zeros_like(tail_ref)
        c_ref[...] = jnp.zeros_like(c_ref)
        n_ref[...] = jnp.zeros_like(n_ref)
        m_ref[...] = jnp.zeros_like(m_ref)

    ti = lax.broadcasted_iota(jnp.int32, (L, L), 0)
    si = lax.broadcasted_iota(jnp.int32, (L, L), 1)
    causal = si <= ti
    tri = causal.astype(F32)

    def conv_act(x_ref, bb, part):
        cols = slice(part * M_QK_WIDTH, (part + 1) * M_QK_WIDTH)
        x = x_ref[bb]
        xx = jnp.concatenate([tail_ref[bb, :, cols], x], axis=0)
        y = cb_ref[:, cols]
        for j in range(M_CONV):
            off = 8 - (M_CONV - 1) + j
            y = y + cw_ref[j:j + 1, cols] * xx[off:off + L, :]
        tail_ref[bb, :, cols] = x[L - 8:, :]
        return (y * _sigmoid(y)).astype(BF16)

    per_b = []
    for bb in range(NB):
        q_act = conv_act(mq_ref, bb, 0)
        k_act = conv_act(mk_ref, bb, 1)
        gi_c = mif_ref[bb] + ifb_ref[...]
        gi_r = mift_ref[bb] + ifbt_ref[...]
        bcum_c = jnp.dot(tri, _log_sigmoid(gi_c), precision=hp, preferred_element_type=F32)
        bcum_r = lax.dot_general(_log_sigmoid(gi_r), tri, (((1,), (1,)), ((), ())), precision=hp,
                                 preferred_element_type=F32)
        per_b.append((q_act, k_act, gi_c, gi_r, bcum_c, bcum_r))

    chains = [(bb, hd) for bb in range(NB) for hd in range(M_HEADS)]

    ph1 = []
    for bb, hd in chains:
        q_act, k_act, gi_c, gi_r, bcum_c, bcum_r = per_b[bb]
        st = bb * M_HEADS + hd
        q = q_act[:, hd * M_QK_DIM:(hd + 1) * M_QK_DIM]
        k = k_act[:, hd * M_QK_DIM:(hd + 1) * M_QK_DIM]
        b_c = bcum_c[:, M_HEADS + hd:M_HEADS + hd + 1]
        i_c = gi_c[:, hd:hd + 1]
        b_r = bcum_r[M_HEADS + hd:M_HEADS + hd + 1, :]
        i_r = gi_r[hd:hd + 1, :]
        m_prev = m_ref[st]
        dmat = jnp.where(causal, b_c + (i_r - b_r), NEG)
        inter = b_c + m_prev
        m_t = jnp.maximum(inter, jnp.max(dmat, axis=-1, keepdims=True))
        w_intra = jnp.exp(dmat - m_t)
        w_inter = jnp.exp(inter - m_t)
        qk = lax.dot_general(q, k, (((1,), (1,)), ((), ())), preferred_element_type=F32) * scale * w_intra
        ph1.append((q, k, b_c, i_c, m_prev, m_t, w_inter, qk))

    ph2 = []
    for (bb, hd), (q, k, b_c, i_c, m_prev, m_t, w_inter, qk) in zip(chains, ph1):
        st = bb * M_HEADS + hd
        vcols = slice((hd % hpb) * M_V_DIM, (hd % hpb + 1) * M_V_DIM)
        v = mv_refs[hd // hpb][bb, :, vcols].astype(BF16)
        c_prev = c_ref[st]
        n_prev = n_ref[st]
        num = (w_inter * scale) * jnp.dot(q, c_prev.astype(BF16), preferred_element_type=F32) \
            + jnp.dot(qk.astype(BF16), v, preferred_element_type=F32)
        den = (w_inter * scale) * jnp.sum(q.astype(F32) * n_prev, axis=-1, keepdims=True) \
            + jnp.sum(qk, axis=-1, keepdims=True)
        hh = num / jnp.maximum(jnp.abs(den), jnp.exp(-m_t))
        ph2.append((v, c_prev, n_prev, hh))

    for (bb, hd), (q, k, b_c, i_c, m_prev, m_t, w_inter, qk), (v, c_prev, n_prev, hh) in zip(chains, ph1, ph2):
        st = bb * M_HEADS + hd
        b_last = b_c[L - 1:L, :]
        w_log = b_last - b_c + i_c
        m_new = jnp.maximum(b_last + m_prev, jnp.max(w_log, axis=0, keepdims=True))
        wk = jnp.exp(w_log - m_new)
        decay = jnp.exp(b_last + m_prev - m_new)
        kw = (k.astype(F32) * wk)
        c_ref[st] = decay * c_prev + lax.dot_general(kw.astype(BF16), v, (((0,), (0,)), ((), ())),
                                                     preferred_element_type=F32)
        n_ref[st] = decay * n_prev + jnp.sum(kw, axis=0, keepdims=True)
        m_ref[st] = m_new

    for (bb, hd), (v, c_prev, n_prev, hh) in zip(chains, ph2):
        vcols = slice((hd % hpb) * M_V_DIM, (hd % hpb + 1) * M_V_DIM)
        mu = jnp.mean(hh, axis=-1, keepdims=True)
        hc = hh - mu
        var = jnp.mean(hc * hc, axis=-1, keepdims=True)
        hn = hc * lax.rsqrt(var + LN_EPS) * nw_ref[:, hd * M_V_DIM:(hd + 1) * M_V_DIM]
        og = _sigmoid(mo_refs[hd // hpb][bb, :, vcols])
        o_ref[bb, :, hd * M_V_DIM:(hd + 1) * M_V_DIM] = (hn * og).astype(o_ref.dtype)


def _mlstm(proj_a, mif, mif_t, if_bias, conv_w, conv_b, norm_w, B, S):
    L = M_CHUNK
    W = 2 * M_QK_WIDTH
    cw = M_MLSTM_COLS
    ifb = jnp.zeros((1, ROUTE_LANES), F32).at[0, :2 * M_HEADS].set(if_bias)
    ifbt = jnp.broadcast_to(if_bias.reshape(2 * M_HEADS, 1), (2 * M_HEADS, L))

    def col_spec(col0):
        assert col0 % cw == 0
        return pl.BlockSpec((B, L, cw), lambda c, col0=col0: (0, c, col0 // cw))

    const = lambda c: (0, 0)
    return pl.pallas_call(
        _mlstm_kernel,
        grid=(S // L,),
        in_specs=[col_spec(COL_MQK), col_spec(COL_MQK + M_QK_WIDTH),
                  col_spec(COL_MV), col_spec(COL_MV + cw),
                  col_spec(COL_MO), col_spec(COL_MO + cw),
                  pl.BlockSpec((B, L, ROUTE_LANES), lambda c: (0, c, 0)),
                  pl.BlockSpec((B, 2 * M_HEADS, L), lambda c: (0, 0, c)),
                  pl.BlockSpec((1, ROUTE_LANES), const),
                  pl.BlockSpec((2 * M_HEADS, L), const),
                  pl.BlockSpec((M_CONV, W), const),
                  pl.BlockSpec((1, W), const),
                  pl.BlockSpec((1, M_V_WIDTH), const)],
        out_specs=pl.BlockSpec((B, L, M_V_WIDTH), lambda c: (0, c, 0)),
        out_shape=jax.ShapeDtypeStruct((B, S, M_V_WIDTH), BF16),
        scratch_shapes=[pltpu.VMEM((B, 8, W), F32),
                        pltpu.VMEM((B * M_HEADS, M_QK_DIM, M_V_DIM), F32),
                        pltpu.VMEM((B * M_HEADS, 1, M_QK_DIM), F32),
                        pltpu.VMEM((B * M_HEADS, 1, 1), F32)],
        compiler_params=_params("arbitrary"),
        name="mlstm",
    )(proj_a, proj_a, proj_a, proj_a, proj_a, proj_a, mif, mif_t, ifb, ifbt, conv_w, conv_b.reshape(1, W),
      norm_w.reshape(1, M_V_WIDTH))


def _merge_kernel(att_ref, hm_ref, gate_ref, x_ref, g0_ref, b0_ref, wpa_ref, wpm_ref, wo_ref, g_ref, b_ref,
                  wrh_ref, wrl_ref, br_ref, h1_ref, lg_ref):
    pa = jnp.dot(att_ref[...], wpa_ref[...], preferred_element_type=F32)
    pm = jnp.dot(hm_ref[...], wpm_ref[...], preferred_element_type=F32)
    ga = _sigmoid(gate_ref[:, :D_MODEL])
    gm = _sigmoid(gate_ref[:, D_MODEL:])
    merged = (ga * pa + gm * pm).astype(BF16)
    y = jnp.dot(merged, wo_ref[...], preferred_element_type=F32)
    h = _layer_norm_rows(x_ref[...], g0_ref[...], b0_ref[...])
    h1 = _layer_norm_rows(DEEPNORM_ALPHA * h + y, g_ref[...], b_ref[...])
    h1_ref[...] = h1
    h1h = h1.astype(BF16)
    h1l = (h1 - h1h.astype(F32)).astype(BF16)
    lg_ref[...] = (jnp.dot(h1h, wrh_ref[...], preferred_element_type=F32)
                   + jnp.dot(h1l, wrh_ref[...], preferred_element_type=F32)
                   + jnp.dot(h1h, wrl_ref[...], preferred_element_type=F32)) + br_ref[...]


def _merge(att, hm, gate, x2, g0, b0, wpa, wpm, wo, g1, b1, wr, br, tm=256):
    T, D = x2.shape
    const = lambda i: (0, 0)
    one = pl.Buffered(1)
    wrh = wr.astype(BF16)
    wrl = (wr - wrh.astype(F32)).astype(BF16)
    return pl.pallas_call(
        _merge_kernel,
        grid=(T // tm,),
        in_specs=[pl.BlockSpec((tm, ATT_OUT_WIDTH), lambda i: (i, 0)),
                  pl.BlockSpec((tm, M_V_WIDTH), lambda i: (i, 0)),
                  pl.BlockSpec((tm, N_BRANCHES * D), lambda i: (i, 0)),
                  pl.BlockSpec((tm, D), lambda i: (i, 0)),
                  pl.BlockSpec((1, D), const),
                  pl.BlockSpec((1, D), const),
                  pl.BlockSpec((ATT_OUT_WIDTH, D), const, pipeline_mode=one),
                  pl.BlockSpec((M_V_WIDTH, D), const, pipeline_mode=one),
                  pl.BlockSpec((D, D), const, pipeline_mode=one),
                  pl.BlockSpec((1, D), const),
                  pl.BlockSpec((1, D), const),
                  pl.BlockSpec((D, ROUTE_LANES), const, pipeline_mode=one),
                  pl.BlockSpec((D, ROUTE_LANES), const, pipeline_mode=one),
                  pl.BlockSpec((1, ROUTE_LANES), const)],
        out_specs=[pl.BlockSpec((tm, D), lambda i: (i, 0)),
                   pl.BlockSpec((tm, ROUTE_LANES), lambda i: (i, 0))],
        out_shape=[jax.ShapeDtypeStruct((T, D), F32), jax.ShapeDtypeStruct((T, ROUTE_LANES), F32)],
        compiler_params=_params("parallel"),
        name="merge_out_ln1",
    )(att, hm, gate, x2, g0.reshape(1, D), b0.reshape(1, D), wpa, wpm, wo, g1.reshape(1, D), b1.reshape(1, D),
      wrh, wrl, br)


def _route_kernel(lg_ref, e_ref, w_ref):
    lg = lg_ref[...]
    col = lax.broadcasted_iota(jnp.int32, lg.shape, 1)
    big = jnp.int32(ROUTE_LANES)

    def first_argmax(v, vmax):
        return jnp.min(jnp.where(v == vmax, col, big), axis=-1, keepdims=True)

    gl = jnp.where(col < N_GROUPS, lg, NEG)
    gmax = jnp.max(gl, axis=-1, keepdims=True)
    grp = first_argmax(gl, gmax)
    gsum = jnp.sum(jnp.where(col < N_GROUPS, jnp.exp(lg - gmax), 0.0), axis=-1, keepdims=True)
    g_w = 1.0 / gsum
    ecol = col - N_GROUPS
    egrp = lax.shift_right_arithmetic(ecol, int(math.log2(EXPERTS_PER_GROUP)))
    in_grp = (ecol >= 0) & (ecol < N_EXPERTS) & (egrp == grp)
    el = jnp.where(in_grp, lg, NEG)
    v1 = jnp.max(el, axis=-1, keepdims=True)
    i1 = first_argmax(el, v1)
    el2 = jnp.where(col == i1, NEG, el)
    v2 = jnp.max(el2, axis=-1, keepdims=True)
    i2 = first_argmax(el2, v2)
    t = jnp.exp(v2 - v1)
    p1 = 1.0 / (1.0 + t)
    p2 = t / (1.0 + t)
    e_ref[...] = jnp.where(col == 0, i1 - N_GROUPS, jnp.where(col == 1, i2 - N_GROUPS, 0))
    w_ref[...] = jnp.where(col == 0, g_w * p1, jnp.where(col == 1, g_w * p2, 0.0))


def _route(logits, tm=1024):
    T = logits.shape[0]
    spec = pl.BlockSpec((tm, ROUTE_LANES), lambda i: (i, 0))
    return pl.pallas_call(
        _route_kernel,
        grid=(T // tm,),
        in_specs=[spec],
        out_specs=[spec, spec],
        out_shape=[jax.ShapeDtypeStruct((T, ROUTE_LANES), jnp.int32),
                   jax.ShapeDtypeStruct((T, ROUTE_LANES), F32)],
        compiler_params=_params("parallel"),
        name="route",
    )(logits)


def _dispatch_plan(e_tk, T):
    M = T * TOP_K
    e_flat = e_tk.reshape(M)
    onehot = (e_flat[:, None] == jnp.arange(N_EXPERTS, dtype=jnp.int32)[None, :]).astype(jnp.int32)
    csum = jnp.cumsum(onehot, axis=0)
    counts = csum[-1]
    rank = jnp.sum((csum - onehot) * onehot, axis=1)
    padded = (counts + MOE_SUB - 1) // MOE_SUB * MOE_SUB
    pstart = jnp.cumsum(padded) - padded
    dest = jnp.sum(onehot * pstart[None, :], axis=1) + rank

    nsb_max = N_EXPERTS + M // MOE_SUPER
    nsb_e = (padded + MOE_SUPER - 1) // MOE_SUPER
    sb_end = jnp.cumsum(nsb_e)
    sb_beg = sb_end - nsb_e
    total = sb_end[-1]
    sb = jnp.arange(nsb_max, dtype=jnp.int32)
    sb_c = jnp.minimum(sb, total - 1)
    ex = jnp.sum((sb_end[None, :] <= sb_c[:, None]).astype(jnp.int32), axis=1)
    local = sb_c - sb_beg[ex]
    row0 = pstart[ex] + local * MOE_SUPER
    active = sb < total
    cnt = jnp.where(active, jnp.clip(counts[ex] - local * MOE_SUPER, 0, MOE_SUPER), 0)
    nsub = jnp.where(active, jnp.clip(padded[ex] - local * MOE_SUPER, 0, MOE_SUPER) // MOE_SUB, 0)
    return (ex.astype(jnp.int32), row0.astype(jnp.int32), cnt.astype(jnp.int32), nsub.astype(jnp.int32),
            dest.astype(jnp.int32))


def _moe_kernel(sb_ex, sb_row0, sb_cnt, sb_nsub, dest,
                h1_hbm, wga_ref, wua_ref, wda_ref, wgb_ref, wub_ref, wdb_ref, y2_hbm,
                stage_buf, xb_buf, acc_buf, slot_tok, slot_dst, gsem, ssem, *, T):
    b = pl.program_id(0)
    j = pl.program_id(1)
    nb = pl.num_programs(0)
    nsub = sb_nsub[b]
    cnt = sb_cnt[b]
    slot = lax.rem(b, 2)
    U = MOE_DMA_UNROLL

    def gather_batches(bb):
        return (sb_cnt[bb] + (U - 1)) // U

    def gather_issue(bb):
        r0 = sb_row0[bb]

        def pad_row(i, c):
            slot_tok[r0 + i] = 0
            return c
        lax.fori_loop(sb_cnt[bb], gather_batches(bb) * U, pad_row, 0)

        def issue(q, c):
            i0 = pl.multiple_of(q * U, U)
            for k in range(U):
                tok = slot_tok[r0 + i0 + k]
                pltpu.make_async_copy(h1_hbm.at[lax.shift_right_logical(tok, 3), pl.ds(tok & 7, 1), :],
                                      stage_buf.at[q * (U // 8) + k // 8, pl.ds(k % 8, 1), :], gsem).start()
            return c
        lax.fori_loop(0, gather_batches(bb), issue, 0)

    def gather_wait(bb):
        def wait(k, c):
            pltpu.make_async_copy(h1_hbm.at[pl.ds(0, U // 8)], stage_buf.at[pl.ds(0, U // 8)], gsem).wait()
            return c
        lax.fori_loop(0, gather_batches(bb), wait, 0)

    def build_tables():
        n_asg = dest.shape[0]
        step = 16

        def fill(q, c):
            for k in range(step):
                d = dest[q * step + k]
                tok = q * (step // TOP_K) + k // TOP_K
                slot_tok[d] = tok
                slot_dst[d] = (k % TOP_K) * T + tok
            return c
        lax.fori_loop(0, n_asg // step, fill, 0)

    def scatter_issue():
        r0 = sb_row0[b]
        nq = cnt // U

        def row_out(i):
            return pltpu.make_async_copy(acc_buf.at[slot, pl.ds(i, 1), :],
                                         y2_hbm.at[pl.ds(slot_dst[r0 + i], 1), :], ssem)

        def issue(q, c):
            i0 = pl.multiple_of(q * U, U)
            for k in range(U):
                row_out(i0 + k).start()
            return c
        lax.fori_loop(0, nq, issue, 0)

        def issue_one(i, c):
            row_out(i).start()
            return c
        lax.fori_loop(nq * U, cnt, issue_one, 0)

    def scatter_wait(n_rows):
        nq = n_rows // U

        def wait(q, c):
            pltpu.make_async_copy(acc_buf.at[0, pl.ds(0, U), :], y2_hbm.at[pl.ds(0, U), :], ssem).wait()
            return c
        lax.fori_loop(0, nq, wait, 0)

        def wait_one(i, c):
            pltpu.make_async_copy(acc_buf.at[0, pl.ds(0, 1), :], y2_hbm.at[pl.ds(0, 1), :], ssem).wait()
            return c
        lax.fori_loop(nq * U, n_rows, wait_one, 0)

    @pl.when(j == 0)
    def _first_step():
        @pl.when(b == 0)
        def _():
            build_tables()

            def clear(q, c):
                stage_buf[q] = jnp.zeros((8, D_MODEL), F32)
                return c
            lax.fori_loop(0, stage_buf.shape[0], clear, 0)
            gather_issue(0)

        @pl.when(nsub > 0)
        def _():
            gather_wait(b)

            def cast(k, c):
                rs = pl.ds(pl.multiple_of(k * MOE_SUB, MOE_SUB), MOE_SUB)
                tiles = pl.ds(pl.multiple_of(k * (MOE_SUB // 8), MOE_SUB // 8), MOE_SUB // 8)
                xb_buf[rs, :] = stage_buf[tiles].reshape(MOE_SUB, D_MODEL).astype(BF16)
                acc_buf[slot, rs, :] = jnp.zeros((MOE_SUB, D_MODEL), F32)
                return c
            lax.fori_loop(0, nsub, cast, 0)

        nxt = jnp.minimum(b + 1, nb - 1)

        @pl.when((b + 1 < nb) & (sb_nsub[nxt] > 0))
        def _():
            gather_issue(nxt)

    def ffn_tile(wg_ref, wu_ref, wd_ref):
        def chunk(r0, rows):
            rs = pl.ds(r0, rows)
            x = xb_buf[rs, :]
            gt = jnp.dot(x, wg_ref[0].astype(BF16), preferred_element_type=F32)
            ut = jnp.dot(x, wu_ref[0].astype(BF16), preferred_element_type=F32)
            hmid = (gt * _sigmoid(gt) * ut).astype(BF16)
            acc_buf[slot, rs, :] += jnp.dot(hmid, wd_ref[0].astype(BF16), preferred_element_type=F32)

        whole = [n for n in range(MOE_CHUNK // MOE_SUB + 1, MOE_WHOLE_MAX // MOE_SUB + 1)]
        is_whole = functools.reduce(jnp.logical_or, [nsub == n for n in whole])
        for n in whole:
            @pl.when(nsub == n)
            def _(n=n):
                chunk(0, n * MOE_SUB)

        @pl.when(jnp.logical_not(is_whole))
        def _():
            per = MOE_CHUNK // MOE_SUB
            nfull = nsub // per

            def full(k, c):
                chunk(pl.multiple_of(k * MOE_CHUNK, MOE_CHUNK), MOE_CHUNK)
                return c
            lax.fori_loop(0, nfull, full, 0)
            rem = nsub - nfull * per
            base = nfull * MOE_CHUNK
            size = MOE_CHUNK // 2
            while size >= MOE_SUB:
                units = size // MOE_SUB

                @pl.when(lax.rem(rem, 2 * units) >= units)
                def _(size=size, units=units):
                    skipped = (rem // (2 * units)) * (2 * units)
                    chunk(pl.multiple_of(base + skipped * MOE_SUB, size), size)
                size //= 2

    @pl.when(nsub > 0)
    def _tiles():
        ffn_tile(wga_ref, wua_ref, wda_ref)

        @pl.when(j == MOE_NJ - 1)
        def _():
            ffn_tile(wgb_ref, wub_ref, wdb_ref)

    @pl.when(j == MOE_NJ - 1)
    def _last_step():
        prev = jnp.maximum(b - 1, 0)

        @pl.when((b > 0) & (sb_nsub[prev] > 0))
        def _():
            scatter_wait(sb_cnt[prev])

        @pl.when(nsub > 0)
        def _():
            scatter_issue()

        @pl.when((b == nb - 1) & (nsub > 0))
        def _():
            scatter_wait(cnt)


def _moe_ffn(h1, plan, w_gate, w_up, w_down):
    T, D = h1.shape
    sb_ex, sb_row0, sb_cnt, sb_nsub, dest = plan
    nsb_max = sb_ex.shape[0]
    n_rows = dest.shape[0] + N_EXPERTS * MOE_SUB
    last = MOE_NFT - 1
    tail = D_FF_EXPERT // MOE_FT_TAIL - 1

    def ja(b, j, nsub):
        return jnp.where(nsub[b] > 0, jnp.minimum(j, last), last)

    def main_cols(b, j, ex, r0, ct, ns, ds):
        return (ex[b], 0, ja(b, j, ns))

    def main_rows(b, j, ex, r0, ct, ns, ds):
        return (ex[b], ja(b, j, ns), 0)

    def tail_cols(b, j, ex, r0, ct, ns, ds):
        return (ex[b], 0, tail)

    def tail_rows(b, j, ex, r0, ct, ns, ds):
        return (ex[b], tail, 0)

    grid_spec = pltpu.PrefetchScalarGridSpec(
        num_scalar_prefetch=5,
        grid=(nsb_max, MOE_NJ),
        in_specs=[pl.BlockSpec(memory_space=pl.ANY),
                  pl.BlockSpec((1, D, MOE_FT), main_cols),
                  pl.BlockSpec((1, D, MOE_FT), main_cols),
                  pl.BlockSpec((1, MOE_FT, D), main_rows),
                  pl.BlockSpec((1, D, MOE_FT_TAIL), tail_cols),
                  pl.BlockSpec((1, D, MOE_FT_TAIL), tail_cols),
                  pl.BlockSpec((1, MOE_FT_TAIL, D), tail_rows)],
        out_specs=pl.BlockSpec(memory_space=pl.ANY),
        scratch_shapes=[pltpu.VMEM((MOE_SUPER // 8, 8, D), F32),
                        pltpu.VMEM((MOE_SUPER, D), BF16),
                        pltpu.VMEM((2, MOE_SUPER, D), F32),
                        pltpu.SMEM((n_rows,), jnp.int32),
                        pltpu.SMEM((n_rows,), jnp.int32),
                        pltpu.SemaphoreType.DMA(()),
                        pltpu.SemaphoreType.DMA(())],
    )
    return pl.pallas_call(
        functools.partial(_moe_kernel, T=T),
        grid_spec=grid_spec,
        out_shape=jax.ShapeDtypeStruct((TOP_K * T, D), F32),
        compiler_params=_params("arbitrary", "arbitrary"),
        name="moe_experts",
    )(sb_ex, sb_row0, sb_cnt, sb_nsub, dest, h1.reshape(T // 8, 8, D),
      w_gate, w_up, w_down, w_gate, w_up, w_down)


def _ln_out_kernel(h1_ref, y0_ref, y1_ref, rw_ref, g_ref, b_ref, o_ref):
    rw = rw_ref[...]
    z = DEEPNORM_ALPHA * h1_ref[...] + rw[:, 0:1] * y0_ref[...] + rw[:, 1:2] * y1_ref[...]
    o_ref[...] = _layer_norm_rows(z, g_ref[...], b_ref[...])


def _ln_out(h1, y2, rw, g, b, tm=256):
    T, D = h1.shape
    nb = T // tm
    return pl.pallas_call(
        _ln_out_kernel,
        grid=(nb,),
        in_specs=[pl.BlockSpec((tm, D), lambda i: (i, 0)),
                  pl.BlockSpec((tm, D), lambda i: (i, 0)),
                  pl.BlockSpec((tm, D), lambda i: (i + nb, 0)),
                  pl.BlockSpec((tm, ROUTE_LANES), lambda i: (i, 0)),
                  pl.BlockSpec((1, D), lambda i: (0, 0)),
                  pl.BlockSpec((1, D), lambda i: (0, 0))],
        out_specs=pl.BlockSpec((tm, D), lambda i: (i, 0)),
        out_shape=jax.ShapeDtypeStruct((T, D), F32),
        compiler_params=_params("parallel"),
        name="combine_ln2",
    )(h1, y2, y2, rw, g.reshape(1, D), b.reshape(1, D))


def kernel(x, ln_in_g, ln_in_b, w_in, m_conv_w, m_conv_b, m_if_bias, m_norm_w, w_proj_att, w_proj_mlstm, w_out,
           ln1_g, ln1_b, w_router_group, b_router_group, w_router_expert, b_router_expert, w_gate, w_up, w_down,
           ln2_g, ln2_b):
    B, S, D = x.shape
    T = B * S
    assert D == D_MODEL and S % ATT_SUPER == 0 and w_in.shape[0] == DEPTH == 1

    x2 = x.reshape(T, D)
    hb = _ln_in(x2, ln_in_g, ln_in_b)
    for l in range(DEPTH):
        wt = jnp.swapaxes(w_in[l], 0, 1)
        proj_a = _matmul_nt(hb, wt, 0, PROJ_A_WIDTH, 1024, 768, F32, "in_proj_a")
        gate = _matmul_nt(hb, wt, COL_GATE, N_BRANCHES * D, 1024, 512, F32, "in_proj_gate")
        mif = _matmul_nt(hb, wt, COL_MIF, ROUTE_LANES, 1024, ROUTE_LANES, F32, "in_proj_if")

        proj_a3 = proj_a.reshape(B, S, PROJ_A_WIDTH)
        att = _attention(proj_a3, B, S)
        mif3 = mif.reshape(B, S, ROUTE_LANES)
        mif_t = jnp.swapaxes(mif3[:, :, :2 * M_HEADS], 1, 2)
        hm = _mlstm(proj_a3, mif3, mif_t, m_if_bias[l], m_conv_w[l], m_conv_b[l], m_norm_w[l], B, S)

        lane_pad = ROUTE_LANES - N_GROUPS - N_EXPERTS
        w_r = jnp.pad(jnp.concatenate([w_router_group[l], w_router_expert[l]], axis=1), ((0, 0), (0, lane_pad)))
        b_r = jnp.pad(jnp.concatenate([b_router_group[l], b_router_expert[l]]), (0, lane_pad)).reshape(1, ROUTE_LANES)
        h1, logits = _merge(att.reshape(T, ATT_OUT_WIDTH), hm.reshape(T, M_V_WIDTH), gate, x2, ln_in_g, ln_in_b,
                            w_proj_att[l].astype(BF16), w_proj_mlstm[l].astype(BF16), w_out[l].astype(BF16),
                            ln1_g[l], ln1_b[l], w_r, b_r)

        e_out, rw = _route(logits)
        plan = _dispatch_plan(e_out[:, :TOP_K], T)
        y2 = _moe_ffn(h1, plan, w_gate[l], w_up[l], w_down[l])
        h = _ln_out(h1, y2, rw, ln2_g[l], ln2_b[l])
    return h.reshape(B, S, D)
```

```python
import functools
import math

import numpy as np
import jax
import jax.numpy as jnp
from jax import lax
from jax.experimental import pallas as pl
from jax.experimental.pallas import tpu as pltpu

F32 = jnp.float32
BF16 = jnp.bfloat16

D_MODEL = 2048
ATT_HEAD_DIM = 128
ATT_HEADS_PER_GROUP = 4
ATT_PATTERNS = ((128, 1), (512, 4), (2048, 16))
ATT_HEADS = ATT_HEADS_PER_GROUP * len(ATT_PATTERNS)
ATT_WIDTH = ATT_HEADS * ATT_HEAD_DIM
ATT_OUT_WIDTH = ATT_HEADS_PER_GROUP * ATT_HEAD_DIM
ATT_BLOCK = 128
ATT_SUPER = 2048

M_HEADS = 4
M_QK_DIM = 128
M_V_DIM = 256
M_QK_WIDTH = M_HEADS * M_QK_DIM
M_V_WIDTH = M_HEADS * M_V_DIM
M_CONV = 4
M_CHUNK = 128
M_MLSTM_COLS = 512

N_BRANCHES = 2
IN_PROJ_SPLITS = (ATT_WIDTH, ATT_WIDTH, ATT_WIDTH, 2 * M_QK_WIDTH, M_V_WIDTH, M_V_WIDTH,
                  2 * M_HEADS, N_BRANCHES * D_MODEL)
COL_AQ = 0
COL_AK = ATT_WIDTH
COL_AV = 2 * ATT_WIDTH
COL_MQK = 3 * ATT_WIDTH
COL_MV = COL_MQK + 2 * M_QK_WIDTH
COL_MO = COL_MV + M_V_WIDTH
COL_MIF = COL_MO + M_V_WIDTH
COL_GATE = COL_MIF + 2 * M_HEADS
PROJ_A_WIDTH = COL_MIF

N_GROUPS = 4
EXPERTS_PER_GROUP = 8
N_EXPERTS = N_GROUPS * EXPERTS_PER_GROUP
TOP_K = 2
D_FF_EXPERT = 1408
MOE_SUB = 128
MOE_SUPER = 1024
MOE_CHUNK = 512
MOE_WHOLE_MAX = 768
MOE_FT = 256
MOE_NFT = D_FF_EXPERT // MOE_FT
MOE_FT_TAIL = D_FF_EXPERT - MOE_NFT * MOE_FT
MOE_NJ = MOE_NFT
MOE_DMA_UNROLL = 32
assert MOE_DMA_UNROLL % 8 == 0 and MOE_SUB % MOE_DMA_UNROLL == 0
assert MOE_FT_TAIL > 0 and D_FF_EXPERT % MOE_FT_TAIL == 0 and MOE_FT_TAIL % 128 == 0
ROUTE_LANES = 128

DEPTH = 1
DEEPNORM_ALPHA = (2 * DEPTH) ** 0.25
LN_EPS = 1e-5
NEG = -1e30

VMEM_LIMIT = 56 * 1024 * 1024


def _alibi_slopes(n):
    def geometric(k):
        start = 2.0 ** (-8.0 / k)
        return [start ** (i + 1) for i in range(k)]
    c = 2 ** int(math.floor(math.log2(n)))
    s = geometric(c) if c == n else geometric(c) + geometric(2 * c)[0::2][: n - c]
    return np.array(sorted(s, reverse=True), dtype=np.float32)


def _params(*sem):
    return pltpu.CompilerParams(dimension_semantics=sem, vmem_limit_bytes=VMEM_LIMIT)


def _layer_norm_rows(z, g, b):
    mu = jnp.mean(z, axis=-1, keepdims=True)
    zc = z - mu
    var = jnp.mean(zc * zc, axis=-1, keepdims=True)
    return zc * lax.rsqrt(var + LN_EPS) * g + b


def _sigmoid(x):
    return 1.0 / (1.0 + jnp.exp(-x))


def _ln_in_kernel(x_ref, g_ref, b_ref, hb_ref):
    hb_ref[...] = _layer_norm_rows(x_ref[...], g_ref[...], b_ref[...]).astype(BF16)


def _ln_in(x2, g, b, tm=512):
    T, D = x2.shape
    return pl.pallas_call(
        _ln_in_kernel,
        grid=(T // tm,),
        in_specs=[pl.BlockSpec((tm, D), lambda i: (i, 0)),
                  pl.BlockSpec((1, D), lambda i: (0, 0)),
                  pl.BlockSpec((1, D), lambda i: (0, 0))],
        out_specs=pl.BlockSpec((tm, D), lambda i: (i, 0)),
        out_shape=jax.ShapeDtypeStruct((T, D), BF16),
        compiler_params=_params("parallel"),
        name="ln_in",
    )(x2, g.reshape(1, D), b.reshape(1, D))


def _mm_nt_kernel(a_ref, w_ref, o_ref, wb_ref):
    @pl.when(pl.program_id(1) == 0)
    def _():
        wb_ref[...] = w_ref[...].astype(BF16)

    o_ref[...] = lax.dot_general(a_ref[...], wb_ref[...], (((1,), (1,)), ((), ())),
                                 preferred_element_type=F32).astype(o_ref.dtype)


def _matmul_nt(a, wt, row0, n_cols, tm, tn, out_dtype, name):
    T, K = a.shape
    if row0 % tn == 0:
        w_spec = pl.BlockSpec((tn, K), lambda j, i: (j + row0 // tn, 0))
    else:
        assert row0 % 8 == 0 and tn % 8 == 0
        w_spec = pl.BlockSpec((pl.Element(tn), pl.Element(K)),
                              lambda j, i: ((row0 // 8 + j * (tn // 8)) * 8, 0))
    return pl.pallas_call(
        _mm_nt_kernel,
        grid=(n_cols // tn, T // tm),
        in_specs=[pl.BlockSpec((tm, K), lambda j, i: (i, 0)), w_spec],
        out_specs=pl.BlockSpec((tm, tn), lambda j, i: (i, j)),
        out_shape=jax.ShapeDtypeStruct((T, n_cols), out_dtype),
        scratch_shapes=[pltpu.VMEM((tn, K), BF16)],
        compiler_params=_params("parallel", "arbitrary"),
        name=name,
    )(a, wt)


ATT_UNROLL = 8


def _batched_loop(n, body):
    u = max(d for d in range(1, ATT_UNROLL + 1) if n % d == 0)
    if n == u:
        body(list(range(n)))
        return

    def step(i, c):
        body([i * u + k for k in range(u)])
        return c
    lax.fori_loop(0, n // u, step, 0)


def _attn_blocks(r, slope_r, prev_bias, q_ref, kc_ref, vc_ref, kp_ref, vp_ref, bases, g, acc_ref, m_ref, l_ref):
    def rows(start):
        return pl.ds(start, ATT_BLOCK, r) if r > 1 else pl.ds(start, ATT_BLOCK)

    dn = (((1,), (1,)), ((), ()))
    scale = ATT_HEAD_DIM ** -0.5
    qi = lax.broadcasted_iota(jnp.int32, (ATT_BLOCK, ATT_BLOCK), 0)
    ki = lax.broadcasted_iota(jnp.int32, (ATT_BLOCK, ATT_BLOCK), 1)
    dlt = (qi - ki).astype(F32)
    alibi_c = -slope_r * dlt
    alibi_p = -slope_r * (dlt + float(ATT_BLOCK)) + prev_bias

    scores = []
    for base, base_prev in bases:
        q = q_ref[0, rows(base), :].astype(BF16)
        kc = kc_ref[0, rows(base), :].astype(BF16)
        kp = kp_ref[0, rows(base_prev), :].astype(BF16)
        sc = lax.dot_general(q, kc, dn, preferred_element_type=F32) * scale + alibi_c
        sp = lax.dot_general(q, kp, dn, preferred_element_type=F32) * scale + alibi_p
        scores.append((jnp.where(ki <= qi, sc, NEG), jnp.where(ki >= qi, sp, NEG)))
    probs = []
    for sc, sp in scores:
        m = jnp.max(jnp.maximum(sc, sp), axis=-1, keepdims=True)
        pc = jnp.exp(sc - m)
        pp = jnp.exp(sp - m)
        l = jnp.sum(pc + pp, axis=-1, keepdims=True)
        probs.append((m, l, pc.astype(BF16), pp.astype(BF16)))
    outs = []
    for (base, base_prev), (m, l, pc, pp) in zip(bases, probs):
        vc = vc_ref[0, rows(base), :].astype(BF16)
        vp = vp_ref[0, rows(base_prev), :].astype(BF16)
        outs.append(jnp.dot(pc, vc, preferred_element_type=F32) + jnp.dot(pp, vp, preferred_element_type=F32))
    for (base, _), (m, l, _, _), acc in zip(bases, probs, outs):
        acc_ref[g, rows(base), :] = acc
        m_ref[g, rows(base), :] = jnp.broadcast_to(m, (ATT_BLOCK, ATT_HEAD_DIM))
        l_ref[g, rows(base), :] = jnp.broadcast_to(l, (ATT_BLOCK, ATT_HEAD_DIM))


def _attn_kernel(slopes_ref, *refs):
    ng = len(ATT_PATTERNS)
    q_refs = refs[0:ng]
    kc_refs = refs[ng:2 * ng]
    vc_refs = refs[2 * ng:3 * ng]
    kp_refs = refs[3 * ng:4 * ng]
    vp_refs = refs[4 * ng:5 * ng]
    o_ref = refs[5 * ng]
    acc_ref, m_ref, l_ref = refs[5 * ng + 1:]
    s = pl.program_id(1)
    h = pl.program_id(2)
    prev_bias = jnp.where(s > 0, 0.0, NEG).astype(F32)

    for g, (window, r) in enumerate(ATT_PATTERNS):
        assert window // r == ATT_BLOCK
        nblk = ATT_SUPER // (ATT_BLOCK * r)
        slope_r = slopes_ref[g, h] * float(r)
        common = dict(r=r, slope_r=slope_r, g=g, acc_ref=acc_ref, m_ref=m_ref, l_ref=l_ref,
                      q_ref=q_refs[g], kc_ref=kc_refs[g], vc_ref=vc_refs[g])

        def first(ps, common=common, g=g):
            _attn_blocks(prev_bias=prev_bias, kp_ref=kp_refs[g], vp_ref=vp_refs[g],
                         bases=[(p, p) for p in ps], **common)
        _batched_loop(r, first)

        if nblk > 1:
            def rest(idxs, common=common, g=g, r=r, nblk=nblk):
                bases = []
                for idx in idxs:
                    p = idx // (nblk - 1)
                    j = idx % (nblk - 1) + 1
                    base = p + j * (ATT_BLOCK * r)
                    bases.append((base, base - ATT_BLOCK * r))
                _attn_blocks(prev_bias=jnp.float32(0.0), kp_ref=kc_refs[g], vp_ref=vc_refs[g],
                             bases=bases, **common)
            _batched_loop(r * (nblk - 1), rest)

    ch = 256
    def merge(i, c):
        rs = pl.ds(pl.multiple_of(i * ch, ch), ch)
        ms = [m_ref[g, rs, :] for g in range(ng)]
        mx = functools.reduce(jnp.maximum, ms)
        num = jnp.zeros((ch, ATT_HEAD_DIM), F32)
        den = jnp.zeros((ch, ATT_HEAD_DIM), F32)
        for g in range(ng):
            w = jnp.exp(ms[g] - mx)
            num = num + w * acc_ref[g, rs, :]
            den = den + w * l_ref[g, rs, :]
        o_ref[0, rs, :] = (num / den).astype(o_ref.dtype)
        return c
    lax.fori_loop(0, ATT_SUPER // ch, merge, 0)


def _attention(proj_a, B, S):
    ng = len(ATT_PATTERNS)
    nsb = S // ATT_SUPER
    cb = ATT_HEAD_DIM
    slopes = jnp.asarray(_alibi_slopes(ATT_HEADS).reshape(ng, ATT_HEADS_PER_GROUP))

    def cur_spec(col0, g):
        return pl.BlockSpec((1, ATT_SUPER, cb),
                            lambda b, s, h, g=g, col0=col0: (b, s, col0 // cb + g * ATT_HEADS_PER_GROUP + h))

    def prev_spec(col0, g):
        rows = ATT_BLOCK * ATT_PATTERNS[g][1]
        per = ATT_SUPER // rows
        return pl.BlockSpec((1, rows, cb),
                            lambda b, s, h, g=g, col0=col0, per=per: (
                                b, jnp.maximum(s * per - 1, 0), col0 // cb + g * ATT_HEADS_PER_GROUP + h))

    in_specs = [pl.BlockSpec(memory_space=pltpu.SMEM)]
    in_specs += [cur_spec(COL_AQ, g) for g in range(ng)]
    in_specs += [cur_spec(COL_AK, g) for g in range(ng)]
    in_specs += [cur_spec(COL_AV, g) for g in range(ng)]
    in_specs += [prev_spec(COL_AK, g) for g in range(ng)]
    in_specs += [prev_spec(COL_AV, g) for g in range(ng)]
    return pl.pallas_call(
        _attn_kernel,
        grid=(B, nsb, ATT_HEADS_PER_GROUP),
        in_specs=in_specs,
        out_specs=pl.BlockSpec((1, ATT_SUPER, cb), lambda b, s, h: (b, s, h)),
        out_shape=jax.ShapeDtypeStruct((B, S, ATT_OUT_WIDTH), BF16),
        scratch_shapes=[pltpu.VMEM((ng, ATT_SUPER, cb), F32)] * 3,
        compiler_params=_params("parallel", "parallel", "parallel"),
        name="dilated_attention",
    )(slopes, *([proj_a] * (5 * ng)))


def _log_sigmoid(x):
    return jnp.minimum(x, 0.0) - jnp.log(1.0 + jnp.exp(-jnp.abs(x)))


def _mlstm_kernel(mq_ref, mk_ref, mva_ref, mvb_ref, moa_ref, mob_ref, mif_ref, mift_ref, ifb_ref, ifbt_ref,
                  cw_ref, cb_ref, nw_ref, o_ref, tail_ref, c_ref, n_ref, m_ref):
    L = M_CHUNK
    NB = mq_ref.shape[0]
    c = pl.program_id(0)
    mv_refs = (mva_ref, mvb_ref)
    mo_refs = (moa_ref, mob_ref)
    hpb = M_MLSTM_COLS // M_V_DIM
    scale = M_QK_DIM ** -0.5
    hp = lax.Precision.HIGHEST

    @pl.when(c == 0)
    def _():
        tail_ref[...] = jnp.zeros_like(tail_ref)
        c_ref[...] = jnp.zeros_like(c_ref)
        n_ref[...] = jnp.zeros_like(n_ref)
        m_ref[...] = jnp.zeros_like(m_ref)

    ti = lax.broadcasted_iota(jnp.int32, (L, L), 0)
    si = lax.broadcasted_iota(jnp.int32, (L, L), 1)
    causal = si <= ti
    tri = causal.astype(F32)

    def conv_act(x_ref, bb, part):
        cols = slice(part * M_QK_WIDTH, (part + 1) * M_QK_WIDTH)
        x = x_ref[bb]
        xx = jnp.concatenate([tail_ref[bb, :, cols], x], axis=0)
        y = cb_ref[:, cols]
        for j in range(M_CONV):
            off = 8 - (M_CONV - 1) + j
            y = y + cw_ref[j:j + 1, cols] * xx[off:off + L, :]
        tail_ref[bb, :, cols] = x[L - 8:, :]
        return (y * _sigmoid(y)).astype(BF16)

    per_b = []
    for bb in range(NB):
        q_act = conv_act(mq_ref, bb, 0)
        k_act = conv_act(mk_ref, bb, 1)
        gi_c = mif_ref[bb] + ifb_ref[...]
        gi_r = mift_ref[bb] + ifbt_ref[...]
        bcum_c = jnp.dot(tri, _log_sigmoid(gi_c), precision=hp, preferred_element_type=F32)
        bcum_r = lax.dot_general(_log_sigmoid(gi_r), tri, (((1,), (1,)), ((), ())), precision=hp,
                                 preferred_element_type=F32)
        per_b.append((q_act, k_act, gi_c, gi_r, bcum_c, bcum_r))

    chains = [(bb, hd) for bb in range(NB) for hd in range(M_HEADS)]

    ph1 = []
    for bb, hd in chains:
        q_act, k_act, gi_c, gi_r, bcum_c, bcum_r = per_b[bb]
        st = bb * M_HEADS + hd
        q = q_act[:, hd * M_QK_DIM:(hd + 1) * M_QK_DIM]
        k = k_act[:, hd * M_QK_DIM:(hd + 1) * M_QK_DIM]
        b_c = bcum_c[:, M_HEADS + hd:M_HEADS + hd + 1]
        i_c = gi_c[:, hd:hd + 1]
        b_r = bcum_r[M_HEADS + hd:M_HEADS + hd + 1, :]
        i_r = gi_r[hd:hd + 1, :]
        m_prev = m_ref[st]
        dmat = jnp.where(causal, b_c + (i_r - b_r), NEG)
        inter = b_c + m_prev
        m_t = jnp.maximum(inter, jnp.max(dmat, axis=-1, keepdims=True))
        w_intra = jnp.exp(dmat - m_t)
        w_inter = jnp.exp(inter - m_t)
        qk = lax.dot_general(q, k, (((1,), (1,)), ((), ())), preferred_element_type=F32) * scale * w_intra
        ph1.append((q, k, b_c, i_c, m_prev, m_t, w_inter, qk))

    ph2 = []
    for (bb, hd), (q, k, b_c, i_c, m_prev, m_t, w_inter, qk) in zip(chains, ph1):
        st = bb * M_HEADS + hd
        vcols = slice((hd % hpb) * M_V_DIM, (hd % hpb + 1) * M_V_DIM)
        v = mv_refs[hd // hpb][bb, :, vcols].astype(BF16)
        c_prev = c_ref[st]
        n_prev = n_ref[st]
        num = (w_inter * scale) * jnp.dot(q, c_prev.astype(BF16), preferred_element_type=F32) \
            + jnp.dot(qk.astype(BF16), v, preferred_element_type=F32)
        den = (w_inter * scale) * jnp.sum(q.astype(F32) * n_prev, axis=-1, keepdims=True) \
            + jnp.sum(qk, axis=-1, keepdims=True)
        hh = num / jnp.maximum(jnp.abs(den), jnp.exp(-m_t))
        ph2.append((v, c_prev, n_prev, hh))

    for (bb, hd), (q, k, b_c, i_c, m_prev, m_t, w_inter, qk), (v, c_prev, n_prev, hh) in zip(chains, ph1, ph2):
        st = bb * M_HEADS + hd
        b_last = b_c[L - 1:L, :]
        w_log = b_last - b_c + i_c
        m_new = jnp.maximum(b_last + m_prev, jnp.max(w_log, axis=0, keepdims=True))
        wk = jnp.exp(w_log - m_new)
        decay = jnp.exp(b_last + m_prev - m_new)
        kw = (k.astype(F32) * wk)
        c_ref[st] = decay * c_prev + lax.dot_general(kw.astype(BF16), v, (((0,), (0,)), ((), ())),
                                                     preferred_element_type=F32)
        n_ref[st] = decay * n_prev + jnp.sum(kw, axis=0, keepdims=True)
        m_ref[st] = m_new

    for (bb, hd), (v, c_prev, n_prev, hh) in zip(chains, ph2):
        vcols = slice((hd % hpb) * M_V_DIM, (hd % hpb + 1) * M_V_DIM)
        mu = jnp.mean(hh, axis=-1, keepdims=True)
        hc = hh - mu
        var = jnp.mean(hc * hc, axis=-1, keepdims=True)
        hn = hc * lax.rsqrt(var + LN_EPS) * nw_ref[:, hd * M_V_DIM:(hd + 1) * M_V_DIM]
        og = _sigmoid(mo_refs[hd // hpb][bb, :, vcols])
        o_ref[bb, :, hd * M_V_DIM:(hd + 1) * M_V_DIM] = (hn * og).astype(o_ref.dtype)


def _mlstm(proj_a, mif, mif_t, if_bias, conv_w, conv_b, norm_w, B, S):
    L = M_CHUNK
    W = 2 * M_QK_WIDTH
    cw = M_MLSTM_COLS
    ifb = jnp.zeros((1, ROUTE_LANES), F32).at[0, :2 * M_HEADS].set(if_bias)
    ifbt = jnp.broadcast_to(if_bias.reshape(2 * M_HEADS, 1), (2 * M_HEADS, L))

    def col_spec(col0):
        assert col0 % cw == 0
        return pl.BlockSpec((B, L, cw), lambda c, col0=col0: (0, c, col0 // cw))

    const = lambda c: (0, 0)
    return pl.pallas_call(
        _mlstm_kernel,
        grid=(S // L,),
        in_specs=[col_spec(COL_MQK), col_spec(COL_MQK + M_QK_WIDTH),
                  col_spec(COL_MV), col_spec(COL_MV + cw),
                  col_spec(COL_MO), col_spec(COL_MO + cw),
                  pl.BlockSpec((B, L, ROUTE_LANES), lambda c: (0, c, 0)),
                  pl.BlockSpec((B, 2 * M_HEADS, L), lambda c: (0, 0, c)),
                  pl.BlockSpec((1, ROUTE_LANES), const),
                  pl.BlockSpec((2 * M_HEADS, L), const),
                  pl.BlockSpec((M_CONV, W), const),
                  pl.BlockSpec((1, W), const),
                  pl.BlockSpec((1, M_V_WIDTH), const)],
        out_specs=pl.BlockSpec((B, L, M_V_WIDTH), lambda c: (0, c, 0)),
        out_shape=jax.ShapeDtypeStruct((B, S, M_V_WIDTH), BF16),
        scratch_shapes=[pltpu.VMEM((B, 8, W), F32),
                        pltpu.VMEM((B * M_HEADS, M_QK_DIM, M_V_DIM), F32),
                        pltpu.VMEM((B * M_HEADS, 1, M_QK_DIM), F32),
                        pltpu.VMEM((B * M_HEADS, 1, 1), F32)],
        compiler_params=_params("arbitrary"),
        name="mlstm",
    )(proj_a, proj_a, proj_a, proj_a, proj_a, proj_a, mif, mif_t, ifb, ifbt, conv_w, conv_b.reshape(1, W),
      norm_w.reshape(1, M_V_WIDTH))


def _merge_kernel(att_ref, hm_ref, gate_ref, x_ref, g0_ref, b0_ref, wpa_ref, wpm_ref, wo_ref, g_ref, b_ref,
                  wrh_ref, wrl_ref, br_ref, h1_ref, lg_ref):
    pa = jnp.dot(att_ref[...], wpa_ref[...], preferred_element_type=F32)
    pm = jnp.dot(hm_ref[...], wpm_ref[...], preferred_element_type=F32)
    ga = _sigmoid(gate_ref[:, :D_MODEL])
    gm = _sigmoid(gate_ref[:, D_MODEL:])
    merged = (ga * pa + gm * pm).astype(BF16)
    y = jnp.dot(merged, wo_ref[...], preferred_element_type=F32)
    h = _layer_norm_rows(x_ref[...], g0_ref[...], b0_ref[...])
    h1 = _layer_norm_rows(DEEPNORM_ALPHA * h + y, g_ref[...], b_ref[...])
    h1_ref[...] = h1
    h1h = h1.astype(BF16)
    h1l = (h1 - h1h.astype(F32)).astype(BF16)
    lg_ref[...] = (jnp.dot(h1h, wrh_ref[...], preferred_element_type=F32)
                   + jnp.dot(h1l, wrh_ref[...], preferred_element_type=F32)
                   + jnp.dot(h1h, wrl_ref[...], preferred_element_type=F32)) + br_ref[...]


def _merge(att, hm, gate, x2, g0, b0, wpa, wpm, wo, g1, b1, wr, br, tm=256):
    T, D = x2.shape
    const = lambda i: (0, 0)
    one = pl.Buffered(1)
    wrh = wr.astype(BF16)
    wrl = (wr - wrh.astype(F32)).astype(BF16)
    return pl.pallas_call(
        _merge_kernel,
        grid=(T // tm,),
        in_specs=[pl.BlockSpec((tm, ATT_OUT_WIDTH), lambda i: (i, 0)),
                  pl.BlockSpec((tm, M_V_WIDTH), lambda i: (i, 0)),
                  pl.BlockSpec((tm, N_BRANCHES * D), lambda i: (i, 0)),
                  pl.BlockSpec((tm, D), lambda i: (i, 0)),
                  pl.BlockSpec((1, D), const),
                  pl.BlockSpec((1, D), const),
                  pl.BlockSpec((ATT_OUT_WIDTH, D), const, pipeline_mode=one),
                  pl.BlockSpec((M_V_WIDTH, D), const, pipeline_mode=one),
                  pl.BlockSpec((D, D), const, pipeline_mode=one),
                  pl.BlockSpec((1, D), const),
                  pl.BlockSpec((1, D), const),
                  pl.BlockSpec((D, ROUTE_LANES), const, pipeline_mode=one),
                  pl.BlockSpec((D, ROUTE_LANES), const, pipeline_mode=one),
                  pl.BlockSpec((1, ROUTE_LANES), const)],
        out_specs=[pl.BlockSpec((tm, D), lambda i: (i, 0)),
                   pl.BlockSpec((tm, ROUTE_LANES), lambda i: (i, 0))],
        out_shape=[jax.ShapeDtypeStruct((T, D), F32), jax.ShapeDtypeStruct((T, ROUTE_LANES), F32)],
        compiler_params=_params("parallel"),
        name="merge_out_ln1",
    )(att, hm, gate, x2, g0.reshape(1, D), b0.reshape(1, D), wpa, wpm, wo, g1.reshape(1, D), b1.reshape(1, D),
      wrh, wrl, br)


def _route_kernel(lg_ref, e_ref, w_ref):
    lg = lg_ref[...]
    col = lax.broadcasted_iota(jnp.int32, lg.shape, 1)
    big = jnp.int32(ROUTE_LANES)

    def first_argmax(v, vmax):
        return jnp.min(jnp.where(v == vmax, col, big), axis=-1, keepdims=True)

    gl = jnp.where(col < N_GROUPS, lg, NEG)
    gmax = jnp.max(gl, axis=-1, keepdims=True)
    grp = first_argmax(gl, gmax)
    gsum = jnp.sum(jnp.where(col < N_GROUPS, jnp.exp(lg - gmax), 0.0), axis=-1, keepdims=True)
    g_w = 1.0 / gsum
    ecol = col - N_GROUPS
    egrp = lax.shift_right_arithmetic(ecol, int(math.log2(EXPERTS_PER_GROUP)))
    in_grp = (ecol >= 0) & (ecol < N_EXPERTS) & (egrp == grp)
    el = jnp.where(in_grp, lg, NEG)
    v1 = jnp.max(el, axis=-1, keepdims=True)
    i1 = first_argmax(el, v1)
    el2 = jnp.where(col == i1, NEG, el)
    v2 = jnp.max(el2, axis=-1, keepdims=True)
    i2 = first_argmax(el2, v2)
    t = jnp.exp(v2 - v1)
    p1 = 1.0 / (1.0 + t)
    p2 = t / (1.0 + t)
    e_ref[...] = jnp.where(col == 0, i1 - N_GROUPS, jnp.where(col == 1, i2 - N_GROUPS, 0))
    w_ref[...] = jnp.where(col == 0, g_w * p1, jnp.where(col == 1, g_w * p2, 0.0))


def _route(logits, tm=1024):
    T = logits.shape[0]
    spec = pl.BlockSpec((tm, ROUTE_LANES), lambda i: (i, 0))
    return pl.pallas_call(
        _route_kernel,
        grid=(T // tm,),
        in_specs=[spec],
        out_specs=[spec, spec],
        out_shape=[jax.ShapeDtypeStruct((T, ROUTE_LANES), jnp.int32),
                   jax.ShapeDtypeStruct((T, ROUTE_LANES), F32)],
        compiler_params=_params("parallel"),
        name="route",
    )(logits)


def _dispatch_plan(e_tk, T):
    M = T * TOP_K
    e_flat = e_tk.reshape(M)
    onehot = (e_flat[:, None] == jnp.arange(N_EXPERTS, dtype=jnp.int32)[None, :]).astype(jnp.int32)
    csum = jnp.cumsum(onehot, axis=0)
    counts = csum[-1]
    rank = jnp.sum((csum - onehot) * onehot, axis=1)
    padded = (counts + MOE_SUB - 1) // MOE_SUB * MOE_SUB
    pstart = jnp.cumsum(padded) - padded
    dest = jnp.sum(onehot * pstart[None, :], axis=1) + rank

    nsb_max = N_EXPERTS + M // MOE_SUPER
    nsb_e = (padded + MOE_SUPER - 1) // MOE_SUPER
    sb_end = jnp.cumsum(nsb_e)
    sb_beg = sb_end - nsb_e
    total = sb_end[-1]
    sb = jnp.arange(nsb_max, dtype=jnp.int32)
    sb_c = jnp.minimum(sb, total - 1)
    ex = jnp.sum((sb_end[None, :] <= sb_c[:, None]).astype(jnp.int32), axis=1)
    local = sb_c - sb_beg[ex]
    row0 = pstart[ex] + local * MOE_SUPER
    active = sb < total
    cnt = jnp.where(active, jnp.clip(counts[ex] - local * MOE_SUPER, 0, MOE_SUPER), 0)
    nsub = jnp.where(active, jnp.clip(padded[ex] - local * MOE_SUPER, 0, MOE_SUPER) // MOE_SUB, 0)
    return (ex.astype(jnp.int32), row0.astype(jnp.int32), cnt.astype(jnp.int32), nsub.astype(jnp.int32),
            dest.astype(jnp.int32), jnp.sum(padded).astype(jnp.int32).reshape(1))


def _moe_kernel(sb_ex, sb_row0, sb_cnt, sb_nsub, dest, used_rows,
                h1_hbm, wga_ref, wua_ref, wda_ref, wgb_ref, wub_ref, wdb_ref, ys_hbm,
                stage_buf, xb_buf, acc_buf, slot_tok, gsem, ssem):
    b = pl.program_id(0)
    j = pl.program_id(1)
    nb = pl.num_programs(0)
    nsub = sb_nsub[b]
    cnt = sb_cnt[b]
    slot = lax.rem(b, 2)
    U = MOE_DMA_UNROLL

    def gather_batches(bb):
        return (sb_cnt[bb] + (U - 1)) // U

    def gather_issue(bb):
        r0 = sb_row0[bb]

        def pad_row(i, c):
            slot_tok[r0 + i] = 0
            return c
        lax.fori_loop(sb_cnt[bb], gather_batches(bb) * U, pad_row, 0)

        def issue(q, c):
            i0 = pl.multiple_of(q * U, U)
            for k in range(U):
                tok = slot_tok[r0 + i0 + k]
                pltpu.make_async_copy(h1_hbm.at[lax.shift_right_logical(tok, 3), pl.ds(tok & 7, 1), :],
                                      stage_buf.at[q * (U // 8) + k // 8, pl.ds(k % 8, 1), :], gsem).start()
            return c
        lax.fori_loop(0, gather_batches(bb), issue, 0)

    def gather_wait(bb):
        def wait(k, c):
            pltpu.make_async_copy(h1_hbm.at[pl.ds(0, U // 8)], stage_buf.at[pl.ds(0, U // 8)], gsem).wait()
            return c
        lax.fori_loop(0, gather_batches(bb), wait, 0)

    def build_tables():
        n_asg = dest.shape[0]
        step = 16

        def fill(q, c):
            for k in range(step):
                slot_tok[dest[q * step + k]] = q * (step // TOP_K) + k // TOP_K
            return c
        lax.fori_loop(0, n_asg // step, fill, 0)

    def out_copy(k, r0):
        rs = pl.ds(pl.multiple_of(k * MOE_SUB, MOE_SUB), MOE_SUB)
        return pltpu.make_async_copy(acc_buf.at[slot, rs, :],
                                     ys_hbm.at[pl.ds(pl.multiple_of(r0 + k * MOE_SUB, MOE_SUB), MOE_SUB), :], ssem)

    def out_issue():
        r0 = sb_row0[b]

        def issue(k, c):
            out_copy(k, r0).start()
            return c
        lax.fori_loop(0, nsub, issue, 0)

    def out_wait(n_sub):
        def wait(k, c):
            out_copy(0, 0).wait()
            return c
        lax.fori_loop(0, n_sub, wait, 0)

    @pl.when(j == 0)
    def _first_step():
        @pl.when(b == 0)
        def _():
            build_tables()

            def clear(q, c):
                stage_buf[q] = jnp.zeros((8, D_MODEL), F32)
                return c
            lax.fori_loop(0, stage_buf.shape[0], clear, 0)
            gather_issue(0)

        @pl.when(nsub > 0)
        def _():
            gather_wait(b)

            def cast(k, c):
                rs = pl.ds(pl.multiple_of(k * MOE_SUB, MOE_SUB), MOE_SUB)
                tiles = pl.ds(pl.multiple_of(k * (MOE_SUB // 8), MOE_SUB // 8), MOE_SUB // 8)
                xb_buf[rs, :] = stage_buf[tiles].reshape(MOE_SUB, D_MODEL).astype(BF16)
                acc_buf[slot, rs, :] = jnp.zeros((MOE_SUB, D_MODEL), F32)
                return c
            lax.fori_loop(0, nsub, cast, 0)

        nxt = jnp.minimum(b + 1, nb - 1)

        @pl.when((b + 1 < nb) & (sb_nsub[nxt] > 0))
        def _():
            gather_issue(nxt)

    def ffn_tile(wg_ref, wu_ref, wd_ref):
        def chunk(r0, rows):
            rs = pl.ds(r0, rows)
            x = xb_buf[rs, :]
            gt = jnp.dot(x, wg_ref[0].astype(BF16), preferred_element_type=F32)
            ut = jnp.dot(x, wu_ref[0].astype(BF16), preferred_element_type=F32)
            hmid = (gt * _sigmoid(gt) * ut).astype(BF16)
            acc_buf[slot, rs, :] += jnp.dot(hmid, wd_ref[0].astype(BF16), preferred_element_type=F32)

        whole = [n for n in range(MOE_CHUNK // MOE_SUB + 1, MOE_WHOLE_MAX // MOE_SUB + 1)]
        is_whole = functools.reduce(jnp.logical_or, [nsub == n for n in whole])
        for n in whole:
            @pl.when(nsub == n)
            def _(n=n):
                chunk(0, n * MOE_SUB)

        @pl.when(jnp.logical_not(is_whole))
        def _():
            per = MOE_CHUNK // MOE_SUB
            nfull = nsub // per

            def full(k, c):
                chunk(pl.multiple_of(k * MOE_CHUNK, MOE_CHUNK), MOE_CHUNK)
                return c
            lax.fori_loop(0, nfull, full, 0)
            rem = nsub - nfull * per
            base = nfull * MOE_CHUNK
            size = MOE_CHUNK // 2
            while size >= MOE_SUB:
                units = size // MOE_SUB

                @pl.when(lax.rem(rem, 2 * units) >= units)
                def _(size=size, units=units):
                    skipped = (rem // (2 * units)) * (2 * units)
                    chunk(pl.multiple_of(base + skipped * MOE_SUB, size), size)
                size //= 2

    @pl.when(nsub > 0)
    def _tiles():
        ffn_tile(wga_ref, wua_ref, wda_ref)

        @pl.when(j == MOE_NJ - 1)
        def _():
            ffn_tile(wgb_ref, wub_ref, wdb_ref)

    @pl.when(j == MOE_NJ - 1)
    def _last_step():
        prev = jnp.maximum(b - 1, 0)

        @pl.when((b > 0) & (sb_nsub[prev] > 0))
        def _():
            out_wait(sb_nsub[prev])

        @pl.when(nsub > 0)
        def _():
            out_issue()

        @pl.when((b == nb - 1) & (nsub > 0))
        def _():
            out_wait(nsub)

        @pl.when(b == nb - 1)
        def _():
            acc_buf[0, pl.ds(0, MOE_SUB), :] = jnp.zeros((MOE_SUB, D_MODEL), F32)

            def fill(k, c):
                rows = pl.ds(pl.multiple_of(k * MOE_SUB, MOE_SUB), MOE_SUB)
                pltpu.make_async_copy(acc_buf.at[0, pl.ds(0, MOE_SUB), :], ys_hbm.at[rows, :], ssem).start()
                return c
            first = used_rows[0] // MOE_SUB
            total = ys_hbm.shape[0] // MOE_SUB
            lax.fori_loop(first, total, fill, 0)

            def drain(k, c):
                out_copy(0, 0).wait()
                return c
            lax.fori_loop(first, total, drain, 0)


def _moe_ffn(h1, plan, w_gate, w_up, w_down):
    T, D = h1.shape
    sb_ex, sb_row0, sb_cnt, sb_nsub, dest, used_rows = plan
    nsb_max = sb_ex.shape[0]
    n_rows = dest.shape[0] + N_EXPERTS * MOE_SUB
    last = MOE_NFT - 1
    tail = D_FF_EXPERT // MOE_FT_TAIL - 1

    def ja(b, j, nsub):
        return jnp.where(nsub[b] > 0, jnp.minimum(j, last), last)

    def main_cols(b, j, ex, r0, ct, ns, ds, ur):
        return (ex[b], 0, ja(b, j, ns))

    def main_rows(b, j, ex, r0, ct, ns, ds, ur):
        return (ex[b], ja(b, j, ns), 0)

    def tail_cols(b, j, ex, r0, ct, ns, ds, ur):
        return (ex[b], 0, tail)

    def tail_rows(b, j, ex, r0, ct, ns, ds, ur):
        return (ex[b], tail, 0)

    grid_spec = pltpu.PrefetchScalarGridSpec(
        num_scalar_prefetch=6,
        grid=(nsb_max, MOE_NJ),
        in_specs=[pl.BlockSpec(memory_space=pl.ANY),
                  pl.BlockSpec((1, D, MOE_FT), main_cols),
                  pl.BlockSpec((1, D, MOE_FT), main_cols),
                  pl.BlockSpec((1, MOE_FT, D), main_rows),
                  pl.BlockSpec((1, D, MOE_FT_TAIL), tail_cols),
                  pl.BlockSpec((1, D, MOE_FT_TAIL), tail_cols),
                  pl.BlockSpec((1, MOE_FT_TAIL, D), tail_rows)],
        out_specs=pl.BlockSpec(memory_space=pl.ANY),
        scratch_shapes=[pltpu.VMEM((MOE_SUPER // 8, 8, D), F32),
                        pltpu.VMEM((MOE_SUPER, D), BF16),
                        pltpu.VMEM((2, MOE_SUPER, D), F32),
                        pltpu.SMEM((n_rows,), jnp.int32),
                        pltpu.SemaphoreType.DMA(()),
                        pltpu.SemaphoreType.DMA(())],
    )
    return pl.pallas_call(
        _moe_kernel,
        grid_spec=grid_spec,
        out_shape=jax.ShapeDtypeStruct((n_rows, D), F32),
        compiler_params=_params("arbitrary", "arbitrary"),
        name="moe_experts",
    )(sb_ex, sb_row0, sb_cnt, sb_nsub, dest, used_rows, h1.reshape(T // 8, 8, D),
      w_gate, w_up, w_down, w_gate, w_up, w_down)


LN_OUT_ROWS = 256
LN_OUT_BATCH = 32
assert LN_OUT_BATCH % (8 * TOP_K) == 0 and (LN_OUT_ROWS * TOP_K) % LN_OUT_BATCH == 0


def _ln_out_kernel(dest, h1_ref, rw_ref, g_ref, b_ref, ys_hbm, o_ref, y_buf, sem):
    i = pl.program_id(0)
    nt = pl.num_programs(0)
    tm = LN_OUT_ROWS
    toks = LN_OUT_BATCH // TOP_K

    def gather_issue(tile, buf):
        base = tile * (tm * TOP_K)

        def issue(q, c):
            for k in range(LN_OUT_BATCH):
                d = dest[base + q * LN_OUT_BATCH + k]
                t = k // TOP_K
                pltpu.make_async_copy(ys_hbm.at[lax.shift_right_logical(d, 3), pl.ds(d & 7, 1), :],
                                      y_buf.at[buf, k % TOP_K, q * (toks // 8) + t // 8, pl.ds(t % 8, 1), :],
                                      sem.at[buf]).start()
            return c
        lax.fori_loop(0, tm * TOP_K // LN_OUT_BATCH, issue, 0)

    def gather_wait(buf):
        for s in range(TOP_K):
            pltpu.make_async_copy(ys_hbm.at[pl.ds(0, tm // 8)], y_buf.at[buf, s], sem.at[buf]).wait()

    @pl.when(i == 0)
    def _():
        gather_issue(0, 0)

    @pl.when(i + 1 < nt)
    def _():
        gather_issue(i + 1, lax.rem(i + 1, 2))

    cur = lax.rem(i, 2)
    gather_wait(cur)
    rw = rw_ref[...]
    z = DEEPNORM_ALPHA * h1_ref[...]
    for s in range(TOP_K):
        z = z + rw[:, s:s + 1] * y_buf[cur, s].reshape(tm, D_MODEL)
    o_ref[...] = _layer_norm_rows(z, g_ref[...], b_ref[...])


def _ln_out(h1, ys, dest, rw, g, b):
    T, D = h1.shape
    tm = LN_OUT_ROWS
    n_rows = ys.shape[0]
    grid_spec = pltpu.PrefetchScalarGridSpec(
        num_scalar_prefetch=1,
        grid=(T // tm,),
        in_specs=[pl.BlockSpec((tm, D), lambda i, ds: (i, 0)),
                  pl.BlockSpec((tm, ROUTE_LANES), lambda i, ds: (i, 0)),
                  pl.BlockSpec((1, D), lambda i, ds: (0, 0)),
                  pl.BlockSpec((1, D), lambda i, ds: (0, 0)),
                  pl.BlockSpec(memory_space=pl.ANY)],
        out_specs=pl.BlockSpec((tm, D), lambda i, ds: (i, 0)),
        scratch_shapes=[pltpu.VMEM((2, TOP_K, tm // 8, 8, D), F32),
                        pltpu.SemaphoreType.DMA((2,))],
    )
    return pl.pallas_call(
        _ln_out_kernel,
        grid_spec=grid_spec,
        out_shape=jax.ShapeDtypeStruct((T, D), F32),
        compiler_params=_params("arbitrary"),
        name="combine_ln2",
    )(dest, h1, rw, g.reshape(1, D), b.reshape(1, D), ys.reshape(n_rows // 8, 8, D))


def kernel(x, ln_in_g, ln_in_b, w_in, m_conv_w, m_conv_b, m_if_bias, m_norm_w, w_proj_att, w_proj_mlstm, w_out,
           ln1_g, ln1_b, w_router_group, b_router_group, w_router_expert, b_router_expert, w_gate, w_up, w_down,
           ln2_g, ln2_b):
    B, S, D = x.shape
    T = B * S
    assert D == D_MODEL and S % ATT_SUPER == 0 and w_in.shape[0] == DEPTH == 1

    x2 = x.reshape(T, D)
    hb = _ln_in(x2, ln_in_g, ln_in_b)
    for l in range(DEPTH):
        wt = jnp.swapaxes(w_in[l], 0, 1)
        proj_a = _matmul_nt(hb, wt, 0, PROJ_A_WIDTH, 1024, 1280, F32, "in_proj_a")
        gate = _matmul_nt(hb, wt, COL_GATE, N_BRANCHES * D, 1024, 1024, F32, "in_proj_gate")
        mif = _matmul_nt(hb, wt, COL_MIF, ROUTE_LANES, 1024, ROUTE_LANES, F32, "in_proj_if")

        proj_a3 = proj_a.reshape(B, S, PROJ_A_WIDTH)
        att = _attention(proj_a3, B, S)
        mif3 = mif.reshape(B, S, ROUTE_LANES)
        mif_t = jnp.swapaxes(mif3[:, :, :2 * M_HEADS], 1, 2)
        hm = _mlstm(proj_a3, mif3, mif_t, m_if_bias[l], m_conv_w[l], m_conv_b[l], m_norm_w[l], B, S)

        lane_pad = ROUTE_LANES - N_GROUPS - N_EXPERTS
        w_r = jnp.pad(jnp.concatenate([w_router_group[l], w_router_expert[l]], axis=1), ((0, 0), (0, lane_pad)))
        b_r = jnp.pad(jnp.concatenate([b_router_group[l], b_router_expert[l]]), (0, lane_pad)).reshape(1, ROUTE_LANES)
        h1, logits = _merge(att.reshape(T, ATT_OUT_WIDTH), hm.reshape(T, M_V_WIDTH), gate, x2, ln_in_g, ln_in_b,
                            w_proj_att[l].astype(BF16), w_proj_mlstm[l].astype(BF16), w_out[l].astype(BF16),
                            ln1_g[l], ln1_b[l], w_r, b_r)

        e_out, rw = _route(logits)
        plan = _dispatch_plan(e_out[:, :TOP_K], T)
        ys = _moe_ffn(h1, plan, w_gate[l], w_up[l], w_down[l])
        h = _ln_out(h1, ys, plan[4], rw, ln2_g[l], ln2_b[l])
    return h.reshape(B, S, D)
```

```python
import functools
import math

import numpy as np
import jax
import jax.numpy as jnp
from jax import lax
from jax.experimental import pallas as pl
from jax.experimental.pallas import tpu as pltpu

F32 = jnp.float32
BF16 = jnp.bfloat16

D_MODEL = 2048
ATT_HEAD_DIM = 128
ATT_HEADS_PER_GROUP = 4
ATT_PATTERNS = ((128, 1), (512, 4), (2048, 16))
ATT_HEADS = ATT_HEADS_PER_GROUP * len(ATT_PATTERNS)
ATT_WIDTH = ATT_HEADS * ATT_HEAD_DIM
ATT_OUT_WIDTH = ATT_HEADS_PER_GROUP * ATT_HEAD_DIM
ATT_BLOCK = 128
ATT_SUPER = 2048

M_HEADS = 4
M_QK_DIM = 128
M_V_DIM = 256
M_QK_WIDTH = M_HEADS * M_QK_DIM
M_V_WIDTH = M_HEADS * M_V_DIM
M_CONV = 4
M_CHUNK = 128
M_MLSTM_COLS = 512

N_BRANCHES = 2
IN_PROJ_SPLITS = (ATT_WIDTH, ATT_WIDTH, ATT_WIDTH, 2 * M_QK_WIDTH, M_V_WIDTH, M_V_WIDTH,
                  2 * M_HEADS, N_BRANCHES * D_MODEL)
COL_AQ = 0
COL_AK = ATT_WIDTH
COL_AV = 2 * ATT_WIDTH
COL_MQK = 3 * ATT_WIDTH
COL_MV = COL_MQK + 2 * M_QK_WIDTH
COL_MO = COL_MV + M_V_WIDTH
COL_MIF = COL_MO + M_V_WIDTH
COL_GATE = COL_MIF + 2 * M_HEADS
PROJ_A_WIDTH = COL_MIF

N_GROUPS = 4
EXPERTS_PER_GROUP = 8
N_EXPERTS = N_GROUPS * EXPERTS_PER_GROUP
TOP_K = 2
D_FF_EXPERT = 1408
MOE_SUB = 128
MOE_SUPER = 1024
MOE_CHUNK = 512
MOE_WHOLE_MAX = 768
MOE_FT = 256
MOE_NFT = D_FF_EXPERT // MOE_FT
MOE_FT_TAIL = D_FF_EXPERT - MOE_NFT * MOE_FT
MOE_NJ = MOE_NFT
MOE_DMA_UNROLL = 32
assert MOE_DMA_UNROLL % 8 == 0 and MOE_SUB % MOE_DMA_UNROLL == 0
assert MOE_FT_TAIL > 0 and D_FF_EXPERT % MOE_FT_TAIL == 0 and MOE_FT_TAIL % 128 == 0
ROUTE_LANES = 128

DEPTH = 1
DEEPNORM_ALPHA = (2 * DEPTH) ** 0.25
LN_EPS = 1e-5
NEG = -1e30

VMEM_LIMIT = 56 * 1024 * 1024


def _alibi_slopes(n):
    def geometric(k):
        start = 2.0 ** (-8.0 / k)
        return [start ** (i + 1) for i in range(k)]
    c = 2 ** int(math.floor(math.log2(n)))
    s = geometric(c) if c == n else geometric(c) + geometric(2 * c)[0::2][: n - c]
    return np.array(sorted(s, reverse=True), dtype=np.float32)


def _params(*sem):
    return pltpu.CompilerParams(dimension_semantics=sem, vmem_limit_bytes=VMEM_LIMIT)


def _layer_norm_rows(z, g, b):
    mu = jnp.mean(z, axis=-1, keepdims=True)
    zc = z - mu
    var = jnp.mean(zc * zc, axis=-1, keepdims=True)
    return zc * lax.rsqrt(var + LN_EPS) * g + b


def _sigmoid(x):
    return 1.0 / (1.0 + jnp.exp(-x))


def _ln_in_kernel(x_ref, g_ref, b_ref, hb_ref):
    hb_ref[...] = _layer_norm_rows(x_ref[...], g_ref[...], b_ref[...]).astype(BF16)


def _ln_in(x2, g, b, tm=512):
    T, D = x2.shape
    return pl.pallas_call(
        _ln_in_kernel,
        grid=(T // tm,),
        in_specs=[pl.BlockSpec((tm, D), lambda i: (i, 0)),
                  pl.BlockSpec((1, D), lambda i: (0, 0)),
                  pl.BlockSpec((1, D), lambda i: (0, 0))],
        out_specs=pl.BlockSpec((tm, D), lambda i: (i, 0)),
        out_shape=jax.ShapeDtypeStruct((T, D), BF16),
        compiler_params=_params("parallel"),
        name="ln_in",
    )(x2, g.reshape(1, D), b.reshape(1, D))


def _mm_nt_kernel(a_ref, w_ref, o_ref, wb_ref):
    @pl.when(pl.program_id(1) == 0)
    def _():
        wb_ref[...] = w_ref[...].astype(BF16)

    o_ref[...] = lax.dot_general(a_ref[...], wb_ref[...], (((1,), (1,)), ((), ())),
                                 preferred_element_type=F32).astype(o_ref.dtype)


def _matmul_nt(a, wt, row0, n_cols, tm, tn, out_dtype, name):
    T, K = a.shape
    if row0 % tn == 0:
        w_spec = pl.BlockSpec((tn, K), lambda j, i: (j + row0 // tn, 0))
    else:
        assert row0 % 8 == 0 and tn % 8 == 0
        w_spec = pl.BlockSpec((pl.Element(tn), pl.Element(K)),
                              lambda j, i: ((row0 // 8 + j * (tn // 8)) * 8, 0))
    return pl.pallas_call(
        _mm_nt_kernel,
        grid=(n_cols // tn, T // tm),
        in_specs=[pl.BlockSpec((tm, K), lambda j, i: (i, 0)), w_spec],
        out_specs=pl.BlockSpec((tm, tn), lambda j, i: (i, j)),
        out_shape=jax.ShapeDtypeStruct((T, n_cols), out_dtype),
        scratch_shapes=[pltpu.VMEM((tn, K), BF16)],
        compiler_params=_params("parallel", "arbitrary"),
        name=name,
    )(a, wt)


ATT_UNROLL = 8


def _batched_loop(n, body):
    u = max(d for d in range(1, ATT_UNROLL + 1) if n % d == 0)
    if n == u:
        body(list(range(n)))
        return

    def step(i, c):
        body([i * u + k for k in range(u)])
        return c
    lax.fori_loop(0, n // u, step, 0)


def _attn_blocks(r, slope_r, prev_bias, q_ref, kc_ref, vc_ref, kp_ref, vp_ref, bases, g, acc_ref, m_ref, l_ref):
    def rows(start):
        return pl.ds(start, ATT_BLOCK, r) if r > 1 else pl.ds(start, ATT_BLOCK)

    dn = (((1,), (1,)), ((), ()))
    scale = ATT_HEAD_DIM ** -0.5
    qi = lax.broadcasted_iota(jnp.int32, (ATT_BLOCK, ATT_BLOCK), 0)
    ki = lax.broadcasted_iota(jnp.int32, (ATT_BLOCK, ATT_BLOCK), 1)
    dlt = (qi - ki).astype(F32)
    alibi_c = -slope_r * dlt
    alibi_p = -slope_r * (dlt + float(ATT_BLOCK)) + prev_bias

    scores = []
    for base, base_prev in bases:
        q = q_ref[0, rows(base), :].astype(BF16)
        kc = kc_ref[0, rows(base), :].astype(BF16)
        kp = kp_ref[0, rows(base_prev), :].astype(BF16)
        sc = lax.dot_general(q, kc, dn, preferred_element_type=F32) * scale + alibi_c
        sp = lax.dot_general(q, kp, dn, preferred_element_type=F32) * scale + alibi_p
        scores.append((jnp.where(ki <= qi, sc, NEG), jnp.where(ki >= qi, sp, NEG)))
    probs = []
    for sc, sp in scores:
        m = jnp.max(jnp.maximum(sc, sp), axis=-1, keepdims=True)
        pc = jnp.exp(sc - m)
        pp = jnp.exp(sp - m)
        l = jnp.sum(pc + pp, axis=-1, keepdims=True)
        probs.append((m, l, pc.astype(BF16), pp.astype(BF16)))
    outs = []
    for (base, base_prev), (m, l, pc, pp) in zip(bases, probs):
        vc = vc_ref[0, rows(base), :].astype(BF16)
        vp = vp_ref[0, rows(base_prev), :].astype(BF16)
        outs.append(jnp.dot(pc, vc, preferred_element_type=F32) + jnp.dot(pp, vp, preferred_element_type=F32))
    for (base, _), (m, l, _, _), acc in zip(bases, probs, outs):
        acc_ref[g, rows(base), :] = acc
        m_ref[g, rows(base), :] = jnp.broadcast_to(m, (ATT_BLOCK, ATT_HEAD_DIM))
        l_ref[g, rows(base), :] = jnp.broadcast_to(l, (ATT_BLOCK, ATT_HEAD_DIM))


def _attn_kernel(slopes_ref, *refs):
    ng = len(ATT_PATTERNS)
    q_refs = refs[0:ng]
    kc_refs = refs[ng:2 * ng]
    vc_refs = refs[2 * ng:3 * ng]
    kp_refs = refs[3 * ng:4 * ng]
    vp_refs = refs[4 * ng:5 * ng]
    o_ref = refs[5 * ng]
    acc_ref, m_ref, l_ref = refs[5 * ng + 1:]
    s = pl.program_id(1)
    h = pl.program_id(2)
    prev_bias = jnp.where(s > 0, 0.0, NEG).astype(F32)

    for g, (window, r) in enumerate(ATT_PATTERNS):
        assert window // r == ATT_BLOCK
        nblk = ATT_SUPER // (ATT_BLOCK * r)
        slope_r = slopes_ref[g, h] * float(r)
        common = dict(r=r, slope_r=slope_r, g=g, acc_ref=acc_ref, m_ref=m_ref, l_ref=l_ref,
                      q_ref=q_refs[g], kc_ref=kc_refs[g], vc_ref=vc_refs[g])

        def first(ps, common=common, g=g):
            _attn_blocks(prev_bias=prev_bias, kp_ref=kp_refs[g], vp_ref=vp_refs[g],
                         bases=[(p, p) for p in ps], **common)
        _batched_loop(r, first)

        if nblk > 1:
            def rest(idxs, common=common, g=g, r=r, nblk=nblk):
                bases = []
                for idx in idxs:
                    p = idx // (nblk - 1)
                    j = idx % (nblk - 1) + 1
                    base = p + j * (ATT_BLOCK * r)
                    bases.append((base, base - ATT_BLOCK * r))
                _attn_blocks(prev_bias=jnp.float32(0.0), kp_ref=kc_refs[g], vp_ref=vc_refs[g],
                             bases=bases, **common)
            _batched_loop(r * (nblk - 1), rest)

    ch = 256
    def merge(i, c):
        rs = pl.ds(pl.multiple_of(i * ch, ch), ch)
        ms = [m_ref[g, rs, :] for g in range(ng)]
        mx = functools.reduce(jnp.maximum, ms)
        num = jnp.zeros((ch, ATT_HEAD_DIM), F32)
        den = jnp.zeros((ch, ATT_HEAD_DIM), F32)
        for g in range(ng):
            w = jnp.exp(ms[g] - mx)
            num = num + w * acc_ref[g, rs, :]
            den = den + w * l_ref[g, rs, :]
        o_ref[0, rs, :] = (num / den).astype(o_ref.dtype)
        return c
    lax.fori_loop(0, ATT_SUPER // ch, merge, 0)


def _attention(proj_a, B, S):
    ng = len(ATT_PATTERNS)
    nsb = S // ATT_SUPER
    cb = ATT_HEAD_DIM
    slopes = jnp.asarray(_alibi_slopes(ATT_HEADS).reshape(ng, ATT_HEADS_PER_GROUP))

    def cur_spec(col0, g):
        return pl.BlockSpec((1, ATT_SUPER, cb),
                            lambda b, s, h, g=g, col0=col0: (b, s, col0 // cb + g * ATT_HEADS_PER_GROUP + h))

    def prev_spec(col0, g):
        rows = ATT_BLOCK * ATT_PATTERNS[g][1]
        per = ATT_SUPER // rows
        return pl.BlockSpec((1, rows, cb),
                            lambda b, s, h, g=g, col0=col0, per=per: (
                                b, jnp.maximum(s * per - 1, 0), col0 // cb + g * ATT_HEADS_PER_GROUP + h))

    in_specs = [pl.BlockSpec(memory_space=pltpu.SMEM)]
    in_specs += [cur_spec(COL_AQ, g) for g in range(ng)]
    in_specs += [cur_spec(COL_AK, g) for g in range(ng)]
    in_specs += [cur_spec(COL_AV, g) for g in range(ng)]
    in_specs += [prev_spec(COL_AK, g) for g in range(ng)]
    in_specs += [prev_spec(COL_AV, g) for g in range(ng)]
    return pl.pallas_call(
        _attn_kernel,
        grid=(B, nsb, ATT_HEADS_PER_GROUP),
        in_specs=in_specs,
        out_specs=pl.BlockSpec((1, ATT_SUPER, cb), lambda b, s, h: (b, s, h)),
        out_shape=jax.ShapeDtypeStruct((B, S, ATT_OUT_WIDTH), BF16),
        scratch_shapes=[pltpu.VMEM((ng, ATT_SUPER, cb), F32)] * 3,
        compiler_params=_params("parallel", "parallel", "parallel"),
        name="dilated_attention",
    )(slopes, *([proj_a] * (5 * ng)))


def _log_sigmoid(x):
    return jnp.minimum(x, 0.0) - jnp.log(1.0 + jnp.exp(-jnp.abs(x)))


def _mlstm_kernel(mq_ref, mk_ref, mva_ref, mvb_ref, moa_ref, mob_ref, mif_ref, mift_ref, ifb_ref, ifbt_ref,
                  cw_ref, cb_ref, nw_ref, o_ref, tail_ref, c_ref, n_ref, m_ref):
    L = M_CHUNK
    NB = mq_ref.shape[0]
    c = pl.program_id(0)
    mv_refs = (mva_ref, mvb_ref)
    mo_refs = (moa_ref, mob_ref)
    hpb = M_MLSTM_COLS // M_V_DIM
    scale = M_QK_DIM ** -0.5
    hp = lax.Precision.HIGHEST

    @pl.when(c == 0)
    def _():
        tail_ref[...] = jnp.zeros_like(tail_ref)
        c_ref[...] = jnp.zeros_like(c_ref)
        n_ref[...] = jnp.zeros_like(n_ref)
        m_ref[...] = jnp.zeros_like(m_ref)

    ti = lax.broadcasted_iota(jnp.int32, (L, L), 0)
    si = lax.broadcasted_iota(jnp.int32, (L, L), 1)
    causal = si <= ti
    tri = causal.astype(F32)

    def conv_act(x_ref, bb, part):
        cols = slice(part * M_QK_WIDTH, (part + 1) * M_QK_WIDTH)
        x = x_ref[bb]
        xx = jnp.concatenate([tail_ref[bb, :, cols], x], axis=0)
        y = cb_ref[:, cols]
        for j in range(M_CONV):
            off = 8 - (M_CONV - 1) + j
            y = y + cw_ref[j:j + 1, cols] * xx[off:off + L, :]
        tail_ref[bb, :, cols] = x[L - 8:, :]
        return (y * _sigmoid(y)).astype(BF16)

    per_b = []
    for bb in range(NB):
        q_act = conv_act(mq_ref, bb, 0)
        k_act = conv_act(mk_ref, bb, 1)
        gi_c = mif_ref[bb] + ifb_ref[...]
        gi_r = mift_ref[bb] + ifbt_ref[...]
        bcum_c = jnp.dot(tri, _log_sigmoid(gi_c), precision=hp, preferred_element_type=F32)
        bcum_r = lax.dot_general(_log_sigmoid(gi_r), tri, (((1,), (1,)), ((), ())), precision=hp,
                                 preferred_element_type=F32)
        per_b.append((q_act, k_act, gi_c, gi_r, bcum_c, bcum_r))

    chains = [(bb, hd) for bb in range(NB) for hd in range(M_HEADS)]

    ph1 = []
    for bb, hd in chains:
        q_act, k_act, gi_c, gi_r, bcum_c, bcum_r = per_b[bb]
        st = bb * M_HEADS + hd
        q = q_act[:, hd * M_QK_DIM:(hd + 1) * M_QK_DIM]
        k = k_act[:, hd * M_QK_DIM:(hd + 1) * M_QK_DIM]
        b_c = bcum_c[:, M_HEADS + hd:M_HEADS + hd + 1]
        i_c = gi_c[:, hd:hd + 1]
        b_r = bcum_r[M_HEADS + hd:M_HEADS + hd + 1, :]
        i_r = gi_r[hd:hd + 1, :]
        m_prev = m_ref[st]
        dmat = jnp.where(causal, b_c + (i_r - b_r), NEG)
        inter = b_c + m_prev
        m_t = jnp.maximum(inter, jnp.max(dmat, axis=-1, keepdims=True))
        w_intra = jnp.exp(dmat - m_t)
        w_inter = jnp.exp(inter - m_t)
        qk = lax.dot_general(q, k, (((1,), (1,)), ((), ())), preferred_element_type=F32) * scale * w_intra
        ph1.append((q, k, b_c, i_c, m_prev, m_t, w_inter, qk))

    ph2 = []
    for (bb, hd), (q, k, b_c, i_c, m_prev, m_t, w_inter, qk) in zip(chains, ph1):
        st = bb * M_HEADS + hd
        vcols = slice((hd % hpb) * M_V_DIM, (hd % hpb + 1) * M_V_DIM)
        v = mv_refs[hd // hpb][bb, :, vcols].astype(BF16)
        c_prev = c_ref[st]
        n_prev = n_ref[st]
        num = (w_inter * scale) * jnp.dot(q, c_prev.astype(BF16), preferred_element_type=F32) \
            + jnp.dot(qk.astype(BF16), v, preferred_element_type=F32)
        den = (w_inter * scale) * jnp.sum(q.astype(F32) * n_prev, axis=-1, keepdims=True) \
            + jnp.sum(qk, axis=-1, keepdims=True)
        hh = num / jnp.maximum(jnp.abs(den), jnp.exp(-m_t))
        ph2.append((v, c_prev, n_prev, hh))

    for (bb, hd), (q, k, b_c, i_c, m_prev, m_t, w_inter, qk), (v, c_prev, n_prev, hh) in zip(chains, ph1, ph2):
        st = bb * M_HEADS + hd
        b_last = b_c[L - 1:L, :]
        w_log = b_last - b_c + i_c
        m_new = jnp.maximum(b_last + m_prev, jnp.max(w_log, axis=0, keepdims=True))
        wk = jnp.exp(w_log - m_new)
        decay = jnp.exp(b_last + m_prev - m_new)
        kw = (k.astype(F32) * wk)
        c_ref[st] = decay * c_prev + lax.dot_general(kw.astype(BF16), v, (((0,), (0,)), ((), ())),
                                                     preferred_element_type=F32)
        n_ref[st] = decay * n_prev + jnp.sum(kw, axis=0, keepdims=True)
        m_ref[st] = m_new

    for (bb, hd), (v, c_prev, n_prev, hh) in zip(chains, ph2):
        vcols = slice((hd % hpb) * M_V_DIM, (hd % hpb + 1) * M_V_DIM)
        mu = jnp.mean(hh, axis=-1, keepdims=True)
        hc = hh - mu
        var = jnp.mean(hc * hc, axis=-1, keepdims=True)
        hn = hc * lax.rsqrt(var + LN_EPS) * nw_ref[:, hd * M_V_DIM:(hd + 1) * M_V_DIM]
        og = _sigmoid(mo_refs[hd // hpb][bb, :, vcols])
        o_ref[bb, :, hd * M_V_DIM:(hd + 1) * M_V_DIM] = (hn * og).astype(o_ref.dtype)


def _mlstm(proj_a, mif, mif_t, if_bias, conv_w, conv_b, norm_w, B, S):
    L = M_CHUNK
    W = 2 * M_QK_WIDTH
    cw = M_MLSTM_COLS
    ifb = jnp.zeros((1, ROUTE_LANES), F32).at[0, :2 * M_HEADS].set(if_bias)
    ifbt = jnp.broadcast_to(if_bias.reshape(2 * M_HEADS, 1), (2 * M_HEADS, L))

    def col_spec(col0):
        assert col0 % cw == 0
        return pl.BlockSpec((B, L, cw), lambda c, col0=col0: (0, c, col0 // cw))

    const = lambda c: (0, 0)
    return pl.pallas_call(
        _mlstm_kernel,
        grid=(S // L,),
        in_specs=[col_spec(COL_MQK), col_spec(COL_MQK + M_QK_WIDTH),
                  col_spec(COL_MV), col_spec(COL_MV + cw),
                  col_spec(COL_MO), col_spec(COL_MO + cw),
                  pl.BlockSpec((B, L, ROUTE_LANES), lambda c: (0, c, 0)),
                  pl.BlockSpec((B, 2 * M_HEADS, L), lambda c: (0, 0, c)),
                  pl.BlockSpec((1, ROUTE_LANES), const),
                  pl.BlockSpec((2 * M_HEADS, L), const),
                  pl.BlockSpec((M_CONV, W), const),
                  pl.BlockSpec((1, W), const),
                  pl.BlockSpec((1, M_V_WIDTH), const)],
        out_specs=pl.BlockSpec((B, L, M_V_WIDTH), lambda c: (0, c, 0)),
        out_shape=jax.ShapeDtypeStruct((B, S, M_V_WIDTH), BF16),
        scratch_shapes=[pltpu.VMEM((B, 8, W), F32),
                        pltpu.VMEM((B * M_HEADS, M_QK_DIM, M_V_DIM), F32),
                        pltpu.VMEM((B * M_HEADS, 1, M_QK_DIM), F32),
                        pltpu.VMEM((B * M_HEADS, 1, 1), F32)],
        compiler_params=_params("arbitrary"),
        name="mlstm",
    )(proj_a, proj_a, proj_a, proj_a, proj_a, proj_a, mif, mif_t, ifb, ifbt, conv_w, conv_b.reshape(1, W),
      norm_w.reshape(1, M_V_WIDTH))


def _merge_kernel(att_ref, hm_ref, gate_ref, x_ref, g0_ref, b0_ref, wpa_ref, wpm_ref, wo_ref, g_ref, b_ref,
                  wrh_ref, wrl_ref, br_ref, h1_ref, lg_ref):
    halves = [pl.ds(k * MERGE_HALF, MERGE_HALF) for k in range(att_ref.shape[0] // MERGE_HALF)]
    proj = []
    for rs in halves:
        pa = jnp.dot(att_ref[rs, :], wpa_ref[...], preferred_element_type=F32)
        pm = jnp.dot(hm_ref[rs, :], wpm_ref[...], preferred_element_type=F32)
        proj.append((pa, pm))
    ys = []
    for rs, (pa, pm) in zip(halves, proj):
        ga = _sigmoid(gate_ref[rs, :D_MODEL].astype(F32))
        gm = _sigmoid(gate_ref[rs, D_MODEL:].astype(F32))
        merged = (ga * pa + gm * pm).astype(BF16)
        ys.append(jnp.dot(merged, wo_ref[...], preferred_element_type=F32))
    for rs, y in zip(halves, ys):
        h = _layer_norm_rows(x_ref[rs, :], g0_ref[...], b0_ref[...])
        h1 = _layer_norm_rows(DEEPNORM_ALPHA * h + y, g_ref[...], b_ref[...])
        h1_ref[rs, :] = h1
        h1h = h1.astype(BF16)
        h1l = (h1 - h1h.astype(F32)).astype(BF16)
        lg_ref[rs, :] = (jnp.dot(h1h, wrh_ref[...], preferred_element_type=F32)
                         + jnp.dot(h1l, wrh_ref[...], preferred_element_type=F32)
                         + jnp.dot(h1h, wrl_ref[...], preferred_element_type=F32)) + br_ref[...]


MERGE_HALF = 256


def _merge(att, hm, gate, x2, g0, b0, wpa, wpm, wo, g1, b1, wr, br, tm=512):
    T, D = x2.shape
    const = lambda i: (0, 0)
    one = pl.Buffered(1)
    wrh = wr.astype(BF16)
    wrl = (wr - wrh.astype(F32)).astype(BF16)
    return pl.pallas_call(
        _merge_kernel,
        grid=(T // tm,),
        in_specs=[pl.BlockSpec((tm, ATT_OUT_WIDTH), lambda i: (i, 0)),
                  pl.BlockSpec((tm, M_V_WIDTH), lambda i: (i, 0)),
                  pl.BlockSpec((tm, N_BRANCHES * D), lambda i: (i, 0)),
                  pl.BlockSpec((tm, D), lambda i: (i, 0)),
                  pl.BlockSpec((1, D), const),
                  pl.BlockSpec((1, D), const),
                  pl.BlockSpec((ATT_OUT_WIDTH, D), const, pipeline_mode=one),
                  pl.BlockSpec((M_V_WIDTH, D), const, pipeline_mode=one),
                  pl.BlockSpec((D, D), const, pipeline_mode=one),
                  pl.BlockSpec((1, D), const),
                  pl.BlockSpec((1, D), const),
                  pl.BlockSpec((D, ROUTE_LANES), const, pipeline_mode=one),
                  pl.BlockSpec((D, ROUTE_LANES), const, pipeline_mode=one),
                  pl.BlockSpec((1, ROUTE_LANES), const)],
        out_specs=[pl.BlockSpec((tm, D), lambda i: (i, 0)),
                   pl.BlockSpec((tm, ROUTE_LANES), lambda i: (i, 0))],
        out_shape=[jax.ShapeDtypeStruct((T, D), F32), jax.ShapeDtypeStruct((T, ROUTE_LANES), F32)],
        compiler_params=_params("parallel"),
        name="merge_out_ln1",
    )(att, hm, gate, x2, g0.reshape(1, D), b0.reshape(1, D), wpa, wpm, wo, g1.reshape(1, D), b1.reshape(1, D),
      wrh, wrl, br)


def _route_kernel(lg_ref, e_ref, w_ref):
    lg = lg_ref[...]
    col = lax.broadcasted_iota(jnp.int32, lg.shape, 1)
    big = jnp.int32(ROUTE_LANES)

    def first_argmax(v, vmax):
        return jnp.min(jnp.where(v == vmax, col, big), axis=-1, keepdims=True)

    gl = jnp.where(col < N_GROUPS, lg, NEG)
    gmax = jnp.max(gl, axis=-1, keepdims=True)
    grp = first_argmax(gl, gmax)
    gsum = jnp.sum(jnp.where(col < N_GROUPS, jnp.exp(lg - gmax), 0.0), axis=-1, keepdims=True)
    g_w = 1.0 / gsum
    ecol = col - N_GROUPS
    egrp = lax.shift_right_arithmetic(ecol, int(math.log2(EXPERTS_PER_GROUP)))
    in_grp = (ecol >= 0) & (ecol < N_EXPERTS) & (egrp == grp)
    el = jnp.where(in_grp, lg, NEG)
    v1 = jnp.max(el, axis=-1, keepdims=True)
    i1 = first_argmax(el, v1)
    el2 = jnp.where(col == i1, NEG, el)
    v2 = jnp.max(el2, axis=-1, keepdims=True)
    i2 = first_argmax(el2, v2)
    t = jnp.exp(v2 - v1)
    p1 = 1.0 / (1.0 + t)
    p2 = t / (1.0 + t)
    e_ref[...] = jnp.where(col == 0, i1 - N_GROUPS, jnp.where(col == 1, i2 - N_GROUPS, 0))
    w_ref[...] = jnp.where(col == 0, g_w * p1, jnp.where(col == 1, g_w * p2, 0.0))


def _route(logits, tm=1024):
    T = logits.shape[0]
    spec = pl.BlockSpec((tm, ROUTE_LANES), lambda i: (i, 0))
    return pl.pallas_call(
        _route_kernel,
        grid=(T // tm,),
        in_specs=[spec],
        out_specs=[spec, spec],
        out_shape=[jax.ShapeDtypeStruct((T, ROUTE_LANES), jnp.int32),
                   jax.ShapeDtypeStruct((T, ROUTE_LANES), F32)],
        compiler_params=_params("parallel"),
        name="route",
    )(logits)


def _dispatch_plan(e_tk, T):
    M = T * TOP_K
    e_flat = e_tk.reshape(M)
    onehot = (e_flat[:, None] == jnp.arange(N_EXPERTS, dtype=jnp.int32)[None, :]).astype(jnp.int32)
    csum = jnp.cumsum(onehot, axis=0)
    counts = csum[-1]
    rank = jnp.sum((csum - onehot) * onehot, axis=1)
    padded = (counts + MOE_SUB - 1) // MOE_SUB * MOE_SUB
    pstart = jnp.cumsum(padded) - padded
    dest = jnp.sum(onehot * pstart[None, :], axis=1) + rank

    nsb_max = N_EXPERTS + M // MOE_SUPER
    nsb_e = (padded + MOE_SUPER - 1) // MOE_SUPER
    sb_end = jnp.cumsum(nsb_e)
    sb_beg = sb_end - nsb_e
    total = sb_end[-1]
    sb = jnp.arange(nsb_max, dtype=jnp.int32)
    sb_c = jnp.minimum(sb, total - 1)
    ex = jnp.sum((sb_end[None, :] <= sb_c[:, None]).astype(jnp.int32), axis=1)
    local = sb_c - sb_beg[ex]
    row0 = pstart[ex] + local * MOE_SUPER
    active = sb < total
    cnt = jnp.where(active, jnp.clip(counts[ex] - local * MOE_SUPER, 0, MOE_SUPER), 0)
    nsub = jnp.where(active, jnp.clip(padded[ex] - local * MOE_SUPER, 0, MOE_SUPER) // MOE_SUB, 0)
    return (ex.astype(jnp.int32), row0.astype(jnp.int32), cnt.astype(jnp.int32), nsub.astype(jnp.int32),
            dest.astype(jnp.int32), jnp.sum(padded).astype(jnp.int32).reshape(1))


def _moe_kernel(sb_ex, sb_row0, sb_cnt, sb_nsub, dest, used_rows,
                h1_hbm, wga_ref, wua_ref, wda_ref, wgb_ref, wub_ref, wdb_ref, ys_hbm,
                stage_buf, xb_buf, acc_buf, slot_tok, gsem, ssem):
    b = pl.program_id(0)
    j = pl.program_id(1)
    nb = pl.num_programs(0)
    nsub = sb_nsub[b]
    cnt = sb_cnt[b]
    slot = lax.rem(b, 2)
    U = MOE_DMA_UNROLL

    def gather_batches(bb):
        return (sb_cnt[bb] + (U - 1)) // U

    def gather_issue(bb):
        r0 = sb_row0[bb]

        def pad_row(i, c):
            slot_tok[r0 + i] = 0
            return c
        lax.fori_loop(sb_cnt[bb], gather_batches(bb) * U, pad_row, 0)

        def issue(q, c):
            i0 = pl.multiple_of(q * U, U)
            for k in range(U):
                tok = slot_tok[r0 + i0 + k]
                pltpu.make_async_copy(h1_hbm.at[lax.shift_right_logical(tok, 3), pl.ds(tok & 7, 1), :],
                                      stage_buf.at[q * (U // 8) + k // 8, pl.ds(k % 8, 1), :], gsem).start()
            return c
        lax.fori_loop(0, gather_batches(bb), issue, 0)

    def gather_wait(bb):
        def wait(k, c):
            pltpu.make_async_copy(h1_hbm.at[pl.ds(0, U // 8)], stage_buf.at[pl.ds(0, U // 8)], gsem).wait()
            return c
        lax.fori_loop(0, gather_batches(bb), wait, 0)

    def build_tables():
        n_asg = dest.shape[0]
        step = 16

        def fill(q, c):
            for k in range(step):
                slot_tok[dest[q * step + k]] = q * (step // TOP_K) + k // TOP_K
            return c
        lax.fori_loop(0, n_asg // step, fill, 0)

    def out_copy(k, r0):
        rs = pl.ds(pl.multiple_of(k * MOE_SUB, MOE_SUB), MOE_SUB)
        return pltpu.make_async_copy(acc_buf.at[slot, rs, :],
                                     ys_hbm.at[pl.ds(pl.multiple_of(r0 + k * MOE_SUB, MOE_SUB), MOE_SUB), :], ssem)

    def out_issue():
        r0 = sb_row0[b]

        def issue(k, c):
            out_copy(k, r0).start()
            return c
        lax.fori_loop(0, nsub, issue, 0)

    def out_wait(n_sub):
        def wait(k, c):
            out_copy(0, 0).wait()
            return c
        lax.fori_loop(0, n_sub, wait, 0)

    @pl.when(j == 0)
    def _first_step():
        @pl.when(b == 0)
        def _():
            build_tables()

            def clear(q, c):
                stage_buf[q] = jnp.zeros((8, D_MODEL), F32)
                return c
            lax.fori_loop(0, stage_buf.shape[0], clear, 0)
            gather_issue(0)

        @pl.when(nsub > 0)
        def _():
            gather_wait(b)

            def cast(k, c):
                rs = pl.ds(pl.multiple_of(k * MOE_SUB, MOE_SUB), MOE_SUB)
                tiles = pl.ds(pl.multiple_of(k * (MOE_SUB // 8), MOE_SUB // 8), MOE_SUB // 8)
                xb_buf[rs, :] = stage_buf[tiles].reshape(MOE_SUB, D_MODEL).astype(BF16)
                acc_buf[slot, rs, :] = jnp.zeros((MOE_SUB, D_MODEL), F32)
                return c
            lax.fori_loop(0, nsub, cast, 0)

        nxt = jnp.minimum(b + 1, nb - 1)

        @pl.when((b + 1 < nb) & (sb_nsub[nxt] > 0))
        def _():
            gather_issue(nxt)

    def ffn_tile(wg_ref, wu_ref, wd_ref):
        def chunk(r0, rows):
            rs = pl.ds(r0, rows)
            x = xb_buf[rs, :]
            gt = jnp.dot(x, wg_ref[0].astype(BF16), preferred_element_type=F32)
            ut = jnp.dot(x, wu_ref[0].astype(BF16), preferred_element_type=F32)
            hmid = (gt * _sigmoid(gt) * ut).astype(BF16)
            acc_buf[slot, rs, :] += jnp.dot(hmid, wd_ref[0].astype(BF16), preferred_element_type=F32)

        whole = [n for n in range(MOE_CHUNK // MOE_SUB + 1, MOE_WHOLE_MAX // MOE_SUB + 1)]
        is_whole = functools.reduce(jnp.logical_or, [nsub == n for n in whole])
        for n in whole:
            @pl.when(nsub == n)
            def _(n=n):
                chunk(0, n * MOE_SUB)

        @pl.when(jnp.logical_not(is_whole))
        def _():
            per = MOE_CHUNK // MOE_SUB
            nfull = nsub // per

            def full(k, c):
                chunk(pl.multiple_of(k * MOE_CHUNK, MOE_CHUNK), MOE_CHUNK)
                return c
            lax.fori_loop(0, nfull, full, 0)
            rem = nsub - nfull * per
            base = nfull * MOE_CHUNK
            size = MOE_CHUNK // 2
            while size >= MOE_SUB:
                units = size // MOE_SUB

                @pl.when(lax.rem(rem, 2 * units) >= units)
                def _(size=size, units=units):
                    skipped = (rem // (2 * units)) * (2 * units)
                    chunk(pl.multiple_of(base + skipped * MOE_SUB, size), size)
                size //= 2

    @pl.when(nsub > 0)
    def _tiles():
        ffn_tile(wga_ref, wua_ref, wda_ref)

        @pl.when(j == MOE_NJ - 1)
        def _():
            ffn_tile(wgb_ref, wub_ref, wdb_ref)

    @pl.when(j == MOE_NJ - 1)
    def _last_step():
        prev = jnp.maximum(b - 1, 0)

        @pl.when((b > 0) & (sb_nsub[prev] > 0))
        def _():
            out_wait(sb_nsub[prev])

        @pl.when(nsub > 0)
        def _():
            out_issue()

        @pl.when((b == nb - 1) & (nsub > 0))
        def _():
            out_wait(nsub)

        @pl.when(b == nb - 1)
        def _():
            acc_buf[0, pl.ds(0, MOE_SUB), :] = jnp.zeros((MOE_SUB, D_MODEL), F32)

            def fill(k, c):
                rows = pl.ds(pl.multiple_of(k * MOE_SUB, MOE_SUB), MOE_SUB)
                pltpu.make_async_copy(acc_buf.at[0, pl.ds(0, MOE_SUB), :], ys_hbm.at[rows, :], ssem).start()
                return c
            first = used_rows[0] // MOE_SUB
            total = ys_hbm.shape[0] // MOE_SUB
            lax.fori_loop(first, total, fill, 0)

            def drain(k, c):
                out_copy(0, 0).wait()
                return c
            lax.fori_loop(first, total, drain, 0)


def _moe_ffn(h1, plan, w_gate, w_up, w_down):
    T, D = h1.shape
    sb_ex, sb_row0, sb_cnt, sb_nsub, dest, used_rows = plan
    nsb_max = sb_ex.shape[0]
    n_rows = dest.shape[0] + N_EXPERTS * MOE_SUB
    last = MOE_NFT - 1
    tail = D_FF_EXPERT // MOE_FT_TAIL - 1

    def ja(b, j, nsub):
        return jnp.where(nsub[b] > 0, jnp.minimum(j, last), last)

    def main_cols(b, j, ex, r0, ct, ns, ds, ur):
        return (ex[b], 0, ja(b, j, ns))

    def main_rows(b, j, ex, r0, ct, ns, ds, ur):
        return (ex[b], ja(b, j, ns), 0)

    def tail_cols(b, j, ex, r0, ct, ns, ds, ur):
        return (ex[b], 0, tail)

    def tail_rows(b, j, ex, r0, ct, ns, ds, ur):
        return (ex[b], tail, 0)

    grid_spec = pltpu.PrefetchScalarGridSpec(
        num_scalar_prefetch=6,
        grid=(nsb_max, MOE_NJ),
        in_specs=[pl.BlockSpec(memory_space=pl.ANY),
                  pl.BlockSpec((1, D, MOE_FT), main_cols),
                  pl.BlockSpec((1, D, MOE_FT), main_cols),
                  pl.BlockSpec((1, MOE_FT, D), main_rows),
                  pl.BlockSpec((1, D, MOE_FT_TAIL), tail_cols),
                  pl.BlockSpec((1, D, MOE_FT_TAIL), tail_cols),
                  pl.BlockSpec((1, MOE_FT_TAIL, D), tail_rows)],
        out_specs=pl.BlockSpec(memory_space=pl.ANY),
        scratch_shapes=[pltpu.VMEM((MOE_SUPER // 8, 8, D), F32),
                        pltpu.VMEM((MOE_SUPER, D), BF16),
                        pltpu.VMEM((2, MOE_SUPER, D), F32),
                        pltpu.SMEM((n_rows,), jnp.int32),
                        pltpu.SemaphoreType.DMA(()),
                        pltpu.SemaphoreType.DMA(())],
    )
    return pl.pallas_call(
        _moe_kernel,
        grid_spec=grid_spec,
        out_shape=jax.ShapeDtypeStruct((n_rows, D), F32),
        compiler_params=_params("arbitrary", "arbitrary"),
        name="moe_experts",
    )(sb_ex, sb_row0, sb_cnt, sb_nsub, dest, used_rows, h1.reshape(T // 8, 8, D),
      w_gate, w_up, w_down, w_gate, w_up, w_down)


LN_OUT_ROWS = 256
LN_OUT_BATCH = 32
assert LN_OUT_BATCH % (8 * TOP_K) == 0 and (LN_OUT_ROWS * TOP_K) % LN_OUT_BATCH == 0


def _ln_out_kernel(dest, h1_ref, rw_ref, g_ref, b_ref, ys_hbm, o_ref, y_buf, sem):
    i = pl.program_id(0)
    nt = pl.num_programs(0)
    tm = LN_OUT_ROWS
    toks = LN_OUT_BATCH // TOP_K

    def gather_issue(tile, buf):
        base = tile * (tm * TOP_K)

        def issue(q, c):
            for k in range(LN_OUT_BATCH):
                d = dest[base + q * LN_OUT_BATCH + k]
                t = k // TOP_K
                pltpu.make_async_copy(ys_hbm.at[lax.shift_right_logical(d, 3), pl.ds(d & 7, 1), :],
                                      y_buf.at[buf, k % TOP_K, q * (toks // 8) + t // 8, pl.ds(t % 8, 1), :],
                                      sem.at[buf]).start()
            return c
        lax.fori_loop(0, tm * TOP_K // LN_OUT_BATCH, issue, 0)

    def gather_wait(buf):
        for s in range(TOP_K):
            pltpu.make_async_copy(ys_hbm.at[pl.ds(0, tm // 8)], y_buf.at[buf, s], sem.at[buf]).wait()

    @pl.when(i == 0)
    def _():
        gather_issue(0, 0)

    @pl.when(i + 1 < nt)
    def _():
        gather_issue(i + 1, lax.rem(i + 1, 2))

    cur = lax.rem(i, 2)
    gather_wait(cur)
    rw = rw_ref[...]
    z = DEEPNORM_ALPHA * h1_ref[...]
    for s in range(TOP_K):
        z = z + rw[:, s:s + 1] * y_buf[cur, s].reshape(tm, D_MODEL)
    o_ref[...] = _layer_norm_rows(z, g_ref[...], b_ref[...])


def _ln_out(h1, ys, dest, rw, g, b):
    T, D = h1.shape
    tm = LN_OUT_ROWS
    n_rows = ys.shape[0]
    grid_spec = pltpu.PrefetchScalarGridSpec(
        num_scalar_prefetch=1,
        grid=(T // tm,),
        in_specs=[pl.BlockSpec((tm, D), lambda i, ds: (i, 0)),
                  pl.BlockSpec((tm, ROUTE_LANES), lambda i, ds: (i, 0)),
                  pl.BlockSpec((1, D), lambda i, ds: (0, 0)),
                  pl.BlockSpec((1, D), lambda i, ds: (0, 0)),
                  pl.BlockSpec(memory_space=pl.ANY)],
        out_specs=pl.BlockSpec((tm, D), lambda i, ds: (i, 0)),
        scratch_shapes=[pltpu.VMEM((2, TOP_K, tm // 8, 8, D), F32),
                        pltpu.SemaphoreType.DMA((2,))],
    )
    return pl.pallas_call(
        _ln_out_kernel,
        grid_spec=grid_spec,
        out_shape=jax.ShapeDtypeStruct((T, D), F32),
        compiler_params=_params("arbitrary"),
        name="combine_ln2",
    )(dest, h1, rw, g.reshape(1, D), b.reshape(1, D), ys.reshape(n_rows // 8, 8, D))


def kernel(x, ln_in_g, ln_in_b, w_in, m_conv_w, m_conv_b, m_if_bias, m_norm_w, w_proj_att, w_proj_mlstm, w_out,
           ln1_g, ln1_b, w_router_group, b_router_group, w_router_expert, b_router_expert, w_gate, w_up, w_down,
           ln2_g, ln2_b):
    B, S, D = x.shape
    T = B * S
    assert D == D_MODEL and S % ATT_SUPER == 0 and w_in.shape[0] == DEPTH == 1

    x2 = x.reshape(T, D)
    hb = _ln_in(x2, ln_in_g, ln_in_b)
    for l in range(DEPTH):
        wt = jnp.swapaxes(w_in[l], 0, 1)
        proj_a = _matmul_nt(hb, wt, 0, PROJ_A_WIDTH, 1024, 1280, F32, "in_proj_a")
        gate = _matmul_nt(hb, wt, COL_GATE, N_BRANCHES * D, 1024, 1024, BF16, "in_proj_gate")
        mif = _matmul_nt(hb, wt, COL_MIF, ROUTE_LANES, 1024, ROUTE_LANES, F32, "in_proj_if")

        proj_a3 = proj_a.reshape(B, S, PROJ_A_WIDTH)
        att = _attention(proj_a3, B, S)
        mif3 = mif.reshape(B, S, ROUTE_LANES)
        mif_t = jnp.swapaxes(mif3[:, :, :2 * M_HEADS], 1, 2)
        hm = _mlstm(proj_a3, mif3, mif_t, m_if_bias[l], m_conv_w[l], m_conv_b[l], m_norm_w[l], B, S)

        lane_pad = ROUTE_LANES - N_GROUPS - N_EXPERTS
        w_r = jnp.pad(jnp.concatenate([w_router_group[l], w_router_expert[l]], axis=1), ((0, 0), (0, lane_pad)))
        b_r = jnp.pad(jnp.concatenate([b_router_group[l], b_router_expert[l]]), (0, lane_pad)).reshape(1, ROUTE_LANES)
        h1, logits = _merge(att.reshape(T, ATT_OUT_WIDTH), hm.reshape(T, M_V_WIDTH), gate, x2, ln_in_g, ln_in_b,
                            w_proj_att[l].astype(BF16), w_proj_mlstm[l].astype(BF16), w_out[l].astype(BF16),
                            ln1_g[l], ln1_b[l], w_r, b_r)

        e_out, rw = _route(logits)
        plan = _dispatch_plan(e_out[:, :TOP_K], T)
        ys = _moe_ffn(h1, plan, w_gate[l], w_up[l], w_down[l])
        h = _ln_out(h1, ys, plan[4], rw, ln2_g[l], ln2_b[l])
    return h.reshape(B, S, D)
```

```python
import functools
import math

import numpy as np
import jax
import jax.numpy as jnp
from jax import lax
from jax.experimental import pallas as pl
from jax.experimental.pallas import tpu as pltpu

F32 = jnp.float32
BF16 = jnp.bfloat16

D_MODEL = 2048
ATT_HEAD_DIM = 128
ATT_HEADS_PER_GROUP = 4
ATT_PATTERNS = ((128, 1), (512, 4), (2048, 16))
ATT_HEADS = ATT_HEADS_PER_GROUP * len(ATT_PATTERNS)
ATT_WIDTH = ATT_HEADS * ATT_HEAD_DIM
ATT_OUT_WIDTH = ATT_HEADS_PER_GROUP * ATT_HEAD_DIM
ATT_BLOCK = 128
ATT_SUPER = 2048

M_HEADS = 4
M_QK_DIM = 128
M_V_DIM = 256
M_QK_WIDTH = M_HEADS * M_QK_DIM
M_V_WIDTH = M_HEADS * M_V_DIM
M_CONV = 4
M_CHUNK = 128
M_MLSTM_COLS = 512

N_BRANCHES = 2
IN_PROJ_SPLITS = (ATT_WIDTH, ATT_WIDTH, ATT_WIDTH, 2 * M_QK_WIDTH, M_V_WIDTH, M_V_WIDTH,
                  2 * M_HEADS, N_BRANCHES * D_MODEL)
COL_AQ = 0
COL_AK = ATT_WIDTH
COL_AV = 2 * ATT_WIDTH
COL_MQK = 3 * ATT_WIDTH
COL_MV = COL_MQK + 2 * M_QK_WIDTH
COL_MO = COL_MV + M_V_WIDTH
COL_MIF = COL_MO + M_V_WIDTH
COL_GATE = COL_MIF + 2 * M_HEADS
PROJ_A_WIDTH = COL_MIF

N_GROUPS = 4
EXPERTS_PER_GROUP = 8
N_EXPERTS = N_GROUPS * EXPERTS_PER_GROUP
TOP_K = 2
D_FF_EXPERT = 1408
MOE_SUB = 128
MOE_SUPER = 1024
MOE_CHUNK = 512
MOE_WHOLE_MAX = 768
MOE_FT = 256
MOE_NFT = D_FF_EXPERT // MOE_FT
MOE_FT_TAIL = D_FF_EXPERT - MOE_NFT * MOE_FT
MOE_NJ = MOE_NFT
MOE_DMA_UNROLL = 32
assert MOE_DMA_UNROLL % 8 == 0 and MOE_SUB % MOE_DMA_UNROLL == 0
assert MOE_FT_TAIL > 0 and D_FF_EXPERT % MOE_FT_TAIL == 0 and MOE_FT_TAIL % 128 == 0
ROUTE_LANES = 128

DEPTH = 1
DEEPNORM_ALPHA = (2 * DEPTH) ** 0.25
LN_EPS = 1e-5
NEG = -1e30

VMEM_LIMIT = 56 * 1024 * 1024


def _alibi_slopes(n):
    def geometric(k):
        start = 2.0 ** (-8.0 / k)
        return [start ** (i + 1) for i in range(k)]
    c = 2 ** int(math.floor(math.log2(n)))
    s = geometric(c) if c == n else geometric(c) + geometric(2 * c)[0::2][: n - c]
    return np.array(sorted(s, reverse=True), dtype=np.float32)


def _params(*sem):
    return pltpu.CompilerParams(dimension_semantics=sem, vmem_limit_bytes=VMEM_LIMIT)


def _layer_norm_rows(z, g, b):
    mu = jnp.mean(z, axis=-1, keepdims=True)
    zc = z - mu
    var = jnp.mean(zc * zc, axis=-1, keepdims=True)
    return zc * lax.rsqrt(var + LN_EPS) * g + b


def _sigmoid(x):
    return 1.0 / (1.0 + jnp.exp(-x))


def _ln_in_kernel(x_ref, g_ref, b_ref, w_ref, hb_ref, mif_ref):
    hb = _layer_norm_rows(x_ref[...], g_ref[...], b_ref[...]).astype(BF16)
    hb_ref[...] = hb
    mif_ref[...] = lax.dot_general(hb, w_ref[...].astype(BF16), (((1,), (1,)), ((), ())),
                                   preferred_element_type=F32)


def _ln_in(x2, g, b, wt, row0, tm=512):
    T, D = x2.shape
    assert row0 % ROUTE_LANES == 0
    return pl.pallas_call(
        _ln_in_kernel,
        grid=(T // tm,),
        in_specs=[pl.BlockSpec((tm, D), lambda i: (i, 0)),
                  pl.BlockSpec((1, D), lambda i: (0, 0)),
                  pl.BlockSpec((1, D), lambda i: (0, 0)),
                  pl.BlockSpec((ROUTE_LANES, D), lambda i: (row0 // ROUTE_LANES, 0))],
        out_specs=[pl.BlockSpec((tm, D), lambda i: (i, 0)),
                   pl.BlockSpec((tm, ROUTE_LANES), lambda i: (i, 0))],
        out_shape=[jax.ShapeDtypeStruct((T, D), BF16), jax.ShapeDtypeStruct((T, ROUTE_LANES), F32)],
        compiler_params=_params("parallel"),
        name="ln_in",
    )(x2, g.reshape(1, D), b.reshape(1, D), wt)


def _mm_nt_kernel(a_ref, w_ref, o_ref, wb_ref):
    @pl.when(pl.program_id(1) == 0)
    def _():
        wb_ref[...] = w_ref[...].astype(BF16)

    o_ref[...] = lax.dot_general(a_ref[...], wb_ref[...], (((1,), (1,)), ((), ())),
                                 preferred_element_type=F32).astype(o_ref.dtype)


def _matmul_nt(a, wt, row0, n_cols, tm, tn, out_dtype, name):
    T, K = a.shape
    if row0 % tn == 0:
        w_spec = pl.BlockSpec((tn, K), lambda j, i: (j + row0 // tn, 0))
    else:
        assert row0 % 8 == 0 and tn % 8 == 0
        w_spec = pl.BlockSpec((pl.Element(tn), pl.Element(K)),
                              lambda j, i: ((row0 // 8 + j * (tn // 8)) * 8, 0))
    return pl.pallas_call(
        _mm_nt_kernel,
        grid=(n_cols // tn, T // tm),
        in_specs=[pl.BlockSpec((tm, K), lambda j, i: (i, 0)), w_spec],
        out_specs=pl.BlockSpec((tm, tn), lambda j, i: (i, j)),
        out_shape=jax.ShapeDtypeStruct((T, n_cols), out_dtype),
        scratch_shapes=[pltpu.VMEM((tn, K), BF16)],
        compiler_params=_params("parallel", "arbitrary"),
        name=name,
    )(a, wt)


ATT_UNROLL = 8


def _batched_loop(n, body):
    u = max(d for d in range(1, ATT_UNROLL + 1) if n % d == 0)
    if n == u:
        body(list(range(n)))
        return

    def step(i, c):
        body([i * u + k for k in range(u)])
        return c
    lax.fori_loop(0, n // u, step, 0)


def _attn_blocks(r, slope_r, prev_bias, q_ref, kc_ref, vc_ref, kp_ref, vp_ref, bases, g, acc_ref, m_ref, l_ref):
    def rows(start):
        return pl.ds(start, ATT_BLOCK, r) if r > 1 else pl.ds(start, ATT_BLOCK)

    dn = (((1,), (1,)), ((), ()))
    scale = ATT_HEAD_DIM ** -0.5
    qi = lax.broadcasted_iota(jnp.int32, (ATT_BLOCK, ATT_BLOCK), 0)
    ki = lax.broadcasted_iota(jnp.int32, (ATT_BLOCK, ATT_BLOCK), 1)
    dlt = (qi - ki).astype(F32)
    alibi_c = -slope_r * dlt
    alibi_p = -slope_r * (dlt + float(ATT_BLOCK)) + prev_bias

    scores = []
    for base, base_prev in bases:
        q = q_ref[0, rows(base), :].astype(BF16)
        kc = kc_ref[0, rows(base), :].astype(BF16)
        kp = kp_ref[0, rows(base_prev), :].astype(BF16)
        sc = lax.dot_general(q, kc, dn, preferred_element_type=F32) * scale + alibi_c
        sp = lax.dot_general(q, kp, dn, preferred_element_type=F32) * scale + alibi_p
        scores.append((jnp.where(ki <= qi, sc, NEG), jnp.where(ki >= qi, sp, NEG)))
    probs = []
    for sc, sp in scores:
        m = jnp.max(jnp.maximum(sc, sp), axis=-1, keepdims=True)
        pc = jnp.exp(sc - m)
        pp = jnp.exp(sp - m)
        l = jnp.sum(pc + pp, axis=-1, keepdims=True)
        probs.append((m, l, pc.astype(BF16), pp.astype(BF16)))
    outs = []
    for (base, base_prev), (m, l, pc, pp) in zip(bases, probs):
        vc = vc_ref[0, rows(base), :].astype(BF16)
        vp = vp_ref[0, rows(base_prev), :].astype(BF16)
        outs.append(jnp.dot(pc, vc, preferred_element_type=F32) + jnp.dot(pp, vp, preferred_element_type=F32))
    for (base, _), (m, l, _, _), acc in zip(bases, probs, outs):
        acc_ref[g, rows(base), :] = acc
        m_ref[g, rows(base), :] = jnp.broadcast_to(m, (ATT_BLOCK, ATT_HEAD_DIM))
        l_ref[g, rows(base), :] = jnp.broadcast_to(l, (ATT_BLOCK, ATT_HEAD_DIM))


def _attn_kernel(slopes_ref, *refs):
    ng = len(ATT_PATTERNS)
    q_refs = refs[0:ng]
    kc_refs = refs[ng:2 * ng]
    vc_refs = refs[2 * ng:3 * ng]
    kp_refs = refs[3 * ng:4 * ng]
    vp_refs = refs[4 * ng:5 * ng]
    o_ref = refs[5 * ng]
    acc_ref, m_ref, l_ref = refs[5 * ng + 1:]
    s = pl.program_id(1)
    h = pl.program_id(2)
    prev_bias = jnp.where(s > 0, 0.0, NEG).astype(F32)

    for g, (window, r) in enumerate(ATT_PATTERNS):
        assert window // r == ATT_BLOCK
        nblk = ATT_SUPER // (ATT_BLOCK * r)
        slope_r = slopes_ref[g, h] * float(r)
        common = dict(r=r, slope_r=slope_r, g=g, acc_ref=acc_ref, m_ref=m_ref, l_ref=l_ref,
                      q_ref=q_refs[g], kc_ref=kc_refs[g], vc_ref=vc_refs[g])

        def first(ps, common=common, g=g):
            _attn_blocks(prev_bias=prev_bias, kp_ref=kp_refs[g], vp_ref=vp_refs[g],
                         bases=[(p, p) for p in ps], **common)
        _batched_loop(r, first)

        if nblk > 1:
            def rest(idxs, common=common, g=g, r=r, nblk=nblk):
                bases = []
                for idx in idxs:
                    p = idx // (nblk - 1)
                    j = idx % (nblk - 1) + 1
                    base = p + j * (ATT_BLOCK * r)
                    bases.append((base, base - ATT_BLOCK * r))
                _attn_blocks(prev_bias=jnp.float32(0.0), kp_ref=kc_refs[g], vp_ref=vc_refs[g],
                             bases=bases, **common)
            _batched_loop(r * (nblk - 1), rest)

    ch = 256
    def merge(i, c):
        rs = pl.ds(pl.multiple_of(i * ch, ch), ch)
        ms = [m_ref[g, rs, :] for g in range(ng)]
        mx = functools.reduce(jnp.maximum, ms)
        num = jnp.zeros((ch, ATT_HEAD_DIM), F32)
        den = jnp.zeros((ch, ATT_HEAD_DIM), F32)
        for g in range(ng):
            w = jnp.exp(ms[g] - mx)
            num = num + w * acc_ref[g, rs, :]
            den = den + w * l_ref[g, rs, :]
        o_ref[0, rs, :] = (num / den).astype(o_ref.dtype)
        return c
    lax.fori_loop(0, ATT_SUPER // ch, merge, 0)


def _attention(proj_a, B, S):
    ng = len(ATT_PATTERNS)
    nsb = S // ATT_SUPER
    cb = ATT_HEAD_DIM
    slopes = jnp.asarray(_alibi_slopes(ATT_HEADS).reshape(ng, ATT_HEADS_PER_GROUP))

    def cur_spec(col0, g):
        return pl.BlockSpec((1, ATT_SUPER, cb),
                            lambda b, s, h, g=g, col0=col0: (b, s, col0 // cb + g * ATT_HEADS_PER_GROUP + h))

    def prev_spec(col0, g):
        rows = ATT_BLOCK * ATT_PATTERNS[g][1]
        per = ATT_SUPER // rows
        return pl.BlockSpec((1, rows, cb),
                            lambda b, s, h, g=g, col0=col0, per=per: (
                                b, jnp.maximum(s * per - 1, 0), col0 // cb + g * ATT_HEADS_PER_GROUP + h))

    in_specs = [pl.BlockSpec(memory_space=pltpu.SMEM)]
    in_specs += [cur_spec(COL_AQ, g) for g in range(ng)]
    in_specs += [cur_spec(COL_AK, g) for g in range(ng)]
    in_specs += [cur_spec(COL_AV, g) for g in range(ng)]
    in_specs += [prev_spec(COL_AK, g) for g in range(ng)]
    in_specs += [prev_spec(COL_AV, g) for g in range(ng)]
    return pl.pallas_call(
        _attn_kernel,
        grid=(B, nsb, ATT_HEADS_PER_GROUP),
        in_specs=in_specs,
        out_specs=pl.BlockSpec((1, ATT_SUPER, cb), lambda b, s, h: (b, s, h)),
        out_shape=jax.ShapeDtypeStruct((B, S, ATT_OUT_WIDTH), BF16),
        scratch_shapes=[pltpu.VMEM((ng, ATT_SUPER, cb), F32)] * 3,
        compiler_params=_params("parallel", "parallel", "parallel"),
        name="dilated_attention",
    )(slopes, *([proj_a] * (5 * ng)))


def _log_sigmoid(x):
    return jnp.minimum(x, 0.0) - jnp.log(1.0 + jnp.exp(-jnp.abs(x)))


def _mlstm_kernel(mq_ref, mk_ref, mva_ref, mvb_ref, moa_ref, mob_ref, mif_ref, mift_ref, ifb_ref, ifbt_ref,
                  cw_ref, cb_ref, nw_ref, o_ref, tail_ref, c_ref, n_ref, m_ref):
    L = M_CHUNK
    NB = mq_ref.shape[0]
    c = pl.program_id(0)
    mv_refs = (mva_ref, mvb_ref)
    mo_refs = (moa_ref, mob_ref)
    hpb = M_MLSTM_COLS // M_V_DIM
    scale = M_QK_DIM ** -0.5
    hp = lax.Precision.HIGHEST

    @pl.when(c == 0)
    def _():
        tail_ref[...] = jnp.zeros_like(tail_ref)
        c_ref[...] = jnp.zeros_like(c_ref)
        n_ref[...] = jnp.zeros_like(n_ref)
        m_ref[...] = jnp.zeros_like(m_ref)

    ti = lax.broadcasted_iota(jnp.int32, (L, L), 0)
    si = lax.broadcasted_iota(jnp.int32, (L, L), 1)
    causal = si <= ti
    tri = causal.astype(F32)

    def conv_act(x_ref, bb, part, out_scale):
        cols = slice(part * M_QK_WIDTH, (part + 1) * M_QK_WIDTH)
        x = x_ref[bb]
        xx = jnp.concatenate([tail_ref[bb, :, cols], x], axis=0)
        y = cb_ref[:, cols]
        for j in range(M_CONV):
            off = 8 - (M_CONV - 1) + j
            y = y + cw_ref[j:j + 1, cols] * xx[off:off + L, :]
        tail_ref[bb, :, cols] = x[L - 8:, :]
        act = y * _sigmoid(y)
        return (act if out_scale == 1.0 else act * out_scale).astype(BF16)

    per_b = []
    for bb in range(NB):
        q_act = conv_act(mq_ref, bb, 0, scale)
        k_act = conv_act(mk_ref, bb, 1, 1.0)
        gi_c = mif_ref[bb] + ifb_ref[...]
        gi_r = mift_ref[bb] + ifbt_ref[...]
        bcum_c = jnp.dot(tri, _log_sigmoid(gi_c), precision=hp, preferred_element_type=F32)
        bcum_r = lax.dot_general(_log_sigmoid(gi_r), tri, (((1,), (1,)), ((), ())), precision=hp,
                                 preferred_element_type=F32)
        per_b.append((q_act, k_act, gi_c, gi_r, bcum_c, bcum_r))

    chains = [(bb, hd) for bb in range(NB) for hd in range(M_HEADS)]

    ph1 = []
    for bb, hd in chains:
        q_act, k_act, gi_c, gi_r, bcum_c, bcum_r = per_b[bb]
        st = bb * M_HEADS + hd
        q = q_act[:, hd * M_QK_DIM:(hd + 1) * M_QK_DIM]
        k = k_act[:, hd * M_QK_DIM:(hd + 1) * M_QK_DIM]
        b_c = bcum_c[:, M_HEADS + hd:M_HEADS + hd + 1]
        i_c = gi_c[:, hd:hd + 1]
        b_r = bcum_r[M_HEADS + hd:M_HEADS + hd + 1, :]
        i_r = gi_r[hd:hd + 1, :]
        m_prev = m_ref[st]
        dmat = jnp.where(causal, b_c + (i_r - b_r), NEG)
        inter = b_c + m_prev
        m_t = jnp.maximum(inter, jnp.max(dmat, axis=-1, keepdims=True))
        w_intra = jnp.exp(dmat - m_t)
        w_inter = jnp.exp(inter - m_t)
        qk = lax.dot_general(q, k, (((1,), (1,)), ((), ())), preferred_element_type=F32) * w_intra
        ph1.append((q, k, b_c, i_c, m_prev, m_t, w_inter, qk))

    ph2 = []
    for (bb, hd), (q, k, b_c, i_c, m_prev, m_t, w_inter, qk) in zip(chains, ph1):
        st = bb * M_HEADS + hd
        vcols = slice((hd % hpb) * M_V_DIM, (hd % hpb + 1) * M_V_DIM)
        v = mv_refs[hd // hpb][bb, :, vcols].astype(BF16)
        c_prev = c_ref[st]
        n_prev = n_ref[st]
        num = w_inter * jnp.dot(q, c_prev.astype(BF16), preferred_element_type=F32) \
            + jnp.dot(qk.astype(BF16), v, preferred_element_type=F32)
        den = w_inter * jnp.sum(q.astype(F32) * n_prev, axis=-1, keepdims=True) \
            + jnp.sum(qk, axis=-1, keepdims=True)
        hh = num / jnp.maximum(jnp.abs(den), jnp.exp(-m_t))
        ph2.append((v, c_prev, n_prev, hh))

    for (bb, hd), (q, k, b_c, i_c, m_prev, m_t, w_inter, qk), (v, c_prev, n_prev, hh) in zip(chains, ph1, ph2):
        st = bb * M_HEADS + hd
        b_last = b_c[L - 1:L, :]
        w_log = b_last - b_c + i_c
        m_new = jnp.maximum(b_last + m_prev, jnp.max(w_log, axis=0, keepdims=True))
        wk = jnp.exp(w_log - m_new)
        decay = jnp.exp(b_last + m_prev - m_new)
        kw = (k.astype(F32) * wk)
        c_ref[st] = decay * c_prev + lax.dot_general(kw.astype(BF16), v, (((0,), (0,)), ((), ())),
                                                     preferred_element_type=F32)
        n_ref[st] = decay * n_prev + jnp.sum(kw, axis=0, keepdims=True)
        m_ref[st] = m_new

    for (bb, hd), (v, c_prev, n_prev, hh) in zip(chains, ph2):
        vcols = slice((hd % hpb) * M_V_DIM, (hd % hpb + 1) * M_V_DIM)
        mu = jnp.mean(hh, axis=-1, keepdims=True)
        hc = hh - mu
        var = jnp.mean(hc * hc, axis=-1, keepdims=True)
        hn = hc * lax.rsqrt(var + LN_EPS) * nw_ref[:, hd * M_V_DIM:(hd + 1) * M_V_DIM]
        og = _sigmoid(mo_refs[hd // hpb][bb, :, vcols])
        o_ref[bb, :, hd * M_V_DIM:(hd + 1) * M_V_DIM] = (hn * og).astype(o_ref.dtype)


def _mlstm(proj_a, mif, mif_t, if_bias, conv_w, conv_b, norm_w, B, S):
    L = M_CHUNK
    W = 2 * M_QK_WIDTH
    cw = M_MLSTM_COLS
    ifb = jnp.zeros((1, ROUTE_LANES), F32).at[0, :2 * M_HEADS].set(if_bias)
    ifbt = jnp.broadcast_to(if_bias.reshape(2 * M_HEADS, 1), (2 * M_HEADS, L))

    def col_spec(col0):
        assert col0 % cw == 0
        return pl.BlockSpec((B, L, cw), lambda c, col0=col0: (0, c, col0 // cw))

    const = lambda c: (0, 0)
    return pl.pallas_call(
        _mlstm_kernel,
        grid=(S // L,),
        in_specs=[col_spec(COL_MQK), col_spec(COL_MQK + M_QK_WIDTH),
                  col_spec(COL_MV), col_spec(COL_MV + cw),
                  col_spec(COL_MO), col_spec(COL_MO + cw),
                  pl.BlockSpec((B, L, ROUTE_LANES), lambda c: (0, c, 0)),
                  pl.BlockSpec((B, 2 * M_HEADS, L), lambda c: (0, 0, c)),
                  pl.BlockSpec((1, ROUTE_LANES), const),
                  pl.BlockSpec((2 * M_HEADS, L), const),
                  pl.BlockSpec((M_CONV, W), const),
                  pl.BlockSpec((1, W), const),
                  pl.BlockSpec((1, M_V_WIDTH), const)],
        out_specs=pl.BlockSpec((B, L, M_V_WIDTH), lambda c: (0, c, 0)),
        out_shape=jax.ShapeDtypeStruct((B, S, M_V_WIDTH), BF16),
        scratch_shapes=[pltpu.VMEM((B, 8, W), F32),
                        pltpu.VMEM((B * M_HEADS, M_QK_DIM, M_V_DIM), F32),
                        pltpu.VMEM((B * M_HEADS, 1, M_QK_DIM), F32),
                        pltpu.VMEM((B * M_HEADS, 1, 1), F32)],
        compiler_params=_params("arbitrary"),
        name="mlstm",
    )(proj_a, proj_a, proj_a, proj_a, proj_a, proj_a, mif, mif_t, ifb, ifbt, conv_w, conv_b.reshape(1, W),
      norm_w.reshape(1, M_V_WIDTH))


def _merge_kernel(att_ref, hm_ref, gate_ref, x_ref, g0_ref, b0_ref, wpa_ref, wpm_ref, wo_ref, g_ref, b_ref,
                  wrh_ref, wrl_ref, br_ref, h1_ref, lg_ref):
    halves = [pl.ds(k * MERGE_HALF, MERGE_HALF) for k in range(att_ref.shape[0] // MERGE_HALF)]
    proj = []
    for rs in halves:
        pa = jnp.dot(att_ref[rs, :], wpa_ref[...], preferred_element_type=F32)
        pm = jnp.dot(hm_ref[rs, :], wpm_ref[...], preferred_element_type=F32)
        proj.append((pa, pm))
    ys = []
    for rs, (pa, pm) in zip(halves, proj):
        ga = _sigmoid(gate_ref[rs, :D_MODEL].astype(F32))
        gm = _sigmoid(gate_ref[rs, D_MODEL:].astype(F32))
        merged = (ga * pa + gm * pm).astype(BF16)
        ys.append(jnp.dot(merged, wo_ref[...], preferred_element_type=F32))
    for rs, y in zip(halves, ys):
        h = _layer_norm_rows(x_ref[rs, :], g0_ref[...], b0_ref[...])
        h1 = _layer_norm_rows(DEEPNORM_ALPHA * h + y, g_ref[...], b_ref[...])
        h1_ref[rs, :] = h1
        h1h = h1.astype(BF16)
        h1l = (h1 - h1h.astype(F32)).astype(BF16)
        lg_ref[rs, :] = (jnp.dot(h1h, wrh_ref[...], preferred_element_type=F32)
                         + jnp.dot(h1l, wrh_ref[...], preferred_element_type=F32)
                         + jnp.dot(h1h, wrl_ref[...], preferred_element_type=F32)) + br_ref[...]


MERGE_HALF = 256


def _merge(att, hm, gate, x2, g0, b0, wpa, wpm, wo, g1, b1, wr, br, tm=512):
    T, D = x2.shape
    const = lambda i: (0, 0)
    one = pl.Buffered(1)
    wrh = wr.astype(BF16)
    wrl = (wr - wrh.astype(F32)).astype(BF16)
    return pl.pallas_call(
        _merge_kernel,
        grid=(T // tm,),
        in_specs=[pl.BlockSpec((tm, ATT_OUT_WIDTH), lambda i: (i, 0)),
                  pl.BlockSpec((tm, M_V_WIDTH), lambda i: (i, 0)),
                  pl.BlockSpec((tm, N_BRANCHES * D), lambda i: (i, 0)),
                  pl.BlockSpec((tm, D), lambda i: (i, 0)),
                  pl.BlockSpec((1, D), const),
                  pl.BlockSpec((1, D), const),
                  pl.BlockSpec((ATT_OUT_WIDTH, D), const, pipeline_mode=one),
                  pl.BlockSpec((M_V_WIDTH, D), const, pipeline_mode=one),
                  pl.BlockSpec((D, D), const, pipeline_mode=one),
                  pl.BlockSpec((1, D), const),
                  pl.BlockSpec((1, D), const),
                  pl.BlockSpec((D, ROUTE_LANES), const, pipeline_mode=one),
                  pl.BlockSpec((D, ROUTE_LANES), const, pipeline_mode=one),
                  pl.BlockSpec((1, ROUTE_LANES), const)],
        out_specs=[pl.BlockSpec((tm, D), lambda i: (i, 0)),
                   pl.BlockSpec((tm, ROUTE_LANES), lambda i: (i, 0))],
        out_shape=[jax.ShapeDtypeStruct((T, D), F32), jax.ShapeDtypeStruct((T, ROUTE_LANES), F32)],
        compiler_params=_params("parallel"),
        name="merge_out_ln1",
    )(att, hm, gate, x2, g0.reshape(1, D), b0.reshape(1, D), wpa, wpm, wo, g1.reshape(1, D), b1.reshape(1, D),
      wrh, wrl, br)


def _route_kernel(lg_ref, e_ref, w_ref):
    lg = lg_ref[...]
    col = lax.broadcasted_iota(jnp.int32, lg.shape, 1)
    big = jnp.int32(ROUTE_LANES)

    def first_argmax(v, vmax):
        return jnp.min(jnp.where(v == vmax, col, big), axis=-1, keepdims=True)

    gl = jnp.where(col < N_GROUPS, lg, NEG)
    gmax = jnp.max(gl, axis=-1, keepdims=True)
    grp = first_argmax(gl, gmax)
    gsum = jnp.sum(jnp.where(col < N_GROUPS, jnp.exp(lg - gmax), 0.0), axis=-1, keepdims=True)
    g_w = 1.0 / gsum
    ecol = col - N_GROUPS
    egrp = lax.shift_right_arithmetic(ecol, int(math.log2(EXPERTS_PER_GROUP)))
    in_grp = (ecol >= 0) & (ecol < N_EXPERTS) & (egrp == grp)
    el = jnp.where(in_grp, lg, NEG)
    v1 = jnp.max(el, axis=-1, keepdims=True)
    i1 = first_argmax(el, v1)
    el2 = jnp.where(col == i1, NEG, el)
    v2 = jnp.max(el2, axis=-1, keepdims=True)
    i2 = first_argmax(el2, v2)
    t = jnp.exp(v2 - v1)
    p1 = 1.0 / (1.0 + t)
    p2 = t / (1.0 + t)
    e_ref[...] = jnp.where(col == 0, i1 - N_GROUPS, jnp.where(col == 1, i2 - N_GROUPS, 0))
    w_ref[...] = jnp.where(col == 0, g_w * p1, jnp.where(col == 1, g_w * p2, 0.0))


def _route(logits, tm=1024):
    T = logits.shape[0]
    spec = pl.BlockSpec((tm, ROUTE_LANES), lambda i: (i, 0))
    return pl.pallas_call(
        _route_kernel,
        grid=(T // tm,),
        in_specs=[spec],
        out_specs=[spec, spec],
        out_shape=[jax.ShapeDtypeStruct((T, ROUTE_LANES), jnp.int32),
                   jax.ShapeDtypeStruct((T, ROUTE_LANES), F32)],
        compiler_params=_params("parallel"),
        name="route",
    )(logits)


def _dispatch_plan(e_tk, T):
    M = T * TOP_K
    e_flat = e_tk.reshape(M)
    onehot = (e_flat[:, None] == jnp.arange(N_EXPERTS, dtype=jnp.int32)[None, :]).astype(jnp.int32)
    csum = jnp.cumsum(onehot, axis=0)
    counts = csum[-1]
    rank = jnp.sum((csum - onehot) * onehot, axis=1)
    padded = (counts + MOE_SUB - 1) // MOE_SUB * MOE_SUB
    pstart = jnp.cumsum(padded) - padded
    dest = jnp.sum(onehot * pstart[None, :], axis=1) + rank

    nsb_max = N_EXPERTS + M // MOE_SUPER
    nsb_e = (padded + MOE_SUPER - 1) // MOE_SUPER
    sb_end = jnp.cumsum(nsb_e)
    sb_beg = sb_end - nsb_e
    total = sb_end[-1]
    sb = jnp.arange(nsb_max, dtype=jnp.int32)
    sb_c = jnp.minimum(sb, total - 1)
    ex = jnp.sum((sb_end[None, :] <= sb_c[:, None]).astype(jnp.int32), axis=1)
    local = sb_c - sb_beg[ex]
    row0 = pstart[ex] + local * MOE_SUPER
    active = sb < total
    cnt = jnp.where(active, jnp.clip(counts[ex] - local * MOE_SUPER, 0, MOE_SUPER), 0)
    nsub = jnp.where(active, jnp.clip(padded[ex] - local * MOE_SUPER, 0, MOE_SUPER) // MOE_SUB, 0)
    return (ex.astype(jnp.int32), row0.astype(jnp.int32), cnt.astype(jnp.int32), nsub.astype(jnp.int32),
            dest.astype(jnp.int32), jnp.sum(padded).astype(jnp.int32).reshape(1))


def _moe_kernel(sb_ex, sb_row0, sb_cnt, sb_nsub, dest, used_rows,
                h1_hbm, wga_ref, wua_ref, wda_ref, wgb_ref, wub_ref, wdb_ref, ys_hbm,
                stage_buf, xb_buf, acc_buf, slot_tok, gsem, ssem):
    b = pl.program_id(0)
    j = pl.program_id(1)
    nb = pl.num_programs(0)
    nsub = sb_nsub[b]
    cnt = sb_cnt[b]
    slot = lax.rem(b, 2)
    U = MOE_DMA_UNROLL

    def gather_batches(bb):
        return (sb_cnt[bb] + (U - 1)) // U

    def gather_issue(bb):
        r0 = sb_row0[bb]

        def pad_row(i, c):
            slot_tok[r0 + i] = 0
            return c
        lax.fori_loop(sb_cnt[bb], gather_batches(bb) * U, pad_row, 0)

        def issue(q, c):
            i0 = pl.multiple_of(q * U, U)
            for k in range(U):
                tok = slot_tok[r0 + i0 + k]
                pltpu.make_async_copy(h1_hbm.at[lax.shift_right_logical(tok, 3), pl.ds(tok & 7, 1), :],
                                      stage_buf.at[q * (U // 8) + k // 8, pl.ds(k % 8, 1), :], gsem).start()
            return c
        lax.fori_loop(0, gather_batches(bb), issue, 0)

    def gather_wait(bb):
        def wait(k, c):
            pltpu.make_async_copy(h1_hbm.at[pl.ds(0, U // 8)], stage_buf.at[pl.ds(0, U // 8)], gsem).wait()
            return c
        lax.fori_loop(0, gather_batches(bb), wait, 0)

    def build_tables():
        n_asg = dest.shape[0]
        step = 16

        def fill(q, c):
            for k in range(step):
                slot_tok[dest[q * step + k]] = q * (step // TOP_K) + k // TOP_K
            return c
        lax.fori_loop(0, n_asg // step, fill, 0)

    def out_copy(k, r0):
        rs = pl.ds(pl.multiple_of(k * MOE_SUB, MOE_SUB), MOE_SUB)
        return pltpu.make_async_copy(acc_buf.at[slot, rs, :],
                                     ys_hbm.at[pl.ds(pl.multiple_of(r0 + k * MOE_SUB, MOE_SUB), MOE_SUB), :], ssem)

    def out_issue():
        r0 = sb_row0[b]

        def issue(k, c):
            out_copy(k, r0).start()
            return c
        lax.fori_loop(0, nsub, issue, 0)

    def out_wait(n_sub):
        def wait(k, c):
            out_copy(0, 0).wait()
            return c
        lax.fori_loop(0, n_sub, wait, 0)

    @pl.when(j == 0)
    def _first_step():
        @pl.when(b == 0)
        def _():
            build_tables()

            def clear(q, c):
                stage_buf[q] = jnp.zeros((8, D_MODEL), F32)
                return c
            lax.fori_loop(0, stage_buf.shape[0], clear, 0)
            gather_issue(0)

        @pl.when(nsub > 0)
        def _():
            gather_wait(b)

            def cast(k, c):
                rs = pl.ds(pl.multiple_of(k * MOE_SUB, MOE_SUB), MOE_SUB)
                tiles = pl.ds(pl.multiple_of(k * (MOE_SUB // 8), MOE_SUB // 8), MOE_SUB // 8)
                xb_buf[rs, :] = stage_buf[tiles].reshape(MOE_SUB, D_MODEL).astype(BF16)
                acc_buf[slot, rs, :] = jnp.zeros((MOE_SUB, D_MODEL), F32)
                return c
            lax.fori_loop(0, nsub, cast, 0)

        nxt = jnp.minimum(b + 1, nb - 1)

        @pl.when((b + 1 < nb) & (sb_nsub[nxt] > 0))
        def _():
            gather_issue(nxt)

    def ffn_tile(wg_ref, wu_ref, wd_ref):
        def chunk(r0, rows):
            rs = pl.ds(r0, rows)
            x = xb_buf[rs, :]
            gt = jnp.dot(x, wg_ref[0].astype(BF16), preferred_element_type=F32)
            ut = jnp.dot(x, wu_ref[0].astype(BF16), preferred_element_type=F32)
            hmid = (gt * _sigmoid(gt) * ut).astype(BF16)
            acc_buf[slot, rs, :] += jnp.dot(hmid, wd_ref[0].astype(BF16), preferred_element_type=F32)

        whole = [n for n in range(MOE_CHUNK // MOE_SUB + 1, MOE_WHOLE_MAX // MOE_SUB + 1)]
        is_whole = functools.reduce(jnp.logical_or, [nsub == n for n in whole])
        for n in whole:
            @pl.when(nsub == n)
            def _(n=n):
                chunk(0, n * MOE_SUB)

        @pl.when(jnp.logical_not(is_whole))
        def _():
            per = MOE_CHUNK // MOE_SUB
            nfull = nsub // per

            def full(k, c):
                chunk(pl.multiple_of(k * MOE_CHUNK, MOE_CHUNK), MOE_CHUNK)
                return c
            lax.fori_loop(0, nfull, full, 0)
            rem = nsub - nfull * per
            base = nfull * MOE_CHUNK
            size = MOE_CHUNK // 2
            while size >= MOE_SUB:
                units = size // MOE_SUB

                @pl.when(lax.rem(rem, 2 * units) >= units)
                def _(size=size, units=units):
                    skipped = (rem // (2 * units)) * (2 * units)
                    chunk(pl.multiple_of(base + skipped * MOE_SUB, size), size)
                size //= 2

    @pl.when(nsub > 0)
    def _tiles():
        ffn_tile(wga_ref, wua_ref, wda_ref)

        @pl.when(j == MOE_NJ - 1)
        def _():
            ffn_tile(wgb_ref, wub_ref, wdb_ref)

    @pl.when(j == MOE_NJ - 1)
    def _last_step():
        prev = jnp.maximum(b - 1, 0)

        @pl.when((b > 0) & (sb_nsub[prev] > 0))
        def _():
            out_wait(sb_nsub[prev])

        @pl.when(nsub > 0)
        def _():
            out_issue()

        @pl.when((b == nb - 1) & (nsub > 0))
        def _():
            out_wait(nsub)

        @pl.when(b == nb - 1)
        def _():
            acc_buf[0, pl.ds(0, MOE_SUB), :] = jnp.zeros((MOE_SUB, D_MODEL), F32)

            def fill(k, c):
                rows = pl.ds(pl.multiple_of(k * MOE_SUB, MOE_SUB), MOE_SUB)
                pltpu.make_async_copy(acc_buf.at[0, pl.ds(0, MOE_SUB), :], ys_hbm.at[rows, :], ssem).start()
                return c
            first = used_rows[0] // MOE_SUB
            total = ys_hbm.shape[0] // MOE_SUB
            lax.fori_loop(first, total, fill, 0)

            def drain(k, c):
                out_copy(0, 0).wait()
                return c
            lax.fori_loop(first, total, drain, 0)


def _moe_ffn(h1, plan, w_gate, w_up, w_down):
    T, D = h1.shape
    sb_ex, sb_row0, sb_cnt, sb_nsub, dest, used_rows = plan
    nsb_max = sb_ex.shape[0]
    n_rows = dest.shape[0] + N_EXPERTS * MOE_SUB
    last = MOE_NFT - 1
    tail = D_FF_EXPERT // MOE_FT_TAIL - 1

    def ja(b, j, nsub):
        return jnp.where(nsub[b] > 0, jnp.minimum(j, last), last)

    def main_cols(b, j, ex, r0, ct, ns, ds, ur):
        return (ex[b], 0, ja(b, j, ns))

    def main_rows(b, j, ex, r0, ct, ns, ds, ur):
        return (ex[b], ja(b, j, ns), 0)

    def tail_cols(b, j, ex, r0, ct, ns, ds, ur):
        return (ex[b], 0, tail)

    def tail_rows(b, j, ex, r0, ct, ns, ds, ur):
        return (ex[b], tail, 0)

    grid_spec = pltpu.PrefetchScalarGridSpec(
        num_scalar_prefetch=6,
        grid=(nsb_max, MOE_NJ),
        in_specs=[pl.BlockSpec(memory_space=pl.ANY),
                  pl.BlockSpec((1, D, MOE_FT), main_cols),
                  pl.BlockSpec((1, D, MOE_FT), main_cols),
                  pl.BlockSpec((1, MOE_FT, D), main_rows),
                  pl.BlockSpec((1, D, MOE_FT_TAIL), tail_cols),
                  pl.BlockSpec((1, D, MOE_FT_TAIL), tail_cols),
                  pl.BlockSpec((1, MOE_FT_TAIL, D), tail_rows)],
        out_specs=pl.BlockSpec(memory_space=pl.ANY),
        scratch_shapes=[pltpu.VMEM((MOE_SUPER // 8, 8, D), F32),
                        pltpu.VMEM((MOE_SUPER, D), BF16),
                        pltpu.VMEM((2, MOE_SUPER, D), F32),
                        pltpu.SMEM((n_rows,), jnp.int32),
                        pltpu.SemaphoreType.DMA(()),
                        pltpu.SemaphoreType.DMA(())],
    )
    return pl.pallas_call(
        _moe_kernel,
        grid_spec=grid_spec,
        out_shape=jax.ShapeDtypeStruct((n_rows, D), F32),
        compiler_params=_params("arbitrary", "arbitrary"),
        name="moe_experts",
    )(sb_ex, sb_row0, sb_cnt, sb_nsub, dest, used_rows, h1.reshape(T // 8, 8, D),
      w_gate, w_up, w_down, w_gate, w_up, w_down)


LN_OUT_ROWS = 256
LN_OUT_BATCH = 32
assert LN_OUT_BATCH % (8 * TOP_K) == 0 and (LN_OUT_ROWS * TOP_K) % LN_OUT_BATCH == 0


def _ln_out_kernel(dest, h1_ref, rw_ref, g_ref, b_ref, ys_hbm, o_ref, y_buf, sem):
    i = pl.program_id(0)
    nt = pl.num_programs(0)
    tm = LN_OUT_ROWS
    toks = LN_OUT_BATCH // TOP_K

    def gather_issue(tile, buf):
        base = tile * (tm * TOP_K)

        def issue(q, c):
            for k in range(LN_OUT_BATCH):
                d = dest[base + q * LN_OUT_BATCH + k]
                t = k // TOP_K
                pltpu.make_async_copy(ys_hbm.at[lax.shift_right_logical(d, 3), pl.ds(d & 7, 1), :],
                                      y_buf.at[buf, k % TOP_K, q * (toks // 8) + t // 8, pl.ds(t % 8, 1), :],
                                      sem.at[buf]).start()
            return c
        lax.fori_loop(0, tm * TOP_K // LN_OUT_BATCH, issue, 0)

    def gather_wait(buf):
        for s in range(TOP_K):
            pltpu.make_async_copy(ys_hbm.at[pl.ds(0, tm // 8)], y_buf.at[buf, s], sem.at[buf]).wait()

    @pl.when(i == 0)
    def _():
        gather_issue(0, 0)

    @pl.when(i + 1 < nt)
    def _():
        gather_issue(i + 1, lax.rem(i + 1, 2))

    cur = lax.rem(i, 2)
    gather_wait(cur)
    rw = rw_ref[...]
    z = DEEPNORM_ALPHA * h1_ref[...]
    for s in range(TOP_K):
        z = z + rw[:, s:s + 1] * y_buf[cur, s].reshape(tm, D_MODEL)
    o_ref[...] = _layer_norm_rows(z, g_ref[...], b_ref[...])


def _ln_out(h1, ys, dest, rw, g, b):
    T, D = h1.shape
    tm = LN_OUT_ROWS
    n_rows = ys.shape[0]
    grid_spec = pltpu.PrefetchScalarGridSpec(
        num_scalar_prefetch=1,
        grid=(T // tm,),
        in_specs=[pl.BlockSpec((tm, D), lambda i, ds: (i, 0)),
                  pl.BlockSpec((tm, ROUTE_LANES), lambda i, ds: (i, 0)),
                  pl.BlockSpec((1, D), lambda i, ds: (0, 0)),
                  pl.BlockSpec((1, D), lambda i, ds: (0, 0)),
                  pl.BlockSpec(memory_space=pl.ANY)],
        out_specs=pl.BlockSpec((tm, D), lambda i, ds: (i, 0)),
        scratch_shapes=[pltpu.VMEM((2, TOP_K, tm // 8, 8, D), F32),
                        pltpu.SemaphoreType.DMA((2,))],
    )
    return pl.pallas_call(
        _ln_out_kernel,
        grid_spec=grid_spec,
        out_shape=jax.ShapeDtypeStruct((T, D), F32),
        compiler_params=_params("arbitrary"),
        name="combine_ln2",
    )(dest, h1, rw, g.reshape(1, D), b.reshape(1, D), ys.reshape(n_rows // 8, 8, D))


def kernel(x, ln_in_g, ln_in_b, w_in, m_conv_w, m_conv_b, m_if_bias, m_norm_w, w_proj_att, w_proj_mlstm, w_out,
           ln1_g, ln1_b, w_router_group, b_router_group, w_router_expert, b_router_expert, w_gate, w_up, w_down,
           ln2_g, ln2_b):
    B, S, D = x.shape
    T = B * S
    assert D == D_MODEL and S % ATT_SUPER == 0 and w_in.shape[0] == DEPTH == 1

    x2 = x.reshape(T, D)
    for l in range(DEPTH):
        wt = jnp.swapaxes(w_in[l], 0, 1)
        hb, mif = _ln_in(x2, ln_in_g, ln_in_b, wt, COL_MIF)
        proj_a = _matmul_nt(hb, wt, 0, PROJ_A_WIDTH, 1024, 1280, F32, "in_proj_a")
        gate = _matmul_nt(hb, wt, COL_GATE, N_BRANCHES * D, 1024, 1024, BF16, "in_proj_gate")

        proj_a3 = proj_a.reshape(B, S, PROJ_A_WIDTH)
        att = _attention(proj_a3, B, S)
        mif3 = mif.reshape(B, S, ROUTE_LANES)
        mif_t = jnp.swapaxes(mif3[:, :, :2 * M_HEADS], 1, 2)
        hm = _mlstm(proj_a3, mif3, mif_t, m_if_bias[l], m_conv_w[l], m_conv_b[l], m_norm_w[l], B, S)

        lane_pad = ROUTE_LANES - N_GROUPS - N_EXPERTS
        w_r = jnp.pad(jnp.concatenate([w_router_group[l], w_router_expert[l]], axis=1), ((0, 0), (0, lane_pad)))
        b_r = jnp.pad(jnp.concatenate([b_router_group[l], b_router_expert[l]]), (0, lane_pad)).reshape(1, ROUTE_LANES)
        h1, logits = _merge(att.reshape(T, ATT_OUT_WIDTH), hm.reshape(T, M_V_WIDTH), gate, x2, ln_in_g, ln_in_b,
                            w_proj_att[l].astype(BF16), w_proj_mlstm[l].astype(BF16), w_out[l].astype(BF16),
                            ln1_g[l], ln1_b[l], w_r, b_r)

        e_out, rw = _route(logits)
        plan = _dispatch_plan(e_out[:, :TOP_K], T)
        ys = _moe_ffn(h1, plan, w_gate[l], w_up[l], w_down[l])
        h = _ln_out(h1, ys, plan[4], rw, ln2_g[l], ln2_b[l])
    return h.reshape(B, S, D)
```

```python
import functools
import math

import numpy as np
import jax
import jax.numpy as jnp
from jax import lax
from jax.experimental import pallas as pl
from jax.experimental.pallas import tpu as pltpu

F32 = jnp.float32
BF16 = jnp.bfloat16

D_MODEL = 2048
ATT_HEAD_DIM = 128
ATT_HEADS_PER_GROUP = 4
ATT_PATTERNS = ((128, 1), (512, 4), (2048, 16))
ATT_HEADS = ATT_HEADS_PER_GROUP * len(ATT_PATTERNS)
ATT_WIDTH = ATT_HEADS * ATT_HEAD_DIM
ATT_OUT_WIDTH = ATT_HEADS_PER_GROUP * ATT_HEAD_DIM
ATT_BLOCK = 128
ATT_SUPER = 2048

M_HEADS = 4
M_QK_DIM = 128
M_V_DIM = 256
M_QK_WIDTH = M_HEADS * M_QK_DIM
M_V_WIDTH = M_HEADS * M_V_DIM
M_CONV = 4
M_CHUNK = 128
M_MLSTM_COLS = 512

N_BRANCHES = 2
IN_PROJ_SPLITS = (ATT_WIDTH, ATT_WIDTH, ATT_WIDTH, 2 * M_QK_WIDTH, M_V_WIDTH, M_V_WIDTH,
                  2 * M_HEADS, N_BRANCHES * D_MODEL)
COL_AQ = 0
COL_AK = ATT_WIDTH
COL_AV = 2 * ATT_WIDTH
COL_MQK = 3 * ATT_WIDTH
COL_MV = COL_MQK + 2 * M_QK_WIDTH
COL_MO = COL_MV + M_V_WIDTH
COL_MIF = COL_MO + M_V_WIDTH
COL_GATE = COL_MIF + 2 * M_HEADS
PROJ_A_WIDTH = COL_MIF

N_GROUPS = 4
EXPERTS_PER_GROUP = 8
N_EXPERTS = N_GROUPS * EXPERTS_PER_GROUP
TOP_K = 2
D_FF_EXPERT = 1408
MOE_SUB = 128
MOE_SUPER = 1024
MOE_CHUNK = 512
MOE_WHOLE_MAX = 768
MOE_FT = 256
MOE_NFT = D_FF_EXPERT // MOE_FT
MOE_FT_TAIL = D_FF_EXPERT - MOE_NFT * MOE_FT
MOE_NJ = MOE_NFT
MOE_DMA_UNROLL = 32
assert MOE_DMA_UNROLL % 8 == 0 and MOE_SUB % MOE_DMA_UNROLL == 0
assert MOE_FT_TAIL > 0 and D_FF_EXPERT % MOE_FT_TAIL == 0 and MOE_FT_TAIL % 128 == 0
ROUTE_LANES = 128

DEPTH = 1
DEEPNORM_ALPHA = (2 * DEPTH) ** 0.25
LN_EPS = 1e-5
NEG = -1e30

VMEM_LIMIT = 56 * 1024 * 1024


def _alibi_slopes(n):
    def geometric(k):
        start = 2.0 ** (-8.0 / k)
        return [start ** (i + 1) for i in range(k)]
    c = 2 ** int(math.floor(math.log2(n)))
    s = geometric(c) if c == n else geometric(c) + geometric(2 * c)[0::2][: n - c]
    return np.array(sorted(s, reverse=True), dtype=np.float32)


def _params(*sem):
    return pltpu.CompilerParams(dimension_semantics=sem, vmem_limit_bytes=VMEM_LIMIT)


def _layer_norm_rows(z, g, b):
    mu = jnp.mean(z, axis=-1, keepdims=True)
    zc = z - mu
    var = jnp.mean(zc * zc, axis=-1, keepdims=True)
    return zc * lax.rsqrt(var + LN_EPS) * g + b


def _sigmoid(x):
    return 1.0 / (1.0 + jnp.exp(-x))


def _ln_in_kernel(x_ref, g_ref, b_ref, w_ref, hb_ref, mif_ref):
    hb = _layer_norm_rows(x_ref[...], g_ref[...], b_ref[...]).astype(BF16)
    hb_ref[...] = hb
    mif_ref[...] = lax.dot_general(hb, w_ref[...].astype(BF16), (((1,), (1,)), ((), ())),
                                   preferred_element_type=F32)


def _ln_in(x2, g, b, wt, row0, tm=512):
    T, D = x2.shape
    assert row0 % ROUTE_LANES == 0
    return pl.pallas_call(
        _ln_in_kernel,
        grid=(T // tm,),
        in_specs=[pl.BlockSpec((tm, D), lambda i: (i, 0)),
                  pl.BlockSpec((1, D), lambda i: (0, 0)),
                  pl.BlockSpec((1, D), lambda i: (0, 0)),
                  pl.BlockSpec((ROUTE_LANES, D), lambda i: (row0 // ROUTE_LANES, 0))],
        out_specs=[pl.BlockSpec((tm, D), lambda i: (i, 0)),
                   pl.BlockSpec((tm, ROUTE_LANES), lambda i: (i, 0))],
        out_shape=[jax.ShapeDtypeStruct((T, D), BF16), jax.ShapeDtypeStruct((T, ROUTE_LANES), F32)],
        compiler_params=_params("parallel"),
        name="ln_in",
    )(x2, g.reshape(1, D), b.reshape(1, D), wt)


def _mm_nt_kernel(a_ref, w_ref, o_ref, wb_ref):
    @pl.when(pl.program_id(1) == 0)
    def _():
        wb_ref[...] = w_ref[...].astype(BF16)

    o_ref[...] = lax.dot_general(a_ref[...], wb_ref[...], (((1,), (1,)), ((), ())),
                                 preferred_element_type=F32).astype(o_ref.dtype)


def _matmul_nt(a, wt, row0, n_cols, tm, tn, out_dtype, name):
    T, K = a.shape
    if row0 % tn == 0:
        w_spec = pl.BlockSpec((tn, K), lambda j, i: (j + row0 // tn, 0))
    else:
        assert row0 % 8 == 0 and tn % 8 == 0
        w_spec = pl.BlockSpec((pl.Element(tn), pl.Element(K)),
                              lambda j, i: ((row0 // 8 + j * (tn // 8)) * 8, 0))
    return pl.pallas_call(
        _mm_nt_kernel,
        grid=(n_cols // tn, T // tm),
        in_specs=[pl.BlockSpec((tm, K), lambda j, i: (i, 0)), w_spec],
        out_specs=pl.BlockSpec((tm, tn), lambda j, i: (i, j)),
        out_shape=jax.ShapeDtypeStruct((T, n_cols), out_dtype),
        scratch_shapes=[pltpu.VMEM((tn, K), BF16)],
        compiler_params=_params("parallel", "arbitrary"),
        name=name,
    )(a, wt)


ATT_UNROLL = 8


def _batched_loop(n, body):
    u = max(d for d in range(1, ATT_UNROLL + 1) if n % d == 0)
    if n == u:
        body(list(range(n)))
        return

    def step(i, c):
        body([i * u + k for k in range(u)])
        return c
    lax.fori_loop(0, n // u, step, 0)


def _attn_blocks(r, slope_r, prev_bias, q_ref, kc_ref, vc_ref, kp_ref, vp_ref, bases, g, acc_ref, m_ref, l_ref):
    def rows(start):
        return pl.ds(start, ATT_BLOCK, r) if r > 1 else pl.ds(start, ATT_BLOCK)

    dn = (((1,), (1,)), ((), ()))
    scale = ATT_HEAD_DIM ** -0.5
    qi = lax.broadcasted_iota(jnp.int32, (ATT_BLOCK, ATT_BLOCK), 0)
    ki = lax.broadcasted_iota(jnp.int32, (ATT_BLOCK, ATT_BLOCK), 1)
    dlt = (qi - ki).astype(F32)
    alibi_c = -slope_r * dlt
    alibi_p = -slope_r * (dlt + float(ATT_BLOCK)) + prev_bias

    scores = []
    for base, base_prev in bases:
        q = q_ref[0, rows(base), :].astype(BF16)
        kc = kc_ref[0, rows(base), :].astype(BF16)
        kp = kp_ref[0, rows(base_prev), :].astype(BF16)
        sc = lax.dot_general(q, kc, dn, preferred_element_type=F32) * scale + alibi_c
        sp = lax.dot_general(q, kp, dn, preferred_element_type=F32) * scale + alibi_p
        scores.append((jnp.where(ki <= qi, sc, NEG), jnp.where(ki >= qi, sp, NEG)))
    probs = []
    for sc, sp in scores:
        m = jnp.max(jnp.maximum(sc, sp), axis=-1, keepdims=True)
        pc = jnp.exp(sc - m)
        pp = jnp.exp(sp - m)
        l = jnp.sum(pc + pp, axis=-1, keepdims=True)
        probs.append((m, l, pc.astype(BF16), pp.astype(BF16)))
    outs = []
    for (base, base_prev), (m, l, pc, pp) in zip(bases, probs):
        vc = vc_ref[0, rows(base), :].astype(BF16)
        vp = vp_ref[0, rows(base_prev), :].astype(BF16)
        outs.append(jnp.dot(pc, vc, preferred_element_type=F32) + jnp.dot(pp, vp, preferred_element_type=F32))
    for (base, _), (m, l, _, _), acc in zip(bases, probs, outs):
        acc_ref[g, rows(base), :] = acc
        m_ref[g, rows(base), :] = jnp.broadcast_to(m, (ATT_BLOCK, ATT_HEAD_DIM))
        l_ref[g, rows(base), :] = jnp.broadcast_to(l, (ATT_BLOCK, ATT_HEAD_DIM))


def _attn_kernel(slopes_ref, *refs):
    ng = len(ATT_PATTERNS)
    q_refs = refs[0:ng]
    kc_refs = refs[ng:2 * ng]
    vc_refs = refs[2 * ng:3 * ng]
    kp_refs = refs[3 * ng:4 * ng]
    vp_refs = refs[4 * ng:5 * ng]
    o_ref = refs[5 * ng]
    acc_ref, m_ref, l_ref = refs[5 * ng + 1:]
    s = pl.program_id(1)
    h = pl.program_id(2)
    prev_bias = jnp.where(s > 0, 0.0, NEG).astype(F32)

    for g, (window, r) in enumerate(ATT_PATTERNS):
        assert window // r == ATT_BLOCK
        nblk = ATT_SUPER // (ATT_BLOCK * r)
        slope_r = slopes_ref[g, h] * float(r)
        common = dict(r=r, slope_r=slope_r, g=g, acc_ref=acc_ref, m_ref=m_ref, l_ref=l_ref,
                      q_ref=q_refs[g], kc_ref=kc_refs[g], vc_ref=vc_refs[g])

        def first(ps, common=common, g=g):
            _attn_blocks(prev_bias=prev_bias, kp_ref=kp_refs[g], vp_ref=vp_refs[g],
                         bases=[(p, p) for p in ps], **common)
        _batched_loop(r, first)

        if nblk > 1:
            def rest(idxs, common=common, g=g, r=r, nblk=nblk):
                bases = []
                for idx in idxs:
                    p = idx // (nblk - 1)
                    j = idx % (nblk - 1) + 1
                    base = p + j * (ATT_BLOCK * r)
                    bases.append((base, base - ATT_BLOCK * r))
                _attn_blocks(prev_bias=jnp.float32(0.0), kp_ref=kc_refs[g], vp_ref=vc_refs[g],
                             bases=bases, **common)
            _batched_loop(r * (nblk - 1), rest)

    ch = 256
    def merge(i, c):
        rs = pl.ds(pl.multiple_of(i * ch, ch), ch)
        ms = [m_ref[g, rs, :] for g in range(ng)]
        mx = functools.reduce(jnp.maximum, ms)
        num = jnp.zeros((ch, ATT_HEAD_DIM), F32)
        den = jnp.zeros((ch, ATT_HEAD_DIM), F32)
        for g in range(ng):
            w = jnp.exp(ms[g] - mx)
            num = num + w * acc_ref[g, rs, :]
            den = den + w * l_ref[g, rs, :]
        o_ref[0, rs, :] = (num / den).astype(o_ref.dtype)
        return c
    lax.fori_loop(0, ATT_SUPER // ch, merge, 0)


def _attention(proj_a, B, S):
    ng = len(ATT_PATTERNS)
    nsb = S // ATT_SUPER
    cb = ATT_HEAD_DIM
    slopes = jnp.asarray(_alibi_slopes(ATT_HEADS).reshape(ng, ATT_HEADS_PER_GROUP))

    def cur_spec(col0, g):
        return pl.BlockSpec((1, ATT_SUPER, cb),
                            lambda b, s, h, g=g, col0=col0: (b, s, col0 // cb + g * ATT_HEADS_PER_GROUP + h))

    def prev_spec(col0, g):
        rows = ATT_BLOCK * ATT_PATTERNS[g][1]
        per = ATT_SUPER // rows
        return pl.BlockSpec((1, rows, cb),
                            lambda b, s, h, g=g, col0=col0, per=per: (
                                b, jnp.maximum(s * per - 1, 0), col0 // cb + g * ATT_HEADS_PER_GROUP + h))

    in_specs = [pl.BlockSpec(memory_space=pltpu.SMEM)]
    in_specs += [cur_spec(COL_AQ, g) for g in range(ng)]
    in_specs += [cur_spec(COL_AK, g) for g in range(ng)]
    in_specs += [cur_spec(COL_AV, g) for g in range(ng)]
    in_specs += [prev_spec(COL_AK, g) for g in range(ng)]
    in_specs += [prev_spec(COL_AV, g) for g in range(ng)]
    return pl.pallas_call(
        _attn_kernel,
        grid=(B, nsb, ATT_HEADS_PER_GROUP),
        in_specs=in_specs,
        out_specs=pl.BlockSpec((1, ATT_SUPER, cb), lambda b, s, h: (b, s, h)),
        out_shape=jax.ShapeDtypeStruct((B, S, ATT_OUT_WIDTH), BF16),
        scratch_shapes=[pltpu.VMEM((ng, ATT_SUPER, cb), F32)] * 3,
        compiler_params=_params("parallel", "parallel", "parallel"),
        name="dilated_attention",
    )(slopes, *([proj_a] * (5 * ng)))


def _log_sigmoid(x):
    return jnp.minimum(x, 0.0) - jnp.log(1.0 + jnp.exp(-jnp.abs(x)))


def _mlstm_kernel(mq_ref, mk_ref, mva_ref, mvb_ref, moa_ref, mob_ref, mif_ref, mift_ref, ifb_ref, ifbt_ref,
                  cw_ref, cb_ref, nw_ref, o_ref, tail_ref, c_ref, n_ref, m_ref):
    L = M_CHUNK
    NB = mq_ref.shape[0]
    c = pl.program_id(0)
    mv_refs = (mva_ref, mvb_ref)
    mo_refs = (moa_ref, mob_ref)
    hpb = M_MLSTM_COLS // M_V_DIM
    scale = M_QK_DIM ** -0.5
    hp = lax.Precision.HIGHEST

    @pl.when(c == 0)
    def _():
        tail_ref[...] = jnp.zeros_like(tail_ref)
        c_ref[...] = jnp.zeros_like(c_ref)
        n_ref[...] = jnp.zeros_like(n_ref)
        m_ref[...] = jnp.zeros_like(m_ref)

    ti = lax.broadcasted_iota(jnp.int32, (L, L), 0)
    si = lax.broadcasted_iota(jnp.int32, (L, L), 1)
    causal = si <= ti
    tri = causal.astype(F32)

    def conv_act(x_ref, bb, part, out_scale):
        cols = slice(part * M_QK_WIDTH, (part + 1) * M_QK_WIDTH)
        x = x_ref[bb]
        xx = jnp.concatenate([tail_ref[bb, :, cols], x], axis=0)
        y = cb_ref[:, cols]
        for j in range(M_CONV):
            off = 8 - (M_CONV - 1) + j
            y = y + cw_ref[j:j + 1, cols] * xx[off:off + L, :]
        tail_ref[bb, :, cols] = x[L - 8:, :]
        act = y * _sigmoid(y)
        return (act if out_scale == 1.0 else act * out_scale).astype(BF16)

    per_b = []
    for bb in range(NB):
        q_act = conv_act(mq_ref, bb, 0, scale)
        k_act = conv_act(mk_ref, bb, 1, 1.0)
        gi_c = mif_ref[bb] + ifb_ref[...]
        gi_r = mift_ref[bb] + ifbt_ref[...]
        bcum_c = jnp.dot(tri, _log_sigmoid(gi_c), precision=hp, preferred_element_type=F32)
        bcum_r = lax.dot_general(_log_sigmoid(gi_r), tri, (((1,), (1,)), ((), ())), precision=hp,
                                 preferred_element_type=F32)
        per_b.append((q_act, k_act, gi_c, gi_r, bcum_c, bcum_r))

    chains = [(bb, hd) for bb in range(NB) for hd in range(M_HEADS)]

    ph1 = []
    for bb, hd in chains:
        q_act, k_act, gi_c, gi_r, bcum_c, bcum_r = per_b[bb]
        st = bb * M_HEADS + hd
        q = q_act[:, hd * M_QK_DIM:(hd + 1) * M_QK_DIM]
        k = k_act[:, hd * M_QK_DIM:(hd + 1) * M_QK_DIM]
        b_c = bcum_c[:, M_HEADS + hd:M_HEADS + hd + 1]
        i_c = gi_c[:, hd:hd + 1]
        b_r = bcum_r[M_HEADS + hd:M_HEADS + hd + 1, :]
        i_r = gi_r[hd:hd + 1, :]
        m_prev = m_ref[st]
        dmat = jnp.where(causal, b_c + (i_r - b_r), NEG)
        inter = b_c + m_prev
        m_t = jnp.maximum(inter, jnp.max(dmat, axis=-1, keepdims=True))
        w_intra = jnp.exp(dmat - m_t)
        w_inter = jnp.exp(inter - m_t)
        qk = lax.dot_general(q, k, (((1,), (1,)), ((), ())), preferred_element_type=F32) * w_intra
        ph1.append((q, k, b_c, i_c, m_prev, m_t, w_inter, qk))

    ph2 = []
    for (bb, hd), (q, k, b_c, i_c, m_prev, m_t, w_inter, qk) in zip(chains, ph1):
        st = bb * M_HEADS + hd
        vcols = slice((hd % hpb) * M_V_DIM, (hd % hpb + 1) * M_V_DIM)
        v = mv_refs[hd // hpb][bb, :, vcols].astype(BF16)
        c_prev = c_ref[st]
        n_prev = n_ref[st]
        num = w_inter * jnp.dot(q, c_prev.astype(BF16), preferred_element_type=F32) \
            + jnp.dot(qk.astype(BF16), v, preferred_element_type=F32)
        den = w_inter * jnp.sum(q.astype(F32) * n_prev, axis=-1, keepdims=True) \
            + jnp.sum(qk, axis=-1, keepdims=True)
        hh = num / jnp.maximum(jnp.abs(den), jnp.exp(-m_t))
        ph2.append((v, c_prev, n_prev, hh))

    for (bb, hd), (q, k, b_c, i_c, m_prev, m_t, w_inter, qk), (v, c_prev, n_prev, hh) in zip(chains, ph1, ph2):
        st = bb * M_HEADS + hd
        b_last = b_c[L - 1:L, :]
        w_log = b_last - b_c + i_c
        m_new = jnp.maximum(b_last + m_prev, jnp.max(w_log, axis=0, keepdims=True))
        wk = jnp.exp(w_log - m_new)
        decay = jnp.exp(b_last + m_prev - m_new)
        kw = (k.astype(F32) * wk)
        c_ref[st] = decay * c_prev + lax.dot_general(kw.astype(BF16), v, (((0,), (0,)), ((), ())),
                                                     preferred_element_type=F32)
        n_ref[st] = decay * n_prev + jnp.sum(kw, axis=0, keepdims=True)
        m_ref[st] = m_new

    for (bb, hd), (v, c_prev, n_prev, hh) in zip(chains, ph2):
        vcols = slice((hd % hpb) * M_V_DIM, (hd % hpb + 1) * M_V_DIM)
        mu = jnp.mean(hh, axis=-1, keepdims=True)
        hc = hh - mu
        var = jnp.mean(hc * hc, axis=-1, keepdims=True)
        hn = hc * lax.rsqrt(var + LN_EPS) * nw_ref[:, hd * M_V_DIM:(hd + 1) * M_V_DIM]
        og = _sigmoid(mo_refs[hd // hpb][bb, :, vcols])
        o_ref[bb, :, hd * M_V_DIM:(hd + 1) * M_V_DIM] = (hn * og).astype(o_ref.dtype)


def _mlstm(proj_a, mif, mif_t, if_bias, conv_w, conv_b, norm_w, B, S):
    L = M_CHUNK
    W = 2 * M_QK_WIDTH
    cw = M_MLSTM_COLS
    ifb = jnp.zeros((1, ROUTE_LANES), F32).at[0, :2 * M_HEADS].set(if_bias)
    ifbt = jnp.broadcast_to(if_bias.reshape(2 * M_HEADS, 1), (2 * M_HEADS, L))

    def col_spec(col0):
        assert col0 % cw == 0
        return pl.BlockSpec((B, L, cw), lambda c, col0=col0: (0, c, col0 // cw))

    const = lambda c: (0, 0)
    return pl.pallas_call(
        _mlstm_kernel,
        grid=(S // L,),
        in_specs=[col_spec(COL_MQK), col_spec(COL_MQK + M_QK_WIDTH),
                  col_spec(COL_MV), col_spec(COL_MV + cw),
                  col_spec(COL_MO), col_spec(COL_MO + cw),
                  pl.BlockSpec((B, L, ROUTE_LANES), lambda c: (0, c, 0)),
                  pl.BlockSpec((B, 2 * M_HEADS, L), lambda c: (0, 0, c)),
                  pl.BlockSpec((1, ROUTE_LANES), const),
                  pl.BlockSpec((2 * M_HEADS, L), const),
                  pl.BlockSpec((M_CONV, W), const),
                  pl.BlockSpec((1, W), const),
                  pl.BlockSpec((1, M_V_WIDTH), const)],
        out_specs=pl.BlockSpec((B, L, M_V_WIDTH), lambda c: (0, c, 0)),
        out_shape=jax.ShapeDtypeStruct((B, S, M_V_WIDTH), BF16),
        scratch_shapes=[pltpu.VMEM((B, 8, W), F32),
                        pltpu.VMEM((B * M_HEADS, M_QK_DIM, M_V_DIM), F32),
                        pltpu.VMEM((B * M_HEADS, 1, M_QK_DIM), F32),
                        pltpu.VMEM((B * M_HEADS, 1, 1), F32)],
        compiler_params=_params("arbitrary"),
        name="mlstm",
    )(proj_a, proj_a, proj_a, proj_a, proj_a, proj_a, mif, mif_t, ifb, ifbt, conv_w, conv_b.reshape(1, W),
      norm_w.reshape(1, M_V_WIDTH))


def _merge_kernel(att_ref, hm_ref, gate_ref, x_ref, g0_ref, b0_ref, wpa_ref, wpm_ref, wo_ref, g_ref, b_ref,
                  wrh_ref, wrl_ref, br_ref, h1_ref, lg_ref):
    halves = [pl.ds(k * MERGE_HALF, MERGE_HALF) for k in range(att_ref.shape[0] // MERGE_HALF)]
    proj = []
    for rs in halves:
        pa = jnp.dot(att_ref[rs, :], wpa_ref[...], preferred_element_type=F32)
        pm = jnp.dot(hm_ref[rs, :], wpm_ref[...], preferred_element_type=F32)
        proj.append((pa, pm))
    ys = []
    for rs, (pa, pm) in zip(halves, proj):
        ga = _sigmoid(gate_ref[rs, :D_MODEL].astype(F32))
        gm = _sigmoid(gate_ref[rs, D_MODEL:].astype(F32))
        merged = (ga * pa + gm * pm).astype(BF16)
        ys.append(jnp.dot(merged, wo_ref[...], preferred_element_type=F32))
    for rs, y in zip(halves, ys):
        h = _layer_norm_rows(x_ref[rs, :], g0_ref[...], b0_ref[...])
        h1 = _layer_norm_rows(DEEPNORM_ALPHA * h + y, g_ref[...], b_ref[...])
        h1_ref[rs, :] = h1
        h1h = h1.astype(BF16)
        h1l = (h1 - h1h.astype(F32)).astype(BF16)
        lg_ref[rs, :] = (jnp.dot(h1h, wrh_ref[...], preferred_element_type=F32)
                         + jnp.dot(h1l, wrh_ref[...], preferred_element_type=F32)
                         + jnp.dot(h1h, wrl_ref[...], preferred_element_type=F32)) + br_ref[...]


MERGE_HALF = 256


def _merge(att, hm, gate, x2, g0, b0, wpa, wpm, wo, g1, b1, wr, br, tm=512):
    T, D = x2.shape
    const = lambda i: (0, 0)
    one = pl.Buffered(1)
    wrh = wr.astype(BF16)
    wrl = (wr - wrh.astype(F32)).astype(BF16)
    return pl.pallas_call(
        _merge_kernel,
        grid=(T // tm,),
        in_specs=[pl.BlockSpec((tm, ATT_OUT_WIDTH), lambda i: (i, 0)),
                  pl.BlockSpec((tm, M_V_WIDTH), lambda i: (i, 0)),
                  pl.BlockSpec((tm, N_BRANCHES * D), lambda i: (i, 0)),
                  pl.BlockSpec((tm, D), lambda i: (i, 0)),
                  pl.BlockSpec((1, D), const),
                  pl.BlockSpec((1, D), const),
                  pl.BlockSpec((ATT_OUT_WIDTH, D), const, pipeline_mode=one),
                  pl.BlockSpec((M_V_WIDTH, D), const, pipeline_mode=one),
                  pl.BlockSpec((D, D), const, pipeline_mode=one),
                  pl.BlockSpec((1, D), const),
                  pl.BlockSpec((1, D), const),
                  pl.BlockSpec((D, ROUTE_LANES), const, pipeline_mode=one),
                  pl.BlockSpec((D, ROUTE_LANES), const, pipeline_mode=one),
                  pl.BlockSpec((1, ROUTE_LANES), const)],
        out_specs=[pl.BlockSpec((tm, D), lambda i: (i, 0)),
                   pl.BlockSpec((tm, ROUTE_LANES), lambda i: (i, 0))],
        out_shape=[jax.ShapeDtypeStruct((T, D), F32), jax.ShapeDtypeStruct((T, ROUTE_LANES), F32)],
        compiler_params=_params("parallel"),
        name="merge_out_ln1",
    )(att, hm, gate, x2, g0.reshape(1, D), b0.reshape(1, D), wpa, wpm, wo, g1.reshape(1, D), b1.reshape(1, D),
      wrh, wrl, br)


def _route_kernel(lg_ref, e_ref, w_ref):
    lg = lg_ref[...]
    col = lax.broadcasted_iota(jnp.int32, lg.shape, 1)
    big = jnp.int32(ROUTE_LANES)

    def first_argmax(v, vmax):
        return jnp.min(jnp.where(v == vmax, col, big), axis=-1, keepdims=True)

    gl = jnp.where(col < N_GROUPS, lg, NEG)
    gmax = jnp.max(gl, axis=-1, keepdims=True)
    grp = first_argmax(gl, gmax)
    gsum = jnp.sum(jnp.where(col < N_GROUPS, jnp.exp(lg - gmax), 0.0), axis=-1, keepdims=True)
    g_w = 1.0 / gsum
    ecol = col - N_GROUPS
    egrp = lax.shift_right_arithmetic(ecol, int(math.log2(EXPERTS_PER_GROUP)))
    in_grp = (ecol >= 0) & (ecol < N_EXPERTS) & (egrp == grp)
    el = jnp.where(in_grp, lg, NEG)
    v1 = jnp.max(el, axis=-1, keepdims=True)
    i1 = first_argmax(el, v1)
    el2 = jnp.where(col == i1, NEG, el)
    v2 = jnp.max(el2, axis=-1, keepdims=True)
    i2 = first_argmax(el2, v2)
    t = jnp.exp(v2 - v1)
    p1 = 1.0 / (1.0 + t)
    p2 = t / (1.0 + t)
    e_ref[...] = jnp.where(col == 0, i1 - N_GROUPS, jnp.where(col == 1, i2 - N_GROUPS, 0))
    w_ref[...] = jnp.where(col == 0, g_w * p1, jnp.where(col == 1, g_w * p2, 0.0))


def _route(logits, tm=1024):
    T = logits.shape[0]
    spec = pl.BlockSpec((tm, ROUTE_LANES), lambda i: (i, 0))
    return pl.pallas_call(
        _route_kernel,
        grid=(T // tm,),
        in_specs=[spec],
        out_specs=[spec, spec],
        out_shape=[jax.ShapeDtypeStruct((T, ROUTE_LANES), jnp.int32),
                   jax.ShapeDtypeStruct((T, ROUTE_LANES), F32)],
        compiler_params=_params("parallel"),
        name="route",
    )(logits)


def _dispatch_plan(e_tk, T):
    M = T * TOP_K
    e_flat = e_tk.reshape(M)
    onehot = (e_flat[:, None] == jnp.arange(N_EXPERTS, dtype=jnp.int32)[None, :]).astype(jnp.int32)
    csum = jnp.cumsum(onehot, axis=0)
    counts = csum[-1]
    rank = jnp.sum((csum - onehot) * onehot, axis=1)
    padded = (counts + MOE_SUB - 1) // MOE_SUB * MOE_SUB
    pstart = jnp.cumsum(padded) - padded
    dest = jnp.sum(onehot * pstart[None, :], axis=1) + rank

    nsb_max = N_EXPERTS + M // MOE_SUPER
    nsb_e = (padded + MOE_SUPER - 1) // MOE_SUPER
    sb_end = jnp.cumsum(nsb_e)
    sb_beg = sb_end - nsb_e
    total = sb_end[-1]
    sb = jnp.arange(nsb_max, dtype=jnp.int32)
    sb_c = jnp.minimum(sb, total - 1)
    ex = jnp.sum((sb_end[None, :] <= sb_c[:, None]).astype(jnp.int32), axis=1)
    local = sb_c - sb_beg[ex]
    row0 = pstart[ex] + local * MOE_SUPER
    active = sb < total
    cnt = jnp.where(active, jnp.clip(counts[ex] - local * MOE_SUPER, 0, MOE_SUPER), 0)
    nsub = jnp.where(active, jnp.clip(padded[ex] - local * MOE_SUPER, 0, MOE_SUPER) // MOE_SUB, 0)
    return (ex.astype(jnp.int32), row0.astype(jnp.int32), cnt.astype(jnp.int32), nsub.astype(jnp.int32),
            dest.astype(jnp.int32), jnp.sum(padded).astype(jnp.int32).reshape(1))


def _moe_kernel(sb_ex, sb_row0, sb_cnt, sb_nsub, dest, used_rows,
                h1_hbm, wga_ref, wua_ref, wda_ref, wgb_ref, wub_ref, wdb_ref, ys_hbm,
                stage_buf, acc_buf, slot_tok, gsem, ssem):
    b = pl.program_id(0)
    j = pl.program_id(1)
    nb = pl.num_programs(0)
    nsub = sb_nsub[b]
    cnt = sb_cnt[b]
    slot = lax.rem(b, 2)
    U = MOE_DMA_UNROLL

    def gather_batches(bb):
        return (sb_cnt[bb] + (U - 1)) // U

    def gather_issue(bb):
        r0 = sb_row0[bb]

        def pad_row(i, c):
            slot_tok[r0 + i] = 0
            return c
        lax.fori_loop(sb_cnt[bb], gather_batches(bb) * U, pad_row, 0)

        def issue(q, c):
            i0 = pl.multiple_of(q * U, U)
            for k in range(U):
                tok = slot_tok[r0 + i0 + k]
                pltpu.make_async_copy(h1_hbm.at[lax.shift_right_logical(tok, 3), pl.ds(tok & 7, 1), :],
                                      stage_buf.at[half, q * (U // 8) + k // 8, pl.ds(k % 8, 1), :], gsem).start()
            return c
        half = lax.rem(bb, 2)
        lax.fori_loop(0, gather_batches(bb), issue, 0)

    def gather_wait(bb):
        def wait(k, c):
            pltpu.make_async_copy(h1_hbm.at[pl.ds(0, U // 8)], stage_buf.at[0, pl.ds(0, U // 8)], gsem).wait()
            return c
        lax.fori_loop(0, gather_batches(bb), wait, 0)

    def build_tables():
        n_asg = dest.shape[0]
        step = 16

        def fill(q, c):
            for k in range(step):
                slot_tok[dest[q * step + k]] = q * (step // TOP_K) + k // TOP_K
            return c
        lax.fori_loop(0, n_asg // step, fill, 0)

    def out_copy(k, r0):
        rs = pl.ds(pl.multiple_of(k * MOE_SUB, MOE_SUB), MOE_SUB)
        return pltpu.make_async_copy(acc_buf.at[slot, rs, :],
                                     ys_hbm.at[pl.ds(pl.multiple_of(r0 + k * MOE_SUB, MOE_SUB), MOE_SUB), :], ssem)

    def out_issue():
        r0 = sb_row0[b]

        def issue(k, c):
            out_copy(k, r0).start()
            return c
        lax.fori_loop(0, nsub, issue, 0)

    def out_wait(n_sub):
        def wait(k, c):
            out_copy(0, 0).wait()
            return c
        lax.fori_loop(0, n_sub, wait, 0)

    @pl.when(j == 0)
    def _first_step():
        @pl.when(b == 0)
        def _():
            build_tables()

            def clear(q, c):
                for half in range(2):
                    stage_buf[half, q] = jnp.zeros((8, D_MODEL), F32)
                return c
            lax.fori_loop(0, stage_buf.shape[1], clear, 0)
            gather_issue(0)

        @pl.when(nsub > 0)
        def _():
            gather_wait(b)

            def clear_acc(k, c):
                rs = pl.ds(pl.multiple_of(k * MOE_SUB, MOE_SUB), MOE_SUB)
                acc_buf[slot, rs, :] = jnp.zeros((MOE_SUB, D_MODEL), F32)
                return c
            lax.fori_loop(0, nsub, clear_acc, 0)

        nxt = jnp.minimum(b + 1, nb - 1)

        @pl.when((b + 1 < nb) & (sb_nsub[nxt] > 0))
        def _():
            gather_issue(nxt)

    def ffn_tile(wg_ref, wu_ref, wd_ref):
        def chunk(r0, rows):
            rs = pl.ds(r0, rows)
            t0 = r0 // 8 if isinstance(r0, int) else pl.multiple_of(r0 // 8, MOE_SUB // 8)
            tiles = pl.ds(t0, rows // 8)
            x = stage_buf[slot, tiles].reshape(rows, D_MODEL).astype(BF16)
            gt = jnp.dot(x, wg_ref[0].astype(BF16), preferred_element_type=F32)
            ut = jnp.dot(x, wu_ref[0].astype(BF16), preferred_element_type=F32)
            hmid = (gt * _sigmoid(gt) * ut).astype(BF16)
            acc_buf[slot, rs, :] += jnp.dot(hmid, wd_ref[0].astype(BF16), preferred_element_type=F32)

        whole = [n for n in range(MOE_CHUNK // MOE_SUB + 1, MOE_WHOLE_MAX // MOE_SUB + 1)]
        is_whole = functools.reduce(jnp.logical_or, [nsub == n for n in whole])
        for n in whole:
            @pl.when(nsub == n)
            def _(n=n):
                chunk(0, n * MOE_SUB)

        @pl.when(jnp.logical_not(is_whole))
        def _():
            per = MOE_CHUNK // MOE_SUB
            nfull = nsub // per

            def full(k, c):
                chunk(pl.multiple_of(k * MOE_CHUNK, MOE_CHUNK), MOE_CHUNK)
                return c
            lax.fori_loop(0, nfull, full, 0)
            rem = nsub - nfull * per
            base = nfull * MOE_CHUNK
            size = MOE_CHUNK // 2
            while size >= MOE_SUB:
                units = size // MOE_SUB

                @pl.when(lax.rem(rem, 2 * units) >= units)
                def _(size=size, units=units):
                    skipped = (rem // (2 * units)) * (2 * units)
                    chunk(pl.multiple_of(base + skipped * MOE_SUB, size), size)
                size //= 2

    @pl.when(nsub > 0)
    def _tiles():
        ffn_tile(wga_ref, wua_ref, wda_ref)

        @pl.when(j == MOE_NJ - 1)
        def _():
            ffn_tile(wgb_ref, wub_ref, wdb_ref)

    @pl.when(j == MOE_NJ - 1)
    def _last_step():
        prev = jnp.maximum(b - 1, 0)

        @pl.when((b > 0) & (sb_nsub[prev] > 0))
        def _():
            out_wait(sb_nsub[prev])

        @pl.when(nsub > 0)
        def _():
            out_issue()

        @pl.when((b == nb - 1) & (nsub > 0))
        def _():
            out_wait(nsub)

        @pl.when(b == nb - 1)
        def _():
            acc_buf[0, pl.ds(0, MOE_SUB), :] = jnp.zeros((MOE_SUB, D_MODEL), F32)

            def fill(k, c):
                rows = pl.ds(pl.multiple_of(k * MOE_SUB, MOE_SUB), MOE_SUB)
                pltpu.make_async_copy(acc_buf.at[0, pl.ds(0, MOE_SUB), :], ys_hbm.at[rows, :], ssem).start()
                return c
            first = used_rows[0] // MOE_SUB
            total = ys_hbm.shape[0] // MOE_SUB
            lax.fori_loop(first, total, fill, 0)

            def drain(k, c):
                out_copy(0, 0).wait()
                return c
            lax.fori_loop(first, total, drain, 0)


def _moe_ffn(h1, plan, w_gate, w_up, w_down):
    T, D = h1.shape
    sb_ex, sb_row0, sb_cnt, sb_nsub, dest, used_rows = plan
    nsb_max = sb_ex.shape[0]
    n_rows = dest.shape[0] + N_EXPERTS * MOE_SUB
    last = MOE_NFT - 1
    tail = D_FF_EXPERT // MOE_FT_TAIL - 1

    def ja(b, j, nsub):
        return jnp.where(nsub[b] > 0, jnp.minimum(j, last), last)

    def main_cols(b, j, ex, r0, ct, ns, ds, ur):
        return (ex[b], 0, ja(b, j, ns))

    def main_rows(b, j, ex, r0, ct, ns, ds, ur):
        return (ex[b], ja(b, j, ns), 0)

    def tail_cols(b, j, ex, r0, ct, ns, ds, ur):
        return (ex[b], 0, tail)

    def tail_rows(b, j, ex, r0, ct, ns, ds, ur):
        return (ex[b], tail, 0)

    grid_spec = pltpu.PrefetchScalarGridSpec(
        num_scalar_prefetch=6,
        grid=(nsb_max, MOE_NJ),
        in_specs=[pl.BlockSpec(memory_space=pl.ANY),
                  pl.BlockSpec((1, D, MOE_FT), main_cols),
                  pl.BlockSpec((1, D, MOE_FT), main_cols),
                  pl.BlockSpec((1, MOE_FT, D), main_rows),
                  pl.BlockSpec((1, D, MOE_FT_TAIL), tail_cols),
                  pl.BlockSpec((1, D, MOE_FT_TAIL), tail_cols),
                  pl.BlockSpec((1, MOE_FT_TAIL, D), tail_rows)],
        out_specs=pl.BlockSpec(memory_space=pl.ANY),
        scratch_shapes=[pltpu.VMEM((2, MOE_SUPER // 8, 8, D), F32),
                        pltpu.VMEM((2, MOE_SUPER, D), F32),
                        pltpu.SMEM((n_rows,), jnp.int32),
                        pltpu.SemaphoreType.DMA(()),
                        pltpu.SemaphoreType.DMA(())],
    )
    return pl.pallas_call(
        _moe_kernel,
        grid_spec=grid_spec,
        out_shape=jax.ShapeDtypeStruct((n_rows, D), F32),
        compiler_params=_params("arbitrary", "arbitrary"),
        name="moe_experts",
    )(sb_ex, sb_row0, sb_cnt, sb_nsub, dest, used_rows, h1.reshape(T // 8, 8, D),
      w_gate, w_up, w_down, w_gate, w_up, w_down)


LN_OUT_ROWS = 512
LN_OUT_BATCH = 32
assert LN_OUT_BATCH % (8 * TOP_K) == 0 and (LN_OUT_ROWS * TOP_K) % LN_OUT_BATCH == 0


def _ln_out_kernel(dest, h1_ref, rw_ref, g_ref, b_ref, ys_hbm, o_ref, y_buf, sem):
    i = pl.program_id(0)
    nt = pl.num_programs(0)
    tm = LN_OUT_ROWS
    toks = LN_OUT_BATCH // TOP_K

    def gather_issue(tile, buf):
        base = tile * (tm * TOP_K)

        def issue(q, c):
            for k in range(LN_OUT_BATCH):
                d = dest[base + q * LN_OUT_BATCH + k]
                t = k // TOP_K
                pltpu.make_async_copy(ys_hbm.at[lax.shift_right_logical(d, 3), pl.ds(d & 7, 1), :],
                                      y_buf.at[buf, k % TOP_K, q * (toks // 8) + t // 8, pl.ds(t % 8, 1), :],
                                      sem.at[buf]).start()
            return c
        lax.fori_loop(0, tm * TOP_K // LN_OUT_BATCH, issue, 0)

    def gather_wait(buf):
        for s in range(TOP_K):
            pltpu.make_async_copy(ys_hbm.at[pl.ds(0, tm // 8)], y_buf.at[buf, s], sem.at[buf]).wait()

    @pl.when(i == 0)
    def _():
        gather_issue(0, 0)

    @pl.when(i + 1 < nt)
    def _():
        gather_issue(i + 1, lax.rem(i + 1, 2))

    cur = lax.rem(i, 2)
    gather_wait(cur)
    rw = rw_ref[...]
    z = DEEPNORM_ALPHA * h1_ref[...]
    for s in range(TOP_K):
        z = z + rw[:, s:s + 1] * y_buf[cur, s].reshape(tm, D_MODEL)
    o_ref[...] = _layer_norm_rows(z, g_ref[...], b_ref[...])


def _ln_out(h1, ys, dest, rw, g, b):
    T, D = h1.shape
    tm = LN_OUT_ROWS
    n_rows = ys.shape[0]
    grid_spec = pltpu.PrefetchScalarGridSpec(
        num_scalar_prefetch=1,
        grid=(T // tm,),
        in_specs=[pl.BlockSpec((tm, D), lambda i, ds: (i, 0)),
                  pl.BlockSpec((tm, ROUTE_LANES), lambda i, ds: (i, 0)),
                  pl.BlockSpec((1, D), lambda i, ds: (0, 0)),
                  pl.BlockSpec((1, D), lambda i, ds: (0, 0)),
                  pl.BlockSpec(memory_space=pl.ANY)],
        out_specs=pl.BlockSpec((tm, D), lambda i, ds: (i, 0)),
        scratch_shapes=[pltpu.VMEM((2, TOP_K, tm // 8, 8, D), F32),
                        pltpu.SemaphoreType.DMA((2,))],
    )
    return pl.pallas_call(
        _ln_out_kernel,
        grid_spec=grid_spec,
        out_shape=jax.ShapeDtypeStruct((T, D), F32),
        compiler_params=_params("arbitrary"),
        name="combine_ln2",
    )(dest, h1, rw, g.reshape(1, D), b.reshape(1, D), ys.reshape(n_rows // 8, 8, D))


def kernel(x, ln_in_g, ln_in_b, w_in, m_conv_w, m_conv_b, m_if_bias, m_norm_w, w_proj_att, w_proj_mlstm, w_out,
           ln1_g, ln1_b, w_router_group, b_router_group, w_router_expert, b_router_expert, w_gate, w_up, w_down,
           ln2_g, ln2_b):
    B, S, D = x.shape
    T = B * S
    assert D == D_MODEL and S % ATT_SUPER == 0 and w_in.shape[0] == DEPTH == 1

    x2 = x.reshape(T, D)
    for l in range(DEPTH):
        wt = jnp.swapaxes(w_in[l], 0, 1)
        hb, mif = _ln_in(x2, ln_in_g, ln_in_b, wt, COL_MIF)
        proj_a = _matmul_nt(hb, wt, 0, PROJ_A_WIDTH, 1024, 1280, F32, "in_proj_a")
        gate = _matmul_nt(hb, wt, COL_GATE, N_BRANCHES * D, 1024, 1024, BF16, "in_proj_gate")

        proj_a3 = proj_a.reshape(B, S, PROJ_A_WIDTH)
        att = _attention(proj_a3, B, S)
        mif3 = mif.reshape(B, S, ROUTE_LANES)
        mif_t = jnp.swapaxes(mif3[:, :, :2 * M_HEADS], 1, 2)
        hm = _mlstm(proj_a3, mif3, mif_t, m_if_bias[l], m_conv_w[l], m_conv_b[l], m_norm_w[l], B, S)

        lane_pad = ROUTE_LANES - N_GROUPS - N_EXPERTS
        w_r = jnp.pad(jnp.concatenate([w_router_group[l], w_router_expert[l]], axis=1), ((0, 0), (0, lane_pad)))
        b_r = jnp.pad(jnp.concatenate([b_router_group[l], b_router_expert[l]]), (0, lane_pad)).reshape(1, ROUTE_LANES)
        h1, logits = _merge(att.reshape(T, ATT_OUT_WIDTH), hm.reshape(T, M_V_WIDTH), gate, x2, ln_in_g, ln_in_b,
                            w_proj_att[l].astype(BF16), w_proj_mlstm[l].astype(BF16), w_out[l].astype(BF16),
                            ln1_g[l], ln1_b[l], w_r, b_r)

        e_out, rw = _route(logits)
        plan = _dispatch_plan(e_out[:, :TOP_K], T)
        ys = _moe_ffn(h1, plan, w_gate[l], w_up[l], w_down[l])
        h = _ln_out(h1, ys, plan[4], rw, ln2_g[l], ln2_b[l])
    return h.reshape(B, S, D)
```

```python
import functools
import math

import numpy as np
import jax
import jax.numpy as jnp
from jax import lax
from jax.experimental import pallas as pl
from jax.experimental.pallas import tpu as pltpu

F32 = jnp.float32
BF16 = jnp.bfloat16

D_MODEL = 2048
ATT_HEAD_DIM = 128
ATT_HEADS_PER_GROUP = 4
ATT_PATTERNS = ((128, 1), (512, 4), (2048, 16))
ATT_HEADS = ATT_HEADS_PER_GROUP * len(ATT_PATTERNS)
ATT_WIDTH = ATT_HEADS * ATT_HEAD_DIM
ATT_OUT_WIDTH = ATT_HEADS_PER_GROUP * ATT_HEAD_DIM
ATT_BLOCK = 128
ATT_SUPER = 2048

M_HEADS = 4
M_QK_DIM = 128
M_V_DIM = 256
M_QK_WIDTH = M_HEADS * M_QK_DIM
M_V_WIDTH = M_HEADS * M_V_DIM
M_CONV = 4
M_CHUNK = 128
M_MLSTM_COLS = 512

N_BRANCHES = 2
IN_PROJ_SPLITS = (ATT_WIDTH, ATT_WIDTH, ATT_WIDTH, 2 * M_QK_WIDTH, M_V_WIDTH, M_V_WIDTH,
                  2 * M_HEADS, N_BRANCHES * D_MODEL)
COL_AQ = 0
COL_AK = ATT_WIDTH
COL_AV = 2 * ATT_WIDTH
COL_MQK = 3 * ATT_WIDTH
COL_MV = COL_MQK + 2 * M_QK_WIDTH
COL_MO = COL_MV + M_V_WIDTH
COL_MIF = COL_MO + M_V_WIDTH
COL_GATE = COL_MIF + 2 * M_HEADS
PROJ_A_WIDTH = COL_MIF

N_GROUPS = 4
EXPERTS_PER_GROUP = 8
N_EXPERTS = N_GROUPS * EXPERTS_PER_GROUP
TOP_K = 2
D_FF_EXPERT = 1408
MOE_SUB = 128
MOE_SUPER = 768
MOE_CHUNK = 512
MOE_WHOLE_MAX = 768
MOE_FT = 512
MOE_NFT = D_FF_EXPERT // MOE_FT
MOE_FT_TAIL = D_FF_EXPERT - MOE_NFT * MOE_FT
MOE_NJ = MOE_NFT
MOE_DMA_UNROLL = 32
assert MOE_DMA_UNROLL % 8 == 0 and MOE_SUB % MOE_DMA_UNROLL == 0
assert MOE_FT_TAIL > 0 and MOE_FT_TAIL % 128 == 0 and MOE_WHOLE_MAX <= MOE_SUPER
ROUTE_LANES = 128

DEPTH = 1
DEEPNORM_ALPHA = (2 * DEPTH) ** 0.25
LN_EPS = 1e-5
NEG = -1e30

VMEM_LIMIT = 56 * 1024 * 1024


def _alibi_slopes(n):
    def geometric(k):
        start = 2.0 ** (-8.0 / k)
        return [start ** (i + 1) for i in range(k)]
    c = 2 ** int(math.floor(math.log2(n)))
    s = geometric(c) if c == n else geometric(c) + geometric(2 * c)[0::2][: n - c]
    return np.array(sorted(s, reverse=True), dtype=np.float32)


def _params(*sem):
    return pltpu.CompilerParams(dimension_semantics=sem, vmem_limit_bytes=VMEM_LIMIT)


def _layer_norm_rows(z, g, b):
    mu = jnp.mean(z, axis=-1, keepdims=True)
    zc = z - mu
    var = jnp.mean(zc * zc, axis=-1, keepdims=True)
    return zc * lax.rsqrt(var + LN_EPS) * g + b


def _sigmoid(x):
    return 1.0 / (1.0 + jnp.exp(-x))


def _ln_in_kernel(x_ref, g_ref, b_ref, w_ref, hb_ref, mif_ref):
    hb = _layer_norm_rows(x_ref[...], g_ref[...], b_ref[...]).astype(BF16)
    hb_ref[...] = hb
    mif_ref[...] = lax.dot_general(hb, w_ref[...].astype(BF16), (((1,), (1,)), ((), ())),
                                   preferred_element_type=F32)


def _ln_in(x2, g, b, wt, row0, tm=512):
    T, D = x2.shape
    assert row0 % ROUTE_LANES == 0
    return pl.pallas_call(
        _ln_in_kernel,
        grid=(T // tm,),
        in_specs=[pl.BlockSpec((tm, D), lambda i: (i, 0)),
                  pl.BlockSpec((1, D), lambda i: (0, 0)),
                  pl.BlockSpec((1, D), lambda i: (0, 0)),
                  pl.BlockSpec((ROUTE_LANES, D), lambda i: (row0 // ROUTE_LANES, 0))],
        out_specs=[pl.BlockSpec((tm, D), lambda i: (i, 0)),
                   pl.BlockSpec((tm, ROUTE_LANES), lambda i: (i, 0))],
        out_shape=[jax.ShapeDtypeStruct((T, D), BF16), jax.ShapeDtypeStruct((T, ROUTE_LANES), F32)],
        compiler_params=_params("parallel"),
        name="ln_in",
    )(x2, g.reshape(1, D), b.reshape(1, D), wt)


def _mm_nt_kernel(a_ref, w_ref, o_ref, wb_ref):
    @pl.when(pl.program_id(1) == 0)
    def _():
        wb_ref[...] = w_ref[...].astype(BF16)

    o_ref[...] = lax.dot_general(a_ref[...], wb_ref[...], (((1,), (1,)), ((), ())),
                                 preferred_element_type=F32).astype(o_ref.dtype)


def _matmul_nt(a, wt, row0, n_cols, tm, tn, out_dtype, name):
    T, K = a.shape
    if row0 % tn == 0:
        w_spec = pl.BlockSpec((tn, K), lambda j, i: (j + row0 // tn, 0))
    else:
        assert row0 % 8 == 0 and tn % 8 == 0
        w_spec = pl.BlockSpec((pl.Element(tn), pl.Element(K)),
                              lambda j, i: ((row0 // 8 + j * (tn // 8)) * 8, 0))
    return pl.pallas_call(
        _mm_nt_kernel,
        grid=(n_cols // tn, T // tm),
        in_specs=[pl.BlockSpec((tm, K), lambda j, i: (i, 0)), w_spec],
        out_specs=pl.BlockSpec((tm, tn), lambda j, i: (i, j)),
        out_shape=jax.ShapeDtypeStruct((T, n_cols), out_dtype),
        scratch_shapes=[pltpu.VMEM((tn, K), BF16)],
        compiler_params=_params("parallel", "arbitrary"),
        name=name,
    )(a, wt)


ATT_UNROLL = 8


def _batched_loop(n, body):
    u = max(d for d in range(1, ATT_UNROLL + 1) if n % d == 0)
    if n == u:
        body(list(range(n)))
        return

    def step(i, c):
        body([i * u + k for k in range(u)])
        return c
    lax.fori_loop(0, n // u, step, 0)


def _attn_blocks(r, slope_r, prev_bias, q_ref, kc_ref, vc_ref, kp_ref, vp_ref, bases, g, acc_ref, m_ref, l_ref):
    def rows(start):
        return pl.ds(start, ATT_BLOCK, r) if r > 1 else pl.ds(start, ATT_BLOCK)

    dn = (((1,), (1,)), ((), ()))
    scale = ATT_HEAD_DIM ** -0.5
    qi = lax.broadcasted_iota(jnp.int32, (ATT_BLOCK, ATT_BLOCK), 0)
    ki = lax.broadcasted_iota(jnp.int32, (ATT_BLOCK, ATT_BLOCK), 1)
    dlt = (qi - ki).astype(F32)
    alibi_c = -slope_r * dlt
    alibi_p = -slope_r * (dlt + float(ATT_BLOCK)) + prev_bias

    scores = []
    for base, base_prev in bases:
        q = q_ref[0, rows(base), :].astype(BF16)
        kc = kc_ref[0, rows(base), :].astype(BF16)
        kp = kp_ref[0, rows(base_prev), :].astype(BF16)
        sc = lax.dot_general(q, kc, dn, preferred_element_type=F32) * scale + alibi_c
        sp = lax.dot_general(q, kp, dn, preferred_element_type=F32) * scale + alibi_p
        scores.append((jnp.where(ki <= qi, sc, NEG), jnp.where(ki >= qi, sp, NEG)))
    probs = []
    for sc, sp in scores:
        m = jnp.max(jnp.maximum(sc, sp), axis=-1, keepdims=True)
        pc = jnp.exp(sc - m)
        pp = jnp.exp(sp - m)
        l = jnp.sum(pc + pp, axis=-1, keepdims=True)
        probs.append((m, l, pc.astype(BF16), pp.astype(BF16)))
    outs = []
    for (base, base_prev), (m, l, pc, pp) in zip(bases, probs):
        vc = vc_ref[0, rows(base), :].astype(BF16)
        vp = vp_ref[0, rows(base_prev), :].astype(BF16)
        outs.append(jnp.dot(pc, vc, preferred_element_type=F32) + jnp.dot(pp, vp, preferred_element_type=F32))
    for (base, _), (m, l, _, _), acc in zip(bases, probs, outs):
        acc_ref[g, rows(base), :] = acc
        m_ref[g, rows(base), :] = jnp.broadcast_to(m, (ATT_BLOCK, ATT_HEAD_DIM))
        l_ref[g, rows(base), :] = jnp.broadcast_to(l, (ATT_BLOCK, ATT_HEAD_DIM))


def _attn_kernel(slopes_ref, *refs):
    ng = len(ATT_PATTERNS)
    q_refs = refs[0:ng]
    kc_refs = refs[ng:2 * ng]
    vc_refs = refs[2 * ng:3 * ng]
    kp_refs = refs[3 * ng:4 * ng]
    vp_refs = refs[4 * ng:5 * ng]
    o_ref = refs[5 * ng]
    acc_ref, m_ref, l_ref = refs[5 * ng + 1:]
    s = pl.program_id(1)
    h = pl.program_id(2)
    prev_bias = jnp.where(s > 0, 0.0, NEG).astype(F32)

    for g, (window, r) in enumerate(ATT_PATTERNS):
        assert window // r == ATT_BLOCK
        nblk = ATT_SUPER // (ATT_BLOCK * r)
        slope_r = slopes_ref[g, h] * float(r)
        common = dict(r=r, slope_r=slope_r, g=g, acc_ref=acc_ref, m_ref=m_ref, l_ref=l_ref,
                      q_ref=q_refs[g], kc_ref=kc_refs[g], vc_ref=vc_refs[g])

        def first(ps, common=common, g=g):
            _attn_blocks(prev_bias=prev_bias, kp_ref=kp_refs[g], vp_ref=vp_refs[g],
                         bases=[(p, p) for p in ps], **common)
        _batched_loop(r, first)

        if nblk > 1:
            def rest(idxs, common=common, g=g, r=r, nblk=nblk):
                bases = []
                for idx in idxs:
                    p = idx // (nblk - 1)
                    j = idx % (nblk - 1) + 1
                    base = p + j * (ATT_BLOCK * r)
                    bases.append((base, base - ATT_BLOCK * r))
                _attn_blocks(prev_bias=jnp.float32(0.0), kp_ref=kc_refs[g], vp_ref=vc_refs[g],
                             bases=bases, **common)
            _batched_loop(r * (nblk - 1), rest)

    ch = 256
    def merge(i, c):
        rs = pl.ds(pl.multiple_of(i * ch, ch), ch)
        ms = [m_ref[g, rs, :] for g in range(ng)]
        mx = functools.reduce(jnp.maximum, ms)
        num = jnp.zeros((ch, ATT_HEAD_DIM), F32)
        den = jnp.zeros((ch, ATT_HEAD_DIM), F32)
        for g in range(ng):
            w = jnp.exp(ms[g] - mx)
            num = num + w * acc_ref[g, rs, :]
            den = den + w * l_ref[g, rs, :]
        o_ref[0, rs, :] = (num / den).astype(o_ref.dtype)
        return c
    lax.fori_loop(0, ATT_SUPER // ch, merge, 0)


def _attention(proj_a, B, S):
    ng = len(ATT_PATTERNS)
    nsb = S // ATT_SUPER
    cb = ATT_HEAD_DIM
    slopes = jnp.asarray(_alibi_slopes(ATT_HEADS).reshape(ng, ATT_HEADS_PER_GROUP))

    def cur_spec(col0, g):
        return pl.BlockSpec((1, ATT_SUPER, cb),
                            lambda b, s, h, g=g, col0=col0: (b, s, col0 // cb + g * ATT_HEADS_PER_GROUP + h))

    def prev_spec(col0, g):
        rows = ATT_BLOCK * ATT_PATTERNS[g][1]
        per = ATT_SUPER // rows
        return pl.BlockSpec((1, rows, cb),
                            lambda b, s, h, g=g, col0=col0, per=per: (
                                b, jnp.maximum(s * per - 1, 0), col0 // cb + g * ATT_HEADS_PER_GROUP + h))

    in_specs = [pl.BlockSpec(memory_space=pltpu.SMEM)]
    in_specs += [cur_spec(COL_AQ, g) for g in range(ng)]
    in_specs += [cur_spec(COL_AK, g) for g in range(ng)]
    in_specs += [cur_spec(COL_AV, g) for g in range(ng)]
    in_specs += [prev_spec(COL_AK, g) for g in range(ng)]
    in_specs += [prev_spec(COL_AV, g) for g in range(ng)]
    return pl.pallas_call(
        _attn_kernel,
        grid=(B, nsb, ATT_HEADS_PER_GROUP),
        in_specs=in_specs,
        out_specs=pl.BlockSpec((1, ATT_SUPER, cb), lambda b, s, h: (b, s, h)),
        out_shape=jax.ShapeDtypeStruct((B, S, ATT_OUT_WIDTH), BF16),
        scratch_shapes=[pltpu.VMEM((ng, ATT_SUPER, cb), F32)] * 3,
        compiler_params=_params("parallel", "parallel", "parallel"),
        name="dilated_attention",
    )(slopes, *([proj_a] * (5 * ng)))


def _log_sigmoid(x):
    return jnp.minimum(x, 0.0) - jnp.log(1.0 + jnp.exp(-jnp.abs(x)))


def _mlstm_kernel(mq_ref, mk_ref, mva_ref, mvb_ref, moa_ref, mob_ref, mif_ref, mift_ref, ifb_ref, ifbt_ref,
                  cw_ref, cb_ref, nw_ref, o_ref, tail_ref, c_ref, n_ref, m_ref):
    L = M_CHUNK
    NB = mq_ref.shape[0]
    c = pl.program_id(0)
    mv_refs = (mva_ref, mvb_ref)
    mo_refs = (moa_ref, mob_ref)
    hpb = M_MLSTM_COLS // M_V_DIM
    scale = M_QK_DIM ** -0.5
    hp = lax.Precision.HIGHEST

    @pl.when(c == 0)
    def _():
        tail_ref[...] = jnp.zeros_like(tail_ref)
        c_ref[...] = jnp.zeros_like(c_ref)
        n_ref[...] = jnp.zeros_like(n_ref)
        m_ref[...] = jnp.zeros_like(m_ref)

    ti = lax.broadcasted_iota(jnp.int32, (L, L), 0)
    si = lax.broadcasted_iota(jnp.int32, (L, L), 1)
    causal = si <= ti
    tri = causal.astype(F32)

    def conv_act(x_ref, bb, part, out_scale):
        cols = slice(part * M_QK_WIDTH, (part + 1) * M_QK_WIDTH)
        x = x_ref[bb]
        xx = jnp.concatenate([tail_ref[bb, :, cols], x], axis=0)
        y = cb_ref[:, cols]
        for j in range(M_CONV):
            off = 8 - (M_CONV - 1) + j
            y = y + cw_ref[j:j + 1, cols] * xx[off:off + L, :]
        tail_ref[bb, :, cols] = x[L - 8:, :]
        act = y * _sigmoid(y)
        return (act if out_scale == 1.0 else act * out_scale).astype(BF16)

    per_b = []
    for bb in range(NB):
        q_act = conv_act(mq_ref, bb, 0, scale)
        k_act = conv_act(mk_ref, bb, 1, 1.0)
        gi_c = mif_ref[bb] + ifb_ref[...]
        gi_r = mift_ref[bb] + ifbt_ref[...]
        bcum_c = jnp.dot(tri, _log_sigmoid(gi_c), precision=hp, preferred_element_type=F32)
        bcum_r = lax.dot_general(_log_sigmoid(gi_r), tri, (((1,), (1,)), ((), ())), precision=hp,
                                 preferred_element_type=F32)
        per_b.append((q_act, k_act, gi_c, gi_r, bcum_c, bcum_r))

    chains = [(bb, hd) for bb in range(NB) for hd in range(M_HEADS)]

    ph1 = []
    for bb, hd in chains:
        q_act, k_act, gi_c, gi_r, bcum_c, bcum_r = per_b[bb]
        st = bb * M_HEADS + hd
        q = q_act[:, hd * M_QK_DIM:(hd + 1) * M_QK_DIM]
        k = k_act[:, hd * M_QK_DIM:(hd + 1) * M_QK_DIM]
        b_c = bcum_c[:, M_HEADS + hd:M_HEADS + hd + 1]
        i_c = gi_c[:, hd:hd + 1]
        b_r = bcum_r[M_HEADS + hd:M_HEADS + hd + 1, :]
        i_r = gi_r[hd:hd + 1, :]
        m_prev = m_ref[st]
        dmat = jnp.where(causal, b_c + (i_r - b_r), NEG)
        inter = b_c + m_prev
        m_t = jnp.maximum(inter, jnp.max(dmat, axis=-1, keepdims=True))
        w_intra = jnp.exp(dmat - m_t)
        w_inter = jnp.exp(inter - m_t)
        qk = lax.dot_general(q, k, (((1,), (1,)), ((), ())), preferred_element_type=F32) * w_intra
        ph1.append((q, k, b_c, i_c, m_prev, m_t, w_inter, qk))

    ph2 = []
    for (bb, hd), (q, k, b_c, i_c, m_prev, m_t, w_inter, qk) in zip(chains, ph1):
        st = bb * M_HEADS + hd
        vcols = slice((hd % hpb) * M_V_DIM, (hd % hpb + 1) * M_V_DIM)
        v = mv_refs[hd // hpb][bb, :, vcols].astype(BF16)
        c_prev = c_ref[st]
        n_prev = n_ref[st]
        num = w_inter * jnp.dot(q, c_prev.astype(BF16), preferred_element_type=F32) \
            + jnp.dot(qk.astype(BF16), v, preferred_element_type=F32)
        den = w_inter * jnp.sum(q.astype(F32) * n_prev, axis=-1, keepdims=True) \
            + jnp.sum(qk, axis=-1, keepdims=True)
        hh = num / jnp.maximum(jnp.abs(den), jnp.exp(-m_t))
        ph2.append((v, c_prev, n_prev, hh))

    for (bb, hd), (q, k, b_c, i_c, m_prev, m_t, w_inter, qk), (v, c_prev, n_prev, hh) in zip(chains, ph1, ph2):
        st = bb * M_HEADS + hd
        b_last = b_c[L - 1:L, :]
        w_log = b_last - b_c + i_c
        m_new = jnp.maximum(b_last + m_prev, jnp.max(w_log, axis=0, keepdims=True))
        wk = jnp.exp(w_log - m_new)
        decay = jnp.exp(b_last + m_prev - m_new)
        kw = (k.astype(F32) * wk)
        c_ref[st] = decay * c_prev + lax.dot_general(kw.astype(BF16), v, (((0,), (0,)), ((), ())),
                                                     preferred_element_type=F32)
        n_ref[st] = decay * n_prev + jnp.sum(kw, axis=0, keepdims=True)
        m_ref[st] = m_new

    for (bb, hd), (v, c_prev, n_prev, hh) in zip(chains, ph2):
        vcols = slice((hd % hpb) * M_V_DIM, (hd % hpb + 1) * M_V_DIM)
        mu = jnp.mean(hh, axis=-1, keepdims=True)
        hc = hh - mu
        var = jnp.mean(hc * hc, axis=-1, keepdims=True)
        hn = hc * lax.rsqrt(var + LN_EPS) * nw_ref[:, hd * M_V_DIM:(hd + 1) * M_V_DIM]
        og = _sigmoid(mo_refs[hd // hpb][bb, :, vcols])
        o_ref[bb, :, hd * M_V_DIM:(hd + 1) * M_V_DIM] = (hn * og).astype(o_ref.dtype)


def _mlstm(proj_a, mif, mif_t, if_bias, conv_w, conv_b, norm_w, B, S):
    L = M_CHUNK
    W = 2 * M_QK_WIDTH
    cw = M_MLSTM_COLS
    ifb = jnp.zeros((1, ROUTE_LANES), F32).at[0, :2 * M_HEADS].set(if_bias)
    ifbt = jnp.broadcast_to(if_bias.reshape(2 * M_HEADS, 1), (2 * M_HEADS, L))

    def col_spec(col0):
        assert col0 % cw == 0
        return pl.BlockSpec((B, L, cw), lambda c, col0=col0: (0, c, col0 // cw))

    const = lambda c: (0, 0)
    return pl.pallas_call(
        _mlstm_kernel,
        grid=(S // L,),
        in_specs=[col_spec(COL_MQK), col_spec(COL_MQK + M_QK_WIDTH),
                  col_spec(COL_MV), col_spec(COL_MV + cw),
                  col_spec(COL_MO), col_spec(COL_MO + cw),
                  pl.BlockSpec((B, L, ROUTE_LANES), lambda c: (0, c, 0)),
                  pl.BlockSpec((B, 2 * M_HEADS, L), lambda c: (0, 0, c)),
                  pl.BlockSpec((1, ROUTE_LANES), const),
                  pl.BlockSpec((2 * M_HEADS, L), const),
                  pl.BlockSpec((M_CONV, W), const),
                  pl.BlockSpec((1, W), const),
                  pl.BlockSpec((1, M_V_WIDTH), const)],
        out_specs=pl.BlockSpec((B, L, M_V_WIDTH), lambda c: (0, c, 0)),
        out_shape=jax.ShapeDtypeStruct((B, S, M_V_WIDTH), BF16),
        scratch_shapes=[pltpu.VMEM((B, 8, W), F32),
                        pltpu.VMEM((B * M_HEADS, M_QK_DIM, M_V_DIM), F32),
                        pltpu.VMEM((B * M_HEADS, 1, M_QK_DIM), F32),
                        pltpu.VMEM((B * M_HEADS, 1, 1), F32)],
        compiler_params=_params("arbitrary"),
        name="mlstm",
    )(proj_a, proj_a, proj_a, proj_a, proj_a, proj_a, mif, mif_t, ifb, ifbt, conv_w, conv_b.reshape(1, W),
      norm_w.reshape(1, M_V_WIDTH))


def _merge_kernel(att_ref, hm_ref, gate_ref, x_ref, g0_ref, b0_ref, wpa_ref, wpm_ref, wo_ref, g_ref, b_ref,
                  wrh_ref, wrl_ref, br_ref, h1_ref, lg_ref):
    halves = [pl.ds(k * MERGE_HALF, MERGE_HALF) for k in range(att_ref.shape[0] // MERGE_HALF)]
    proj = []
    for rs in halves:
        pa = jnp.dot(att_ref[rs, :], wpa_ref[...], preferred_element_type=F32)
        pm = jnp.dot(hm_ref[rs, :], wpm_ref[...], preferred_element_type=F32)
        proj.append((pa, pm))
    ys = []
    for rs, (pa, pm) in zip(halves, proj):
        ga = _sigmoid(gate_ref[rs, :D_MODEL].astype(F32))
        gm = _sigmoid(gate_ref[rs, D_MODEL:].astype(F32))
        merged = (ga * pa + gm * pm).astype(BF16)
        ys.append(jnp.dot(merged, wo_ref[...], preferred_element_type=F32))
    for rs, y in zip(halves, ys):
        h = _layer_norm_rows(x_ref[rs, :], g0_ref[...], b0_ref[...])
        h1 = _layer_norm_rows(DEEPNORM_ALPHA * h + y, g_ref[...], b_ref[...])
        h1_ref[rs, :] = h1
        h1h = h1.astype(BF16)
        h1l = (h1 - h1h.astype(F32)).astype(BF16)
        lg_ref[rs, :] = (jnp.dot(h1h, wrh_ref[...], preferred_element_type=F32)
                         + jnp.dot(h1l, wrh_ref[...], preferred_element_type=F32)
                         + jnp.dot(h1h, wrl_ref[...], preferred_element_type=F32)) + br_ref[...]


MERGE_HALF = 256


def _merge(att, hm, gate, x2, g0, b0, wpa, wpm, wo, g1, b1, wr, br, tm=512):
    T, D = x2.shape
    const = lambda i: (0, 0)
    one = pl.Buffered(1)
    wrh = wr.astype(BF16)
    wrl = (wr - wrh.astype(F32)).astype(BF16)
    return pl.pallas_call(
        _merge_kernel,
        grid=(T // tm,),
        in_specs=[pl.BlockSpec((tm, ATT_OUT_WIDTH), lambda i: (i, 0)),
                  pl.BlockSpec((tm, M_V_WIDTH), lambda i: (i, 0)),
                  pl.BlockSpec((tm, N_BRANCHES * D), lambda i: (i, 0)),
                  pl.BlockSpec((tm, D), lambda i: (i, 0)),
                  pl.BlockSpec((1, D), const),
                  pl.BlockSpec((1, D), const),
                  pl.BlockSpec((ATT_OUT_WIDTH, D), const, pipeline_mode=one),
                  pl.BlockSpec((M_V_WIDTH, D), const, pipeline_mode=one),
                  pl.BlockSpec((D, D), const, pipeline_mode=one),
                  pl.BlockSpec((1, D), const),
                  pl.BlockSpec((1, D), const),
                  pl.BlockSpec((D, ROUTE_LANES), const, pipeline_mode=one),
                  pl.BlockSpec((D, ROUTE_LANES), const, pipeline_mode=one),
                  pl.BlockSpec((1, ROUTE_LANES), const)],
        out_specs=[pl.BlockSpec((tm, D), lambda i: (i, 0)),
                   pl.BlockSpec((tm, ROUTE_LANES), lambda i: (i, 0))],
        out_shape=[jax.ShapeDtypeStruct((T, D), F32), jax.ShapeDtypeStruct((T, ROUTE_LANES), F32)],
        compiler_params=_params("parallel"),
        name="merge_out_ln1",
    )(att, hm, gate, x2, g0.reshape(1, D), b0.reshape(1, D), wpa, wpm, wo, g1.reshape(1, D), b1.reshape(1, D),
      wrh, wrl, br)


def _route_kernel(lg_ref, e_ref, w_ref):
    lg = lg_ref[...]
    col = lax.broadcasted_iota(jnp.int32, lg.shape, 1)
    big = jnp.int32(ROUTE_LANES)

    def first_argmax(v, vmax):
        return jnp.min(jnp.where(v == vmax, col, big), axis=-1, keepdims=True)

    gl = jnp.where(col < N_GROUPS, lg, NEG)
    gmax = jnp.max(gl, axis=-1, keepdims=True)
    grp = first_argmax(gl, gmax)
    gsum = jnp.sum(jnp.where(col < N_GROUPS, jnp.exp(lg - gmax), 0.0), axis=-1, keepdims=True)
    g_w = 1.0 / gsum
    ecol = col - N_GROUPS
    egrp = lax.shift_right_arithmetic(ecol, int(math.log2(EXPERTS_PER_GROUP)))
    in_grp = (ecol >= 0) & (ecol < N_EXPERTS) & (egrp == grp)
    el = jnp.where(in_grp, lg, NEG)
    v1 = jnp.max(el, axis=-1, keepdims=True)
    i1 = first_argmax(el, v1)
    el2 = jnp.where(col == i1, NEG, el)
    v2 = jnp.max(el2, axis=-1, keepdims=True)
    i2 = first_argmax(el2, v2)
    t = jnp.exp(v2 - v1)
    p1 = 1.0 / (1.0 + t)
    p2 = t / (1.0 + t)
    e_ref[...] = jnp.where(col == 0, i1 - N_GROUPS, jnp.where(col == 1, i2 - N_GROUPS, 0))
    w_ref[...] = jnp.where(col == 0, g_w * p1, jnp.where(col == 1, g_w * p2, 0.0))


def _route(logits, tm=1024):
    T = logits.shape[0]
    spec = pl.BlockSpec((tm, ROUTE_LANES), lambda i: (i, 0))
    return pl.pallas_call(
        _route_kernel,
        grid=(T // tm,),
        in_specs=[spec],
        out_specs=[spec, spec],
        out_shape=[jax.ShapeDtypeStruct((T, ROUTE_LANES), jnp.int32),
                   jax.ShapeDtypeStruct((T, ROUTE_LANES), F32)],
        compiler_params=_params("parallel"),
        name="route",
    )(logits)


def _dispatch_plan(e_tk, T):
    M = T * TOP_K
    e_flat = e_tk.reshape(M)
    onehot = (e_flat[:, None] == jnp.arange(N_EXPERTS, dtype=jnp.int32)[None, :]).astype(jnp.int32)
    csum = jnp.cumsum(onehot, axis=0)
    counts = csum[-1]
    rank = jnp.sum((csum - onehot) * onehot, axis=1)
    padded = (counts + MOE_SUB - 1) // MOE_SUB * MOE_SUB
    pstart = jnp.cumsum(padded) - padded
    dest = jnp.sum(onehot * pstart[None, :], axis=1) + rank

    nsb_max = N_EXPERTS + M // MOE_SUPER
    nsb_e = (padded + MOE_SUPER - 1) // MOE_SUPER
    sb_end = jnp.cumsum(nsb_e)
    sb_beg = sb_end - nsb_e
    total = sb_end[-1]
    sb = jnp.arange(nsb_max, dtype=jnp.int32)
    sb_c = jnp.minimum(sb, total - 1)
    ex = jnp.sum((sb_end[None, :] <= sb_c[:, None]).astype(jnp.int32), axis=1)
    local = sb_c - sb_beg[ex]
    row0 = pstart[ex] + local * MOE_SUPER
    active = sb < total
    cnt = jnp.where(active, jnp.clip(counts[ex] - local * MOE_SUPER, 0, MOE_SUPER), 0)
    nsub = jnp.where(active, jnp.clip(padded[ex] - local * MOE_SUPER, 0, MOE_SUPER) // MOE_SUB, 0)
    return (ex.astype(jnp.int32), row0.astype(jnp.int32), cnt.astype(jnp.int32), nsub.astype(jnp.int32),
            dest.astype(jnp.int32), jnp.sum(padded).astype(jnp.int32).reshape(1))


def _moe_kernel(sb_ex, sb_row0, sb_cnt, sb_nsub, dest, used_rows,
                h1_hbm, wga_ref, wua_ref, wda_ref, wgb_ref, wub_ref, wdb_ref, ys_hbm,
                stage_buf, acc_buf, slot_tok, gsem, ssem):
    b = pl.program_id(0)
    j = pl.program_id(1)
    nb = pl.num_programs(0)
    nsub = sb_nsub[b]
    cnt = sb_cnt[b]
    slot = lax.rem(b, 2)
    U = MOE_DMA_UNROLL

    def gather_batches(bb):
        return (sb_cnt[bb] + (U - 1)) // U

    def gather_issue(bb):
        r0 = sb_row0[bb]

        def pad_row(i, c):
            slot_tok[r0 + i] = 0
            return c
        lax.fori_loop(sb_cnt[bb], gather_batches(bb) * U, pad_row, 0)

        def issue(q, c):
            i0 = pl.multiple_of(q * U, U)
            for k in range(U):
                tok = slot_tok[r0 + i0 + k]
                pltpu.make_async_copy(h1_hbm.at[lax.shift_right_logical(tok, 3), pl.ds(tok & 7, 1), :],
                                      stage_buf.at[half, q * (U // 8) + k // 8, pl.ds(k % 8, 1), :], gsem).start()
            return c
        half = lax.rem(bb, 2)
        lax.fori_loop(0, gather_batches(bb), issue, 0)

    def gather_wait(bb):
        def wait(k, c):
            pltpu.make_async_copy(h1_hbm.at[pl.ds(0, U // 8)], stage_buf.at[0, pl.ds(0, U // 8)], gsem).wait()
            return c
        lax.fori_loop(0, gather_batches(bb), wait, 0)

    def build_tables():
        n_asg = dest.shape[0]
        step = 16

        def fill(q, c):
            for k in range(step):
                slot_tok[dest[q * step + k]] = q * (step // TOP_K) + k // TOP_K
            return c
        lax.fori_loop(0, n_asg // step, fill, 0)

    def out_copy(k, r0):
        rs = pl.ds(pl.multiple_of(k * MOE_SUB, MOE_SUB), MOE_SUB)
        return pltpu.make_async_copy(acc_buf.at[rs, :],
                                     ys_hbm.at[pl.ds(pl.multiple_of(r0 + k * MOE_SUB, MOE_SUB), MOE_SUB), :], ssem)

    def out_issue():
        r0 = sb_row0[b]

        def issue(k, c):
            out_copy(k, r0).start()
            return c
        lax.fori_loop(0, nsub, issue, 0)

    def out_wait(n_sub):
        def wait(k, c):
            out_copy(0, 0).wait()
            return c
        lax.fori_loop(0, n_sub, wait, 0)

    @pl.when(j == 0)
    def _first_step():
        @pl.when(b == 0)
        def _():
            build_tables()

            def clear(q, c):
                for half in range(2):
                    stage_buf[half, q] = jnp.zeros((8, D_MODEL), F32)
                return c
            lax.fori_loop(0, stage_buf.shape[1], clear, 0)
            gather_issue(0)

        prev = jnp.maximum(b - 1, 0)

        @pl.when((b > 0) & (sb_nsub[prev] > 0))
        def _():
            out_wait(sb_nsub[prev])

        @pl.when(nsub > 0)
        def _():
            gather_wait(b)

            def clear_acc(k, c):
                rs = pl.ds(pl.multiple_of(k * MOE_SUB, MOE_SUB), MOE_SUB)
                acc_buf[rs, :] = jnp.zeros((MOE_SUB, D_MODEL), F32)
                return c
            lax.fori_loop(0, nsub, clear_acc, 0)

        nxt = jnp.minimum(b + 1, nb - 1)

        @pl.when((b + 1 < nb) & (sb_nsub[nxt] > 0))
        def _():
            gather_issue(nxt)

    def ffn_tile(wg_ref, wu_ref, wd_ref):
        def chunk(r0, rows):
            rs = pl.ds(r0, rows)
            t0 = r0 // 8 if isinstance(r0, int) else pl.multiple_of(r0 // 8, MOE_SUB // 8)
            tiles = pl.ds(t0, rows // 8)
            x = stage_buf[slot, tiles].reshape(rows, D_MODEL).astype(BF16)
            gt = jnp.dot(x, wg_ref[0].astype(BF16), preferred_element_type=F32)
            ut = jnp.dot(x, wu_ref[0].astype(BF16), preferred_element_type=F32)
            hmid = (gt * _sigmoid(gt) * ut).astype(BF16)
            acc_buf[rs, :] += jnp.dot(hmid, wd_ref[0].astype(BF16), preferred_element_type=F32)

        whole = [n for n in range(MOE_CHUNK // MOE_SUB + 1, MOE_WHOLE_MAX // MOE_SUB + 1)]
        is_whole = functools.reduce(jnp.logical_or, [nsub == n for n in whole])
        for n in whole:
            @pl.when(nsub == n)
            def _(n=n):
                chunk(0, n * MOE_SUB)

        @pl.when(jnp.logical_not(is_whole))
        def _():
            per = MOE_CHUNK // MOE_SUB
            nfull = nsub // per

            def full(k, c):
                chunk(pl.multiple_of(k * MOE_CHUNK, MOE_CHUNK), MOE_CHUNK)
                return c
            lax.fori_loop(0, nfull, full, 0)
            rem = nsub - nfull * per
            base = nfull * MOE_CHUNK
            size = MOE_CHUNK // 2
            while size >= MOE_SUB:
                units = size // MOE_SUB

                @pl.when(lax.rem(rem, 2 * units) >= units)
                def _(size=size, units=units):
                    skipped = (rem // (2 * units)) * (2 * units)
                    chunk(pl.multiple_of(base + skipped * MOE_SUB, size), size)
                size //= 2

    @pl.when(nsub > 0)
    def _tiles():
        ffn_tile(wga_ref, wua_ref, wda_ref)

        @pl.when(j == MOE_NJ - 1)
        def _():
            ffn_tile(wgb_ref, wub_ref, wdb_ref)

    @pl.when(j == MOE_NJ - 1)
    def _last_step():
        @pl.when(nsub > 0)
        def _():
            out_issue()

        @pl.when((b == nb - 1) & (nsub > 0))
        def _():
            out_wait(nsub)

        @pl.when(b == nb - 1)
        def _():
            acc_buf[pl.ds(0, MOE_SUB), :] = jnp.zeros((MOE_SUB, D_MODEL), F32)

            def fill(k, c):
                rows = pl.ds(pl.multiple_of(k * MOE_SUB, MOE_SUB), MOE_SUB)
                pltpu.make_async_copy(acc_buf.at[pl.ds(0, MOE_SUB), :], ys_hbm.at[rows, :], ssem).start()
                return c
            first = used_rows[0] // MOE_SUB
            total = ys_hbm.shape[0] // MOE_SUB
            lax.fori_loop(first, total, fill, 0)

            def drain(k, c):
                out_copy(0, 0).wait()
                return c
            lax.fori_loop(first, total, drain, 0)


def _moe_ffn(h1, plan, w_gate, w_up, w_down):
    T, D = h1.shape
    sb_ex, sb_row0, sb_cnt, sb_nsub, dest, used_rows = plan
    nsb_max = sb_ex.shape[0]
    n_rows = dest.shape[0] + N_EXPERTS * MOE_SUB
    last = MOE_NFT - 1
    tail0 = MOE_NFT * MOE_FT
    one = pl.Buffered(1)

    def ja(b, j, nsub):
        return jnp.where(nsub[b] > 0, jnp.minimum(j, last), last)

    def main_cols(b, j, ex, r0, ct, ns, ds, ur):
        return (ex[b], 0, ja(b, j, ns))

    def main_rows(b, j, ex, r0, ct, ns, ds, ur):
        return (ex[b], ja(b, j, ns), 0)

    def tail_cols(b, j, ex, r0, ct, ns, ds, ur):
        return (ex[b], 0, tail0)

    def tail_rows(b, j, ex, r0, ct, ns, ds, ur):
        return (ex[b], tail0, 0)

    E = pl.Element

    grid_spec = pltpu.PrefetchScalarGridSpec(
        num_scalar_prefetch=6,
        grid=(nsb_max, MOE_NJ),
        in_specs=[pl.BlockSpec(memory_space=pl.ANY),
                  pl.BlockSpec((1, D, MOE_FT), main_cols),
                  pl.BlockSpec((1, D, MOE_FT), main_cols),
                  pl.BlockSpec((1, MOE_FT, D), main_rows),
                  pl.BlockSpec((E(1), E(D), E(MOE_FT_TAIL)), tail_cols, pipeline_mode=one),
                  pl.BlockSpec((E(1), E(D), E(MOE_FT_TAIL)), tail_cols, pipeline_mode=one),
                  pl.BlockSpec((E(1), E(MOE_FT_TAIL), E(D)), tail_rows, pipeline_mode=one)],
        out_specs=pl.BlockSpec(memory_space=pl.ANY),
        scratch_shapes=[pltpu.VMEM((2, MOE_SUPER // 8, 8, D), F32),
                        pltpu.VMEM((MOE_SUPER, D), F32),
                        pltpu.SMEM((n_rows,), jnp.int32),
                        pltpu.SemaphoreType.DMA(()),
                        pltpu.SemaphoreType.DMA(())],
    )
    return pl.pallas_call(
        _moe_kernel,
        grid_spec=grid_spec,
        out_shape=jax.ShapeDtypeStruct((n_rows, D), F32),
        compiler_params=_params("arbitrary", "arbitrary"),
        name="moe_experts",
    )(sb_ex, sb_row0, sb_cnt, sb_nsub, dest, used_rows, h1.reshape(T // 8, 8, D),
      w_gate, w_up, w_down, w_gate, w_up, w_down)


LN_OUT_ROWS = 512
LN_OUT_BATCH = 32
assert LN_OUT_BATCH % (8 * TOP_K) == 0 and (LN_OUT_ROWS * TOP_K) % LN_OUT_BATCH == 0


def _ln_out_kernel(dest, h1_ref, rw_ref, g_ref, b_ref, ys_hbm, o_ref, y_buf, sem):
    i = pl.program_id(0)
    nt = pl.num_programs(0)
    tm = LN_OUT_ROWS
    toks = LN_OUT_BATCH // TOP_K

    def gather_issue(tile, buf):
        base = tile * (tm * TOP_K)

        def issue(q, c):
            for k in range(LN_OUT_BATCH):
                d = dest[base + q * LN_OUT_BATCH + k]
                t = k // TOP_K
                pltpu.make_async_copy(ys_hbm.at[lax.shift_right_logical(d, 3), pl.ds(d & 7, 1), :],
                                      y_buf.at[buf, k % TOP_K, q * (toks // 8) + t // 8, pl.ds(t % 8, 1), :],
                                      sem.at[buf]).start()
            return c
        lax.fori_loop(0, tm * TOP_K // LN_OUT_BATCH, issue, 0)

    def gather_wait(buf):
        for s in range(TOP_K):
            pltpu.make_async_copy(ys_hbm.at[pl.ds(0, tm // 8)], y_buf.at[buf, s], sem.at[buf]).wait()

    @pl.when(i == 0)
    def _():
        gather_issue(0, 0)

    @pl.when(i + 1 < nt)
    def _():
        gather_issue(i + 1, lax.rem(i + 1, 2))

    cur = lax.rem(i, 2)
    gather_wait(cur)
    rw = rw_ref[...]
    z = DEEPNORM_ALPHA * h1_ref[...]
    for s in range(TOP_K):
        z = z + rw[:, s:s + 1] * y_buf[cur, s].reshape(tm, D_MODEL)
    o_ref[...] = _layer_norm_rows(z, g_ref[...], b_ref[...])


def _ln_out(h1, ys, dest, rw, g, b):
    T, D = h1.shape
    tm = LN_OUT_ROWS
    n_rows = ys.shape[0]
    grid_spec = pltpu.PrefetchScalarGridSpec(
        num_scalar_prefetch=1,
        grid=(T // tm,),
        in_specs=[pl.BlockSpec((tm, D), lambda i, ds: (i, 0)),
                  pl.BlockSpec((tm, ROUTE_LANES), lambda i, ds: (i, 0)),
                  pl.BlockSpec((1, D), lambda i, ds: (0, 0)),
                  pl.BlockSpec((1, D), lambda i, ds: (0, 0)),
                  pl.BlockSpec(memory_space=pl.ANY)],
        out_specs=pl.BlockSpec((tm, D), lambda i, ds: (i, 0)),
        scratch_shapes=[pltpu.VMEM((2, TOP_K, tm // 8, 8, D), F32),
                        pltpu.SemaphoreType.DMA((2,))],
    )
    return pl.pallas_call(
        _ln_out_kernel,
        grid_spec=grid_spec,
        out_shape=jax.ShapeDtypeStruct((T, D), F32),
        compiler_params=_params("arbitrary"),
        name="combine_ln2",
    )(dest, h1, rw, g.reshape(1, D), b.reshape(1, D), ys.reshape(n_rows // 8, 8, D))


def kernel(x, ln_in_g, ln_in_b, w_in, m_conv_w, m_conv_b, m_if_bias, m_norm_w, w_proj_att, w_proj_mlstm, w_out,
           ln1_g, ln1_b, w_router_group, b_router_group, w_router_expert, b_router_expert, w_gate, w_up, w_down,
           ln2_g, ln2_b):
    B, S, D = x.shape
    T = B * S
    assert D == D_MODEL and S % ATT_SUPER == 0 and w_in.shape[0] == DEPTH == 1

    x2 = x.reshape(T, D)
    for l in range(DEPTH):
        wt = jnp.swapaxes(w_in[l], 0, 1)
        hb, mif = _ln_in(x2, ln_in_g, ln_in_b, wt, COL_MIF)
        proj_a = _matmul_nt(hb, wt, 0, PROJ_A_WIDTH, 1024, 1280, F32, "in_proj_a")
        gate = _matmul_nt(hb, wt, COL_GATE, N_BRANCHES * D, 1024, 1024, BF16, "in_proj_gate")

        proj_a3 = proj_a.reshape(B, S, PROJ_A_WIDTH)
        att = _attention(proj_a3, B, S)
        mif3 = mif.reshape(B, S, ROUTE_LANES)
        mif_t = jnp.swapaxes(mif3[:, :, :2 * M_HEADS], 1, 2)
        hm = _mlstm(proj_a3, mif3, mif_t, m_if_bias[l], m_conv_w[l], m_conv_b[l], m_norm_w[l], B, S)

        lane_pad = ROUTE_LANES - N_GROUPS - N_EXPERTS
        w_r = jnp.pad(jnp.concatenate([w_router_group[l], w_router_expert[l]], axis=1), ((0, 0), (0, lane_pad)))
        b_r = jnp.pad(jnp.concatenate([b_router_group[l], b_router_expert[l]]), (0, lane_pad)).reshape(1, ROUTE_LANES)
        h1, logits = _merge(att.reshape(T, ATT_OUT_WIDTH), hm.reshape(T, M_V_WIDTH), gate, x2, ln_in_g, ln_in_b,
                            w_proj_att[l].astype(BF16), w_proj_mlstm[l].astype(BF16), w_out[l].astype(BF16),
                            ln1_g[l], ln1_b[l], w_r, b_r)

        e_out, rw = _route(logits)
        plan = _dispatch_plan(e_out[:, :TOP_K], T)
        ys = _moe_ffn(h1, plan, w_gate[l], w_up[l], w_down[l])
        h = _ln_out(h1, ys, plan[4], rw, ln2_g[l], ln2_b[l])
    return h.reshape(B, S, D)
```

```python
import functools
import math

import numpy as np
import jax
import jax.numpy as jnp
from jax import lax
from jax.experimental import pallas as pl
from jax.experimental.pallas import tpu as pltpu

F32 = jnp.float32
BF16 = jnp.bfloat16

D_MODEL = 2048
ATT_HEAD_DIM = 128
ATT_HEADS_PER_GROUP = 4
ATT_PATTERNS = ((128, 1), (512, 4), (2048, 16))
ATT_HEADS = ATT_HEADS_PER_GROUP * len(ATT_PATTERNS)
ATT_WIDTH = ATT_HEADS * ATT_HEAD_DIM
ATT_OUT_WIDTH = ATT_HEADS_PER_GROUP * ATT_HEAD_DIM
ATT_BLOCK = 128
ATT_SUPER = 2048

M_HEADS = 4
M_QK_DIM = 128
M_V_DIM = 256
M_QK_WIDTH = M_HEADS * M_QK_DIM
M_V_WIDTH = M_HEADS * M_V_DIM
M_CONV = 4
M_CHUNK = 128
M_MLSTM_COLS = 512

N_BRANCHES = 2
IN_PROJ_SPLITS = (ATT_WIDTH, ATT_WIDTH, ATT_WIDTH, 2 * M_QK_WIDTH, M_V_WIDTH, M_V_WIDTH,
                  2 * M_HEADS, N_BRANCHES * D_MODEL)
COL_AQ = 0
COL_AK = ATT_WIDTH
COL_AV = 2 * ATT_WIDTH
COL_MQK = 3 * ATT_WIDTH
COL_MV = COL_MQK + 2 * M_QK_WIDTH
COL_MO = COL_MV + M_V_WIDTH
COL_MIF = COL_MO + M_V_WIDTH
COL_GATE = COL_MIF + 2 * M_HEADS
PROJ_A_WIDTH = COL_MIF

N_GROUPS = 4
EXPERTS_PER_GROUP = 8
N_EXPERTS = N_GROUPS * EXPERTS_PER_GROUP
TOP_K = 2
D_FF_EXPERT = 1408
MOE_SUB = 128
MOE_SUPER = 1024
MOE_CHUNK = 512
MOE_WHOLE_MAX = 768
MOE_FT = 256
MOE_NFT = D_FF_EXPERT // MOE_FT
MOE_FT_TAIL = D_FF_EXPERT - MOE_NFT * MOE_FT
MOE_NJ = MOE_NFT
MOE_DMA_UNROLL = 32
assert MOE_DMA_UNROLL % 8 == 0 and MOE_SUB % MOE_DMA_UNROLL == 0
assert MOE_FT_TAIL > 0 and D_FF_EXPERT % MOE_FT_TAIL == 0 and MOE_FT_TAIL % 128 == 0
ROUTE_LANES = 128

DEPTH = 1
DEEPNORM_ALPHA = (2 * DEPTH) ** 0.25
LN_EPS = 1e-5
NEG = -1e30

VMEM_LIMIT = 56 * 1024 * 1024


def _alibi_slopes(n):
    def geometric(k):
        start = 2.0 ** (-8.0 / k)
        return [start ** (i + 1) for i in range(k)]
    c = 2 ** int(math.floor(math.log2(n)))
    s = geometric(c) if c == n else geometric(c) + geometric(2 * c)[0::2][: n - c]
    return np.array(sorted(s, reverse=True), dtype=np.float32)


def _params(*sem):
    return pltpu.CompilerParams(dimension_semantics=sem, vmem_limit_bytes=VMEM_LIMIT)


def _layer_norm_rows(z, g, b):
    mu = jnp.mean(z, axis=-1, keepdims=True)
    zc = z - mu
    var = jnp.mean(zc * zc, axis=-1, keepdims=True)
    return zc * lax.rsqrt(var + LN_EPS) * g + b


def _sigmoid(x):
    return 1.0 / (1.0 + jnp.exp(-x))


def _ln_in_kernel(x_ref, g_ref, b_ref, w_ref, hb_ref, mif_ref):
    hb = _layer_norm_rows(x_ref[...], g_ref[...], b_ref[...]).astype(BF16)
    hb_ref[...] = hb
    mif_ref[...] = lax.dot_general(hb, w_ref[...].astype(BF16), (((1,), (1,)), ((), ())),
                                   preferred_element_type=F32)


def _ln_in(x2, g, b, wt, row0, tm=512):
    T, D = x2.shape
    assert row0 % ROUTE_LANES == 0
    return pl.pallas_call(
        _ln_in_kernel,
        grid=(T // tm,),
        in_specs=[pl.BlockSpec((tm, D), lambda i: (i, 0)),
                  pl.BlockSpec((1, D), lambda i: (0, 0)),
                  pl.BlockSpec((1, D), lambda i: (0, 0)),
                  pl.BlockSpec((ROUTE_LANES, D), lambda i: (row0 // ROUTE_LANES, 0))],
        out_specs=[pl.BlockSpec((tm, D), lambda i: (i, 0)),
                   pl.BlockSpec((tm, ROUTE_LANES), lambda i: (i, 0))],
        out_shape=[jax.ShapeDtypeStruct((T, D), BF16), jax.ShapeDtypeStruct((T, ROUTE_LANES), F32)],
        compiler_params=_params("parallel"),
        name="ln_in",
    )(x2, g.reshape(1, D), b.reshape(1, D), wt)


def _mm_nt_kernel(a_ref, w_ref, o_ref, wb_ref):
    @pl.when(pl.program_id(1) == 0)
    def _():
        wb_ref[...] = w_ref[...].astype(BF16)

    o_ref[...] = lax.dot_general(a_ref[...], wb_ref[...], (((1,), (1,)), ((), ())),
                                 preferred_element_type=F32).astype(o_ref.dtype)


def _matmul_nt(a, wt, row0, n_cols, tm, tn, out_dtype, name):
    T, K = a.shape
    if row0 % tn == 0:
        w_spec = pl.BlockSpec((tn, K), lambda j, i: (j + row0 // tn, 0))
    else:
        assert row0 % 8 == 0 and tn % 8 == 0
        w_spec = pl.BlockSpec((pl.Element(tn), pl.Element(K)),
                              lambda j, i: ((row0 // 8 + j * (tn // 8)) * 8, 0))
    return pl.pallas_call(
        _mm_nt_kernel,
        grid=(n_cols // tn, T // tm),
        in_specs=[pl.BlockSpec((tm, K), lambda j, i: (i, 0)), w_spec],
        out_specs=pl.BlockSpec((tm, tn), lambda j, i: (i, j)),
        out_shape=jax.ShapeDtypeStruct((T, n_cols), out_dtype),
        scratch_shapes=[pltpu.VMEM((tn, K), BF16)],
        compiler_params=_params("parallel", "arbitrary"),
        name=name,
    )(a, wt)


ATT_UNROLL = 8


def _batched_loop(n, body):
    u = max(d for d in range(1, ATT_UNROLL + 1) if n % d == 0)
    if n == u:
        body(list(range(n)))
        return

    def step(i, c):
        body([i * u + k for k in range(u)])
        return c
    lax.fori_loop(0, n // u, step, 0)


def _attn_blocks(r, slope_r, prev_bias, q_ref, kc_ref, vc_ref, kp_ref, vp_ref, bases, g, acc_ref, m_ref, l_ref):
    def rows(start):
        return pl.ds(start, ATT_BLOCK, r) if r > 1 else pl.ds(start, ATT_BLOCK)

    dn = (((1,), (1,)), ((), ()))
    scale = ATT_HEAD_DIM ** -0.5
    qi = lax.broadcasted_iota(jnp.int32, (ATT_BLOCK, 2 * ATT_BLOCK), 0)
    ki = lax.broadcasted_iota(jnp.int32, (ATT_BLOCK, 2 * ATT_BLOCK), 1)
    delta = ATT_BLOCK + qi - ki
    valid = (delta >= 0) & (delta <= ATT_BLOCK)
    alibi = -slope_r * delta.astype(F32) + jnp.where(ki < ATT_BLOCK, prev_bias, 0.0)

    scores = []
    for base, base_prev in bases:
        q = q_ref[0, rows(base), :].astype(BF16)
        keys = jnp.concatenate([kp_ref[0, rows(base_prev), :].astype(BF16),
                                kc_ref[0, rows(base), :].astype(BF16)], axis=0)
        s = lax.dot_general(q, keys, dn, preferred_element_type=F32) * scale + alibi
        scores.append(jnp.where(valid, s, NEG))
    probs = []
    for s in scores:
        m = jnp.max(s, axis=-1, keepdims=True)
        p = jnp.exp(s - m)
        l = jnp.sum(p, axis=-1, keepdims=True)
        probs.append((m, l, p.astype(BF16)))
    outs = []
    for (base, base_prev), (m, l, p) in zip(bases, probs):
        vals = jnp.concatenate([vp_ref[0, rows(base_prev), :].astype(BF16),
                                vc_ref[0, rows(base), :].astype(BF16)], axis=0)
        outs.append(jnp.dot(p, vals, preferred_element_type=F32))
    for (base, _), (m, l, _), acc in zip(bases, probs, outs):
        acc_ref[g, rows(base), :] = acc
        m_ref[g, rows(base), :] = jnp.broadcast_to(m, (ATT_BLOCK, ATT_HEAD_DIM))
        l_ref[g, rows(base), :] = jnp.broadcast_to(l, (ATT_BLOCK, ATT_HEAD_DIM))


def _attn_kernel(slopes_ref, *refs):
    ng = len(ATT_PATTERNS)
    q_refs = refs[0:ng]
    kc_refs = refs[ng:2 * ng]
    vc_refs = refs[2 * ng:3 * ng]
    kp_refs = refs[3 * ng:4 * ng]
    vp_refs = refs[4 * ng:5 * ng]
    o_ref = refs[5 * ng]
    acc_ref, m_ref, l_ref = refs[5 * ng + 1:]
    s = pl.program_id(1)
    h = pl.program_id(2)
    prev_bias = jnp.where(s > 0, 0.0, NEG).astype(F32)

    for g, (window, r) in enumerate(ATT_PATTERNS):
        assert window // r == ATT_BLOCK
        nblk = ATT_SUPER // (ATT_BLOCK * r)
        slope_r = slopes_ref[g, h] * float(r)
        common = dict(r=r, slope_r=slope_r, g=g, acc_ref=acc_ref, m_ref=m_ref, l_ref=l_ref,
                      q_ref=q_refs[g], kc_ref=kc_refs[g], vc_ref=vc_refs[g])

        def first(ps, common=common, g=g):
            _attn_blocks(prev_bias=prev_bias, kp_ref=kp_refs[g], vp_ref=vp_refs[g],
                         bases=[(p, p) for p in ps], **common)
        _batched_loop(r, first)

        if nblk > 1:
            def rest(idxs, common=common, g=g, r=r, nblk=nblk):
                bases = []
                for idx in idxs:
                    p = idx // (nblk - 1)
                    j = idx % (nblk - 1) + 1
                    base = p + j * (ATT_BLOCK * r)
                    bases.append((base, base - ATT_BLOCK * r))
                _attn_blocks(prev_bias=jnp.float32(0.0), kp_ref=kc_refs[g], vp_ref=vc_refs[g],
                             bases=bases, **common)
            _batched_loop(r * (nblk - 1), rest)

    ch = 256
    def merge(i, c):
        rs = pl.ds(pl.multiple_of(i * ch, ch), ch)
        ms = [m_ref[g, rs, :] for g in range(ng)]
        mx = functools.reduce(jnp.maximum, ms)
        num = jnp.zeros((ch, ATT_HEAD_DIM), F32)
        den = jnp.zeros((ch, ATT_HEAD_DIM), F32)
        for g in range(ng):
            w = jnp.exp(ms[g] - mx)
            num = num + w * acc_ref[g, rs, :]
            den = den + w * l_ref[g, rs, :]
        o_ref[0, rs, :] = (num / den).astype(o_ref.dtype)
        return c
    lax.fori_loop(0, ATT_SUPER // ch, merge, 0)


def _attention(proj_a, B, S):
    ng = len(ATT_PATTERNS)
    nsb = S // ATT_SUPER
    cb = ATT_HEAD_DIM
    slopes = jnp.asarray(_alibi_slopes(ATT_HEADS).reshape(ng, ATT_HEADS_PER_GROUP))

    def cur_spec(col0, g):
        return pl.BlockSpec((1, ATT_SUPER, cb),
                            lambda b, s, h, g=g, col0=col0: (b, s, col0 // cb + g * ATT_HEADS_PER_GROUP + h))

    def prev_spec(col0, g):
        rows = ATT_BLOCK * ATT_PATTERNS[g][1]
        per = ATT_SUPER // rows
        return pl.BlockSpec((1, rows, cb),
                            lambda b, s, h, g=g, col0=col0, per=per: (
                                b, jnp.maximum(s * per - 1, 0), col0 // cb + g * ATT_HEADS_PER_GROUP + h))

    in_specs = [pl.BlockSpec(memory_space=pltpu.SMEM)]
    in_specs += [cur_spec(COL_AQ, g) for g in range(ng)]
    in_specs += [cur_spec(COL_AK, g) for g in range(ng)]
    in_specs += [cur_spec(COL_AV, g) for g in range(ng)]
    in_specs += [prev_spec(COL_AK, g) for g in range(ng)]
    in_specs += [prev_spec(COL_AV, g) for g in range(ng)]
    return pl.pallas_call(
        _attn_kernel,
        grid=(B, nsb, ATT_HEADS_PER_GROUP),
        in_specs=in_specs,
        out_specs=pl.BlockSpec((1, ATT_SUPER, cb), lambda b, s, h: (b, s, h)),
        out_shape=jax.ShapeDtypeStruct((B, S, ATT_OUT_WIDTH), BF16),
        scratch_shapes=[pltpu.VMEM((ng, ATT_SUPER, cb), F32)] * 3,
        compiler_params=_params("parallel", "parallel", "parallel"),
        name="dilated_attention",
    )(slopes, *([proj_a] * (5 * ng)))


def _log_sigmoid(x):
    return jnp.minimum(x, 0.0) - jnp.log(1.0 + jnp.exp(-jnp.abs(x)))


def _mlstm_kernel(mq_ref, mk_ref, mva_ref, mvb_ref, moa_ref, mob_ref, mif_ref, mift_ref, ifb_ref, ifbt_ref,
                  cw_ref, cb_ref, nw_ref, o_ref, tail_ref, c_ref, n_ref, m_ref):
    L = M_CHUNK
    NB = mq_ref.shape[0]
    c = pl.program_id(0)
    mv_refs = (mva_ref, mvb_ref)
    mo_refs = (moa_ref, mob_ref)
    hpb = M_MLSTM_COLS // M_V_DIM
    scale = M_QK_DIM ** -0.5
    hp = lax.Precision.HIGHEST

    @pl.when(c == 0)
    def _():
        tail_ref[...] = jnp.zeros_like(tail_ref)
        c_ref[...] = jnp.zeros_like(c_ref)
        n_ref[...] = jnp.zeros_like(n_ref)
        m_ref[...] = jnp.zeros_like(m_ref)

    ti = lax.broadcasted_iota(jnp.int32, (L, L), 0)
    si = lax.broadcasted_iota(jnp.int32, (L, L), 1)
    causal = si <= ti
    tri = causal.astype(F32)

    def conv_act(x_ref, bb, part, out_scale):
        cols = slice(part * M_QK_WIDTH, (part + 1) * M_QK_WIDTH)
        x = x_ref[bb]
        xx = jnp.concatenate([tail_ref[bb, :, cols], x], axis=0)
        y = cb_ref[:, cols]
        for j in range(M_CONV):
            off = 8 - (M_CONV - 1) + j
            y = y + cw_ref[j:j + 1, cols] * xx[off:off + L, :]
        tail_ref[bb, :, cols] = x[L - 8:, :]
        act = y * _sigmoid(y)
        return (act if out_scale == 1.0 else act * out_scale).astype(BF16)

    per_b = []
    for bb in range(NB):
        q_act = conv_act(mq_ref, bb, 0, scale)
        k_act = conv_act(mk_ref, bb, 1, 1.0)
        gi_c = mif_ref[bb] + ifb_ref[...]
        gi_r = mift_ref[bb] + ifbt_ref[...]
        bcum_c = jnp.dot(tri, _log_sigmoid(gi_c), precision=hp, preferred_element_type=F32)
        bcum_r = lax.dot_general(_log_sigmoid(gi_r), tri, (((1,), (1,)), ((), ())), precision=hp,
                                 preferred_element_type=F32)
        per_b.append((q_act, k_act, gi_c, gi_r, bcum_c, bcum_r))

    chains = [(bb, hd) for bb in range(NB) for hd in range(M_HEADS)]

    ph1 = []
    for bb, hd in chains:
        q_act, k_act, gi_c, gi_r, bcum_c, bcum_r = per_b[bb]
        st = bb * M_HEADS + hd
        q = q_act[:, hd * M_QK_DIM:(hd + 1) * M_QK_DIM]
        k = k_act[:, hd * M_QK_DIM:(hd + 1) * M_QK_DIM]
        b_c = bcum_c[:, M_HEADS + hd:M_HEADS + hd + 1]
        i_c = gi_c[:, hd:hd + 1]
        b_r = bcum_r[M_HEADS + hd:M_HEADS + hd + 1, :]
        i_r = gi_r[hd:hd + 1, :]
        m_prev = m_ref[st]
        dmat = jnp.where(causal, b_c + (i_r - b_r), NEG)
        inter = b_c + m_prev
        m_t = jnp.maximum(inter, jnp.max(dmat, axis=-1, keepdims=True))
        w_intra = jnp.exp(dmat - m_t)
        w_inter = jnp.exp(inter - m_t)
        qk = lax.dot_general(q, k, (((1,), (1,)), ((), ())), preferred_element_type=F32) * w_intra
        ph1.append((q, k, b_c, i_c, m_prev, m_t, w_inter, qk))

    ph2 = []
    for (bb, hd), (q, k, b_c, i_c, m_prev, m_t, w_inter, qk) in zip(chains, ph1):
        st = bb * M_HEADS + hd
        vcols = slice((hd % hpb) * M_V_DIM, (hd % hpb + 1) * M_V_DIM)
        v = mv_refs[hd // hpb][bb, :, vcols].astype(BF16)
        c_prev = c_ref[st]
        n_prev = n_ref[st]
        num = w_inter * jnp.dot(q, c_prev.astype(BF16), preferred_element_type=F32) \
            + jnp.dot(qk.astype(BF16), v, preferred_element_type=F32)
        den = w_inter * jnp.sum(q.astype(F32) * n_prev, axis=-1, keepdims=True) \
            + jnp.sum(qk, axis=-1, keepdims=True)
        hh = num / jnp.maximum(jnp.abs(den), jnp.exp(-m_t))
        ph2.append((v, c_prev, n_prev, hh))

    for (bb, hd), (q, k, b_c, i_c, m_prev, m_t, w_inter, qk), (v, c_prev, n_prev, hh) in zip(chains, ph1, ph2):
        st = bb * M_HEADS + hd
        b_last = b_c[L - 1:L, :]
        w_log = b_last - b_c + i_c
        m_new = jnp.maximum(b_last + m_prev, jnp.max(w_log, axis=0, keepdims=True))
        wk = jnp.exp(w_log - m_new)
        decay = jnp.exp(b_last + m_prev - m_new)
        kw = (k.astype(F32) * wk)
        c_ref[st] = decay * c_prev + lax.dot_general(kw.astype(BF16), v, (((0,), (0,)), ((), ())),
                                                     preferred_element_type=F32)
        n_ref[st] = decay * n_prev + jnp.sum(kw, axis=0, keepdims=True)
        m_ref[st] = m_new

    for (bb, hd), (v, c_prev, n_prev, hh) in zip(chains, ph2):
        vcols = slice((hd % hpb) * M_V_DIM, (hd % hpb + 1) * M_V_DIM)
        mu = jnp.mean(hh, axis=-1, keepdims=True)
        hc = hh - mu
        var = jnp.mean(hc * hc, axis=-1, keepdims=True)
        hn = hc * lax.rsqrt(var + LN_EPS) * nw_ref[:, hd * M_V_DIM:(hd + 1) * M_V_DIM]
        og = _sigmoid(mo_refs[hd // hpb][bb, :, vcols])
        o_ref[bb, :, hd * M_V_DIM:(hd + 1) * M_V_DIM] = (hn * og).astype(o_ref.dtype)


def _mlstm(proj_a, mif, mif_t, if_bias, conv_w, conv_b, norm_w, B, S):
    L = M_CHUNK
    W = 2 * M_QK_WIDTH
    cw = M_MLSTM_COLS
    ifb = jnp.zeros((1, ROUTE_LANES), F32).at[0, :2 * M_HEADS].set(if_bias)
    ifbt = jnp.broadcast_to(if_bias.reshape(2 * M_HEADS, 1), (2 * M_HEADS, L))

    def col_spec(col0):
        assert col0 % cw == 0
        return pl.BlockSpec((B, L, cw), lambda c, col0=col0: (0, c, col0 // cw))

    const = lambda c: (0, 0)
    return pl.pallas_call(
        _mlstm_kernel,
        grid=(S // L,),
        in_specs=[col_spec(COL_MQK), col_spec(COL_MQK + M_QK_WIDTH),
                  col_spec(COL_MV), col_spec(COL_MV + cw),
                  col_spec(COL_MO), col_spec(COL_MO + cw),
                  pl.BlockSpec((B, L, ROUTE_LANES), lambda c: (0, c, 0)),
                  pl.BlockSpec((B, 2 * M_HEADS, L), lambda c: (0, 0, c)),
                  pl.BlockSpec((1, ROUTE_LANES), const),
                  pl.BlockSpec((2 * M_HEADS, L), const),
                  pl.BlockSpec((M_CONV, W), const),
                  pl.BlockSpec((1, W), const),
                  pl.BlockSpec((1, M_V_WIDTH), const)],
        out_specs=pl.BlockSpec((B, L, M_V_WIDTH), lambda c: (0, c, 0)),
        out_shape=jax.ShapeDtypeStruct((B, S, M_V_WIDTH), BF16),
        scratch_shapes=[pltpu.VMEM((B, 8, W), F32),
                        pltpu.VMEM((B * M_HEADS, M_QK_DIM, M_V_DIM), F32),
                        pltpu.VMEM((B * M_HEADS, 1, M_QK_DIM), F32),
                        pltpu.VMEM((B * M_HEADS, 1, 1), F32)],
        compiler_params=_params("arbitrary"),
        name="mlstm",
    )(proj_a, proj_a, proj_a, proj_a, proj_a, proj_a, mif, mif_t, ifb, ifbt, conv_w, conv_b.reshape(1, W),
      norm_w.reshape(1, M_V_WIDTH))


def _merge_kernel(att_ref, hm_ref, gate_ref, x_ref, g0_ref, b0_ref, wpa_ref, wpm_ref, wo_ref, g_ref, b_ref,
                  wrh_ref, wrl_ref, br_ref, h1_ref, lg_ref):
    halves = [pl.ds(k * MERGE_HALF, MERGE_HALF) for k in range(att_ref.shape[0] // MERGE_HALF)]
    proj = []
    for rs in halves:
        pa = jnp.dot(att_ref[rs, :], wpa_ref[...], preferred_element_type=F32)
        pm = jnp.dot(hm_ref[rs, :], wpm_ref[...], preferred_element_type=F32)
        proj.append((pa, pm))
    ys = []
    for rs, (pa, pm) in zip(halves, proj):
        ga = _sigmoid(gate_ref[rs, :D_MODEL].astype(F32))
        gm = _sigmoid(gate_ref[rs, D_MODEL:].astype(F32))
        merged = (ga * pa + gm * pm).astype(BF16)
        ys.append(jnp.dot(merged, wo_ref[...], preferred_element_type=F32))
    for rs, y in zip(halves, ys):
        h = _layer_norm_rows(x_ref[rs, :], g0_ref[...], b0_ref[...])
        h1 = _layer_norm_rows(DEEPNORM_ALPHA * h + y, g_ref[...], b_ref[...])
        h1_ref[rs, :] = h1
        h1h = h1.astype(BF16)
        h1l = (h1 - h1h.astype(F32)).astype(BF16)
        lg_ref[rs, :] = (jnp.dot(h1h, wrh_ref[...], preferred_element_type=F32)
                         + jnp.dot(h1l, wrh_ref[...], preferred_element_type=F32)
                         + jnp.dot(h1h, wrl_ref[...], preferred_element_type=F32)) + br_ref[...]


MERGE_HALF = 256


def _merge(att, hm, gate, x2, g0, b0, wpa, wpm, wo, g1, b1, wr, br, tm=512):
    T, D = x2.shape
    const = lambda i: (0, 0)
    one = pl.Buffered(1)
    wrh = wr.astype(BF16)
    wrl = (wr - wrh.astype(F32)).astype(BF16)
    return pl.pallas_call(
        _merge_kernel,
        grid=(T // tm,),
        in_specs=[pl.BlockSpec((tm, ATT_OUT_WIDTH), lambda i: (i, 0)),
                  pl.BlockSpec((tm, M_V_WIDTH), lambda i: (i, 0)),
                  pl.BlockSpec((tm, N_BRANCHES * D), lambda i: (i, 0)),
                  pl.BlockSpec((tm, D), lambda i: (i, 0)),
                  pl.BlockSpec((1, D), const),
                  pl.BlockSpec((1, D), const),
                  pl.BlockSpec((ATT_OUT_WIDTH, D), const, pipeline_mode=one),
                  pl.BlockSpec((M_V_WIDTH, D), const, pipeline_mode=one),
                  pl.BlockSpec((D, D), const, pipeline_mode=one),
                  pl.BlockSpec((1, D), const),
                  pl.BlockSpec((1, D), const),
                  pl.BlockSpec((D, ROUTE_LANES), const, pipeline_mode=one),
                  pl.BlockSpec((D, ROUTE_LANES), const, pipeline_mode=one),
                  pl.BlockSpec((1, ROUTE_LANES), const)],
        out_specs=[pl.BlockSpec((tm, D), lambda i: (i, 0)),
                   pl.BlockSpec((tm, ROUTE_LANES), lambda i: (i, 0))],
        out_shape=[jax.ShapeDtypeStruct((T, D), F32), jax.ShapeDtypeStruct((T, ROUTE_LANES), F32)],
        compiler_params=_params("parallel"),
        name="merge_out_ln1",
    )(att, hm, gate, x2, g0.reshape(1, D), b0.reshape(1, D), wpa, wpm, wo, g1.reshape(1, D), b1.reshape(1, D),
      wrh, wrl, br)


def _route_kernel(lg_ref, e_ref, w_ref):
    lg = lg_ref[...]
    col = lax.broadcasted_iota(jnp.int32, lg.shape, 1)
    big = jnp.int32(ROUTE_LANES)

    def first_argmax(v, vmax):
        return jnp.min(jnp.where(v == vmax, col, big), axis=-1, keepdims=True)

    gl = jnp.where(col < N_GROUPS, lg, NEG)
    gmax = jnp.max(gl, axis=-1, keepdims=True)
    grp = first_argmax(gl, gmax)
    gsum = jnp.sum(jnp.where(col < N_GROUPS, jnp.exp(lg - gmax), 0.0), axis=-1, keepdims=True)
    g_w = 1.0 / gsum
    ecol = col - N_GROUPS
    egrp = lax.shift_right_arithmetic(ecol, int(math.log2(EXPERTS_PER_GROUP)))
    in_grp = (ecol >= 0) & (ecol < N_EXPERTS) & (egrp == grp)
    el = jnp.where(in_grp, lg, NEG)
    v1 = jnp.max(el, axis=-1, keepdims=True)
    i1 = first_argmax(el, v1)
    el2 = jnp.where(col == i1, NEG, el)
    v2 = jnp.max(el2, axis=-1, keepdims=True)
    i2 = first_argmax(el2, v2)
    t = jnp.exp(v2 - v1)
    p1 = 1.0 / (1.0 + t)
    p2 = t / (1.0 + t)
    e_ref[...] = jnp.where(col == 0, i1 - N_GROUPS, jnp.where(col == 1, i2 - N_GROUPS, 0))
    w_ref[...] = jnp.where(col == 0, g_w * p1, jnp.where(col == 1, g_w * p2, 0.0))


def _route(logits, tm=1024):
    T = logits.shape[0]
    spec = pl.BlockSpec((tm, ROUTE_LANES), lambda i: (i, 0))
    return pl.pallas_call(
        _route_kernel,
        grid=(T // tm,),
        in_specs=[spec],
        out_specs=[spec, spec],
        out_shape=[jax.ShapeDtypeStruct((T, ROUTE_LANES), jnp.int32),
                   jax.ShapeDtypeStruct((T, ROUTE_LANES), F32)],
        compiler_params=_params("parallel"),
        name="route",
    )(logits)


def _dispatch_plan(e_tk, T):
    M = T * TOP_K
    e_flat = e_tk.reshape(M)
    onehot = (e_flat[:, None] == jnp.arange(N_EXPERTS, dtype=jnp.int32)[None, :]).astype(jnp.int32)
    csum = jnp.cumsum(onehot, axis=0)
    counts = csum[-1]
    rank = jnp.sum((csum - onehot) * onehot, axis=1)
    padded = (counts + MOE_SUB - 1) // MOE_SUB * MOE_SUB
    pstart = jnp.cumsum(padded) - padded
    dest = jnp.sum(onehot * pstart[None, :], axis=1) + rank

    nsb_max = N_EXPERTS + M // MOE_SUPER
    nsb_e = (padded + MOE_SUPER - 1) // MOE_SUPER
    sb_end = jnp.cumsum(nsb_e)
    sb_beg = sb_end - nsb_e
    total = sb_end[-1]
    sb = jnp.arange(nsb_max, dtype=jnp.int32)
    sb_c = jnp.minimum(sb, total - 1)
    ex = jnp.sum((sb_end[None, :] <= sb_c[:, None]).astype(jnp.int32), axis=1)
    local = sb_c - sb_beg[ex]
    row0 = pstart[ex] + local * MOE_SUPER
    active = sb < total
    cnt = jnp.where(active, jnp.clip(counts[ex] - local * MOE_SUPER, 0, MOE_SUPER), 0)
    nsub = jnp.where(active, jnp.clip(padded[ex] - local * MOE_SUPER, 0, MOE_SUPER) // MOE_SUB, 0)
    return (ex.astype(jnp.int32), row0.astype(jnp.int32), cnt.astype(jnp.int32), nsub.astype(jnp.int32),
            dest.astype(jnp.int32), jnp.sum(padded).astype(jnp.int32).reshape(1))


def _moe_kernel(sb_ex, sb_row0, sb_cnt, sb_nsub, dest, used_rows,
                h1_hbm, wga_ref, wua_ref, wda_ref, wgb_ref, wub_ref, wdb_ref, ys_hbm,
                stage_buf, acc_buf, slot_tok, gsem, ssem):
    b = pl.program_id(0)
    j = pl.program_id(1)
    nb = pl.num_programs(0)
    nsub = sb_nsub[b]
    cnt = sb_cnt[b]
    slot = lax.rem(b, 2)
    U = MOE_DMA_UNROLL

    def gather_batches(bb):
        return (sb_cnt[bb] + (U - 1)) // U

    def gather_issue(bb):
        r0 = sb_row0[bb]

        def pad_row(i, c):
            slot_tok[r0 + i] = 0
            return c
        lax.fori_loop(sb_cnt[bb], gather_batches(bb) * U, pad_row, 0)

        def issue(q, c):
            i0 = pl.multiple_of(q * U, U)
            for k in range(U):
                tok = slot_tok[r0 + i0 + k]
                pltpu.make_async_copy(h1_hbm.at[lax.shift_right_logical(tok, 3), pl.ds(tok & 7, 1), :],
                                      stage_buf.at[half, q * (U // 8) + k // 8, pl.ds(k % 8, 1), :], gsem).start()
            return c
        half = lax.rem(bb, 2)
        lax.fori_loop(0, gather_batches(bb), issue, 0)

    def gather_wait(bb):
        def wait(k, c):
            pltpu.make_async_copy(h1_hbm.at[pl.ds(0, U // 8)], stage_buf.at[0, pl.ds(0, U // 8)], gsem).wait()
            return c
        lax.fori_loop(0, gather_batches(bb), wait, 0)

    def build_tables():
        n_asg = dest.shape[0]
        step = 16

        def fill(q, c):
            for k in range(step):
                slot_tok[dest[q * step + k]] = q * (step // TOP_K) + k // TOP_K
            return c
        lax.fori_loop(0, n_asg // step, fill, 0)

    def out_copy(k, r0):
        rs = pl.ds(pl.multiple_of(k * MOE_SUB, MOE_SUB), MOE_SUB)
        return pltpu.make_async_copy(acc_buf.at[slot, rs, :],
                                     ys_hbm.at[pl.ds(pl.multiple_of(r0 + k * MOE_SUB, MOE_SUB), MOE_SUB), :], ssem)

    def out_issue():
        r0 = sb_row0[b]

        def issue(k, c):
            out_copy(k, r0).start()
            return c
        lax.fori_loop(0, nsub, issue, 0)

    def out_wait(n_sub):
        def wait(k, c):
            out_copy(0, 0).wait()
            return c
        lax.fori_loop(0, n_sub, wait, 0)

    @pl.when(j == 0)
    def _first_step():
        @pl.when(b == 0)
        def _():
            build_tables()

            def clear(q, c):
                for half in range(2):
                    stage_buf[half, q] = jnp.zeros((8, D_MODEL), F32)
                return c
            lax.fori_loop(0, stage_buf.shape[1], clear, 0)
            gather_issue(0)

        @pl.when(nsub > 0)
        def _():
            gather_wait(b)

            def clear_acc(k, c):
                rs = pl.ds(pl.multiple_of(k * MOE_SUB, MOE_SUB), MOE_SUB)
                acc_buf[slot, rs, :] = jnp.zeros((MOE_SUB, D_MODEL), F32)
                return c
            lax.fori_loop(0, nsub, clear_acc, 0)

        nxt = jnp.minimum(b + 1, nb - 1)

        @pl.when((b + 1 < nb) & (sb_nsub[nxt] > 0))
        def _():
            gather_issue(nxt)

    def ffn_tile(wg_ref, wu_ref, wd_ref):
        def chunk(r0, rows):
            rs = pl.ds(r0, rows)
            t0 = r0 // 8 if isinstance(r0, int) else pl.multiple_of(r0 // 8, MOE_SUB // 8)
            tiles = pl.ds(t0, rows // 8)
            x = stage_buf[slot, tiles].reshape(rows, D_MODEL).astype(BF16)
            gt = jnp.dot(x, wg_ref[0].astype(BF16), preferred_element_type=F32)
            ut = jnp.dot(x, wu_ref[0].astype(BF16), preferred_element_type=F32)
            hmid = (gt * _sigmoid(gt) * ut).astype(BF16)
            acc_buf[slot, rs, :] += jnp.dot(hmid, wd_ref[0].astype(BF16), preferred_element_type=F32)

        whole = [n for n in range(MOE_CHUNK // MOE_SUB + 1, MOE_WHOLE_MAX // MOE_SUB + 1)]
        is_whole = functools.reduce(jnp.logical_or, [nsub == n for n in whole])
        for n in whole:
            @pl.when(nsub == n)
            def _(n=n):
                chunk(0, n * MOE_SUB)

        @pl.when(jnp.logical_not(is_whole))
        def _():
            per = MOE_CHUNK // MOE_SUB
            nfull = nsub // per

            def full(k, c):
                chunk(pl.multiple_of(k * MOE_CHUNK, MOE_CHUNK), MOE_CHUNK)
                return c
            lax.fori_loop(0, nfull, full, 0)
            rem = nsub - nfull * per
            base = nfull * MOE_CHUNK
            size = MOE_CHUNK // 2
            while size >= MOE_SUB:
                units = size // MOE_SUB

                @pl.when(lax.rem(rem, 2 * units) >= units)
                def _(size=size, units=units):
                    skipped = (rem // (2 * units)) * (2 * units)
                    chunk(pl.multiple_of(base + skipped * MOE_SUB, size), size)
                size //= 2

    @pl.when(nsub > 0)
    def _tiles():
        ffn_tile(wga_ref, wua_ref, wda_ref)

        @pl.when(j == MOE_NJ - 1)
        def _():
            ffn_tile(wgb_ref, wub_ref, wdb_ref)

    @pl.when(j == MOE_NJ - 1)
    def _last_step():
        prev = jnp.maximum(b - 1, 0)

        @pl.when((b > 0) & (sb_nsub[prev] > 0))
        def _():
            out_wait(sb_nsub[prev])

        @pl.when(nsub > 0)
        def _():
            out_issue()

        @pl.when((b == nb - 1) & (nsub > 0))
        def _():
            out_wait(nsub)

        @pl.when(b == nb - 1)
        def _():
            acc_buf[0, pl.ds(0, MOE_SUB), :] = jnp.zeros((MOE_SUB, D_MODEL), F32)

            def fill(k, c):
                rows = pl.ds(pl.multiple_of(k * MOE_SUB, MOE_SUB), MOE_SUB)
                pltpu.make_async_copy(acc_buf.at[0, pl.ds(0, MOE_SUB), :], ys_hbm.at[rows, :], ssem).start()
                return c
            first = used_rows[0] // MOE_SUB
            total = ys_hbm.shape[0] // MOE_SUB
            lax.fori_loop(first, total, fill, 0)

            def drain(k, c):
                out_copy(0, 0).wait()
                return c
            lax.fori_loop(first, total, drain, 0)


def _moe_ffn(h1, plan, w_gate, w_up, w_down):
    T, D = h1.shape
    sb_ex, sb_row0, sb_cnt, sb_nsub, dest, used_rows = plan
    nsb_max = sb_ex.shape[0]
    n_rows = dest.shape[0] + N_EXPERTS * MOE_SUB
    last = MOE_NFT - 1
    tail = D_FF_EXPERT // MOE_FT_TAIL - 1

    def ja(b, j, nsub):
        return jnp.where(nsub[b] > 0, jnp.minimum(j, last), last)

    def main_cols(b, j, ex, r0, ct, ns, ds, ur):
        return (ex[b], 0, ja(b, j, ns))

    def main_rows(b, j, ex, r0, ct, ns, ds, ur):
        return (ex[b], ja(b, j, ns), 0)

    def tail_cols(b, j, ex, r0, ct, ns, ds, ur):
        return (ex[b], 0, tail)

    def tail_rows(b, j, ex, r0, ct, ns, ds, ur):
        return (ex[b], tail, 0)

    grid_spec = pltpu.PrefetchScalarGridSpec(
        num_scalar_prefetch=6,
        grid=(nsb_max, MOE_NJ),
        in_specs=[pl.BlockSpec(memory_space=pl.ANY),
                  pl.BlockSpec((1, D, MOE_FT), main_cols),
                  pl.BlockSpec((1, D, MOE_FT), main_cols),
                  pl.BlockSpec((1, MOE_FT, D), main_rows),
                  pl.BlockSpec((1, D, MOE_FT_TAIL), tail_cols),
                  pl.BlockSpec((1, D, MOE_FT_TAIL), tail_cols),
                  pl.BlockSpec((1, MOE_FT_TAIL, D), tail_rows)],
        out_specs=pl.BlockSpec(memory_space=pl.ANY),
        scratch_shapes=[pltpu.VMEM((2, MOE_SUPER // 8, 8, D), F32),
                        pltpu.VMEM((2, MOE_SUPER, D), F32),
                        pltpu.SMEM((n_rows,), jnp.int32),
                        pltpu.SemaphoreType.DMA(()),
                        pltpu.SemaphoreType.DMA(())],
    )
    return pl.pallas_call(
        _moe_kernel,
        grid_spec=grid_spec,
        out_shape=jax.ShapeDtypeStruct((n_rows, D), F32),
        compiler_params=_params("arbitrary", "arbitrary"),
        name="moe_experts",
    )(sb_ex, sb_row0, sb_cnt, sb_nsub, dest, used_rows, h1.reshape(T // 8, 8, D),
      w_gate, w_up, w_down, w_gate, w_up, w_down)


LN_OUT_ROWS = 512
LN_OUT_BATCH = 32
assert LN_OUT_BATCH % (8 * TOP_K) == 0 and (LN_OUT_ROWS * TOP_K) % LN_OUT_BATCH == 0


def _ln_out_kernel(dest, h1_ref, rw_ref, g_ref, b_ref, ys_hbm, o_ref, y_buf, sem):
    i = pl.program_id(0)
    nt = pl.num_programs(0)
    tm = LN_OUT_ROWS
    toks = LN_OUT_BATCH // TOP_K

    def gather_issue(tile, buf):
        base = tile * (tm * TOP_K)

        def issue(q, c):
            for k in range(LN_OUT_BATCH):
                d = dest[base + q * LN_OUT_BATCH + k]
                t = k // TOP_K
                pltpu.make_async_copy(ys_hbm.at[lax.shift_right_logical(d, 3), pl.ds(d & 7, 1), :],
                                      y_buf.at[buf, k % TOP_K, q * (toks // 8) + t // 8, pl.ds(t % 8, 1), :],
                                      sem.at[buf]).start()
            return c
        lax.fori_loop(0, tm * TOP_K // LN_OUT_BATCH, issue, 0)

    def gather_wait(buf):
        for s in range(TOP_K):
            pltpu.make_async_copy(ys_hbm.at[pl.ds(0, tm // 8)], y_buf.at[buf, s], sem.at[buf]).wait()

    @pl.when(i == 0)
    def _():
        gather_issue(0, 0)

    @pl.when(i + 1 < nt)
    def _():
        gather_issue(i + 1, lax.rem(i + 1, 2))

    cur = lax.rem(i, 2)
    gather_wait(cur)
    rw = rw_ref[...]
    z = DEEPNORM_ALPHA * h1_ref[...]
    for s in range(TOP_K):
        z = z + rw[:, s:s + 1] * y_buf[cur, s].reshape(tm, D_MODEL)
    o_ref[...] = _layer_norm_rows(z, g_ref[...], b_ref[...])


def _ln_out(h1, ys, dest, rw, g, b):
    T, D = h1.shape
    tm = LN_OUT_ROWS
    n_rows = ys.shape[0]
    grid_spec = pltpu.PrefetchScalarGridSpec(
        num_scalar_prefetch=1,
        grid=(T // tm,),
        in_specs=[pl.BlockSpec((tm, D), lambda i, ds: (i, 0)),
                  pl.BlockSpec((tm, ROUTE_LANES), lambda i, ds: (i, 0)),
                  pl.BlockSpec((1, D), lambda i, ds: (0, 0)),
                  pl.BlockSpec((1, D), lambda i, ds: (0, 0)),
                  pl.BlockSpec(memory_space=pl.ANY)],
        out_specs=pl.BlockSpec((tm, D), lambda i, ds: (i, 0)),
        scratch_shapes=[pltpu.VMEM((2, TOP_K, tm // 8, 8, D), F32),
                        pltpu.SemaphoreType.DMA((2,))],
    )
    return pl.pallas_call(
        _ln_out_kernel,
        grid_spec=grid_spec,
        out_shape=jax.ShapeDtypeStruct((T, D), F32),
        compiler_params=_params("arbitrary"),
        name="combine_ln2",
    )(dest, h1, rw, g.reshape(1, D), b.reshape(1, D), ys.reshape(n_rows // 8, 8, D))


def kernel(x, ln_in_g, ln_in_b, w_in, m_conv_w, m_conv_b, m_if_bias, m_norm_w, w_proj_att, w_proj_mlstm, w_out,
           ln1_g, ln1_b, w_router_group, b_router_group, w_router_expert, b_router_expert, w_gate, w_up, w_down,
           ln2_g, ln2_b):
    B, S, D = x.shape
    T = B * S
    assert D == D_MODEL and S % ATT_SUPER == 0 and w_in.shape[0] == DEPTH == 1

    x2 = x.reshape(T, D)
    for l in range(DEPTH):
        wt = jnp.swapaxes(w_in[l], 0, 1)
        hb, mif = _ln_in(x2, ln_in_g, ln_in_b, wt, COL_MIF)
        proj_a = _matmul_nt(hb, wt, 0, PROJ_A_WIDTH, 1024, 1280, F32, "in_proj_a")
        gate = _matmul_nt(hb, wt, COL_GATE, N_BRANCHES * D, 1024, 1024, BF16, "in_proj_gate")

        proj_a3 = proj_a.reshape(B, S, PROJ_A_WIDTH)
        att = _attention(proj_a3, B, S)
        mif3 = mif.reshape(B, S, ROUTE_LANES)
        mif_t = jnp.swapaxes(mif3[:, :, :2 * M_HEADS], 1, 2)
        hm = _mlstm(proj_a3, mif3, mif_t, m_if_bias[l], m_conv_w[l], m_conv_b[l], m_norm_w[l], B, S)

        lane_pad = ROUTE_LANES - N_GROUPS - N_EXPERTS
        w_r = jnp.pad(jnp.concatenate([w_router_group[l], w_router_expert[l]], axis=1), ((0, 0), (0, lane_pad)))
        b_r = jnp.pad(jnp.concatenate([b_router_group[l], b_router_expert[l]]), (0, lane_pad)).reshape(1, ROUTE_LANES)
        h1, logits = _merge(att.reshape(T, ATT_OUT_WIDTH), hm.reshape(T, M_V_WIDTH), gate, x2, ln_in_g, ln_in_b,
                            w_proj_att[l].astype(BF16), w_proj_mlstm[l].astype(BF16), w_out[l].astype(BF16),
                            ln1_g[l], ln1_b[l], w_r, b_r)

        e_out, rw = _route(logits)
        plan = _dispatch_plan(e_out[:, :TOP_K], T)
        ys = _moe_ffn(h1, plan, w_gate[l], w_up[l], w_down[l])
        h = _ln_out(h1, ys, plan[4], rw, ln2_g[l], ln2_b[l])
    return h.reshape(B, S, D)
```
